```python
import math
import jax, jax.numpy as jnp
from jax import lax
import numpy as np

D_MODEL = 1024
BATCH = 8
SEQ = 2048
DEPTH = 4
DEC_BATCH = 32
DEC_SEQ = 1
PAST_LEN = 16384
PAGE_SIZE = 128

N_A_LAYERS = DEPTH // 2
N_B_LAYERS = DEPTH - N_A_LAYERS
N_DENSE = (DEPTH + 1) // 2
N_MOE = DEPTH // 2
DN_HEADS = 6
DN_DK = 128
DN_DV = 128
DN_QK_DIM = DN_HEADS * DN_DK
DN_V_DIM = DN_HEADS * DN_DV
CONV_W = 4
CONV_DIM = 2 * DN_QK_DIM + DN_V_DIM
CHUNK = 64
MLA_HEADS = 6
Q_LORA = 384
KV_LORA = 256
QK_NOPE = 128
QK_ROPE = 64
V_HEAD = 128
ROPE_THETA = 10000.0
Q_BLOCK = 128
MLA_SCALE = (QK_NOPE + QK_ROPE) ** -0.5
N_MEM = 256
MEM_HEADS = 4
MEM_HD = 64
MEM_DIM = MEM_HEADS * MEM_HD
MIX_WIDTH = DN_V_DIM + MEM_DIM
IN_A = CONV_DIM + DN_V_DIM + 2 * DN_HEADS + MEM_DIM
IN_B = Q_LORA + MEM_DIM
D_FF = 2816
N_EXPERTS = 8
TOP_K = 2
D_FF_E = 1408
EPS = 1e-6

kernel_name = "yoco_gdn_mla_memory_decoder_step"

F32 = jnp.float32


def rmsnorm(x, g):
    xf = x.astype(F32)
    y = xf * lax.rsqrt(jnp.mean(xf * xf, -1, keepdims=True) + EPS)
    return (y * g.astype(F32)).astype(x.dtype)


def l2norm(x):
    xf = x.astype(F32)
    return (xf * lax.rsqrt(jnp.sum(xf * xf, -1, keepdims=True) + EPS)).astype(x.dtype)


def rope(x, pos):
    half = x.shape[-1] // 2
    inv = ROPE_THETA ** (-jnp.arange(half, dtype=F32) / half)
    ang = pos.astype(F32)[:, None] * inv[None, :]
    shp = (1, pos.shape[0]) + (1,) * (x.ndim - 3) + (half,)
    cos = jnp.cos(ang).reshape(shp)
    sin = jnp.sin(ang).reshape(shp)
    x1 = x[..., :half].astype(F32)
    x2 = x[..., half:].astype(F32)
    return jnp.concatenate([x1 * cos - x2 * sin, x1 * sin + x2 * cos], -1).astype(x.dtype)


def short_conv(u, prev, w):
    S = u.shape[1]
    up = jnp.concatenate([prev.astype(u.dtype), u], 1)
    y = up[:, 0:S] * w[0]
    for j in range(1, CONV_W):
        y = y + up[:, j:j + S] * w[j]
    return jax.nn.silu(y), up[:, -(CONV_W - 1):]


def gated_delta_rule(q, k, v, g, beta, s0):
    B, S, H, DK = q.shape
    DV = v.shape[-1]
    C = CHUNK
    pad = (-S) % C

    def prep(t):
        t = t.astype(F32)
        t = jnp.pad(t, [(0, 0), (0, pad)] + [(0, 0)] * (t.ndim - 2))
        n = t.shape[1] // C
        t = t.reshape((B, n, C) + t.shape[2:])
        return jnp.moveaxis(t, 3, 1)

    qc, kc, vc, gc, bc = (prep(q.astype(F32) * DK ** -0.5), prep(k), prep(v), prep(g), prep(beta))
    gcum = jnp.cumsum(gc, -1)
    tri = jnp.tril(jnp.ones((C, C), bool))
    eye = jnp.eye(C, dtype=F32)
    decay = jnp.exp(jnp.where(tri, gcum[..., :, None] - gcum[..., None, :], -jnp.inf))
    kb = kc * bc[..., None]
    vb = vc * bc[..., None]
    lower = jnp.einsum('bhnid,bhnjd->bhnij', kb, kc) * decay * (1.0 - eye)
    rhs = jnp.concatenate([vb, kb * jnp.exp(gcum)[..., None]], -1)
    sol = lax.linalg.triangular_solve(eye + lower, rhs, left_side=True, lower=True, unit_diagonal=True)
    u = sol[..., :DV]
    w = sol[..., DV:]
    qk = jnp.einsum('bhnid,bhnjd->bhnij', qc, kc) * decay
    q_g = qc * jnp.exp(gcum)[..., None]
    g_last = gcum[..., -1]
    k_g = kc * jnp.exp(g_last[..., None] - gcum)[..., None]

    def step(st, inp):
        qk_n, qg_n, w_n, u_n, kg_n, gl_n = inp
        v_new = u_n - jnp.einsum('bhcd,bhde->bhce', w_n, st)
        o = jnp.einsum('bhcd,bhde->bhce', qg_n, st) + jnp.einsum('bhij,bhje->bhie', qk_n, v_new)
        st = st * jnp.exp(gl_n)[..., None, None] + jnp.einsum('bhcd,bhce->bhde', kg_n, v_new)
        return st, o

    xs = tuple(jnp.moveaxis(t, 2, 0) for t in (qk, q_g, w, u, k_g, g_last))
    s_fin, o = lax.scan(step, s0.astype(F32), xs)
    n = o.shape[0]
    o = jnp.moveaxis(o, 0, 2).reshape(B, H, n * C, DV)[:, :, :S]
    return jnp.transpose(o, (0, 2, 1, 3)).astype(v.dtype), s_fin


def deltanet_mixer(h, w_in, conv_w, a_log, dt_bias, g_onorm, conv_prev, s_prev):
    B, S, _ = h.shape
    proj = h @ w_in
    o1 = CONV_DIM
    o2 = o1 + DN_V_DIM
    o3 = o2 + DN_HEADS
    o4 = o3 + DN_HEADS
    qkv, conv_new = short_conv(proj[..., :o1], conv_prev, conv_w)
    z = proj[..., o1:o2].reshape(B, S, DN_HEADS, DN_DV)
    b = proj[..., o2:o3]
    a = proj[..., o3:o4]
    mq = proj[..., o4:]
    q = l2norm(qkv[..., :DN_QK_DIM].reshape(B, S, DN_HEADS, DN_DK))
    k = l2norm(qkv[..., DN_QK_DIM:2 * DN_QK_DIM].reshape(B, S, DN_HEADS, DN_DK))
    v = qkv[..., 2 * DN_QK_DIM:].reshape(B, S, DN_HEADS, DN_DV)
    beta = jax.nn.sigmoid(b.astype(F32))
    g = -jnp.exp(a_log.astype(F32)) * jax.nn.softplus(a.astype(F32) + dt_bias.astype(F32))
    o, s_new = gated_delta_rule(q, k, v, g, beta, s_prev)
    o = rmsnorm(o, g_onorm) * jax.nn.silu(z)
    return o.reshape(B, S, DN_V_DIM), mq, conv_new, s_new.astype(s_prev.dtype)


def shared_latent_kv(x, pos, g_kv, w_kv_a, g_ckv):
    kv = rmsnorm(x, g_kv) @ w_kv_a
    ckv = rmsnorm(kv[..., :KV_LORA], g_ckv)
    kpe = rope(kv[..., KV_LORA:], pos)
    return ckv, kpe


def mla_prompt_attend(q_lat, q_pe, ckv, kpe):
    B, S, H, C = q_lat.shape
    nb = S // Q_BLOCK
    kpos = jnp.arange(S)

    def blk(i):
        ql = lax.dynamic_slice_in_dim(q_lat, i * Q_BLOCK, Q_BLOCK, 1)
        qp = lax.dynamic_slice_in_dim(q_pe, i * Q_BLOCK, Q_BLOCK, 1)
        s = (jnp.einsum('bqhc,bkc->bhqk', ql, ckv).astype(F32)
             + jnp.einsum('bqhr,bkr->bhqk', qp, kpe).astype(F32)) * MLA_SCALE
        qpos = i * Q_BLOCK + jnp.arange(Q_BLOCK)
        s = jnp.where(kpos[None, :] <= qpos[:, None], s, -jnp.inf)
        p = jax.nn.softmax(s, -1).astype(ckv.dtype)
        return jnp.einsum('bhqk,bkc->bqhc', p, ckv)

    o = lax.map(blk, jnp.arange(nb))
    return jnp.moveaxis(o, 0, 1).reshape(B, S, H, C)


def mla_sample_attend(q_lat, q_pe, ckv_past, kpe_past, ckv_new, kpe_new):
    Sd = q_lat.shape[1]
    P = ckv_past.shape[1]
    s_past = (jnp.einsum('bqhc,bkc->bhqk', q_lat, ckv_past).astype(F32)
              + jnp.einsum('bqhr,bkr->bhqk', q_pe, kpe_past).astype(F32))
    s_new = (jnp.einsum('bqhc,bkc->bhqk', q_lat, ckv_new).astype(F32)
             + jnp.einsum('bqhr,bkr->bhqk', q_pe, kpe_new).astype(F32))
    s_new = jnp.where(jnp.tril(jnp.ones((Sd, Sd), bool)), s_new, -jnp.inf)
    p = jax.nn.softmax(jnp.concatenate([s_past, s_new], -1) * MLA_SCALE, -1).astype(ckv_past.dtype)
    return (jnp.einsum('bhqk,bkc->bqhc', p[..., :P], ckv_past)
            + jnp.einsum('bhqk,bkc->bqhc', p[..., P:], ckv_new))


def mla_mixer(h, pos, ckv, kpe, w_in, g_ql, w_qb, w_uk, w_uv, attend):
    B, S, _ = h.shape
    proj = h @ w_in
    ql = proj[..., :Q_LORA]
    mq = proj[..., Q_LORA:]
    qf = (rmsnorm(ql, g_ql) @ w_qb).reshape(B, S, MLA_HEADS, QK_NOPE + QK_ROPE)
    q_pe = rope(qf[..., QK_NOPE:], pos)
    q_lat = jnp.einsum('bshn,chn->bshc', qf[..., :QK_NOPE], w_uk)
    o_lat = attend(q_lat, q_pe, ckv, kpe)
    o = jnp.einsum('bshc,chv->bshv', o_lat, w_uv).reshape(B, S, MLA_HEADS * V_HEAD)
    return o, mq


def memory_kv(mem, g_mem, w_mem_kv):
    B, M, _ = mem.shape
    L = g_mem.shape[0]
    mf = mem.astype(F32)
    mn = (mf * lax.rsqrt(jnp.mean(mf * mf, -1, keepdims=True) + EPS)).astype(mem.dtype)
    kv = jnp.einsum('bmd,ld,lde->lbme', mn, g_mem, w_mem_kv)
    k = kv[..., :MEM_DIM].reshape(L, B, M, MEM_HEADS, MEM_HD)
    v = kv[..., MEM_DIM:].reshape(L, B, M, MEM_HEADS, MEM_HD)
    return k, v


def mem_attend(mq, mk, mv):
    B, S, _ = mq.shape
    q = mq.reshape(B, S, MEM_HEADS, MEM_HD)
    s = jnp.einsum('bqhd,bkhd->bhqk', q, mk.astype(q.dtype)).astype(F32) * MEM_HD ** -0.5
    p = jax.nn.softmax(s, -1).astype(q.dtype)
    return jnp.einsum('bhqk,bkhd->bqhd', p, mv.astype(q.dtype)).reshape(B, S, MEM_DIM)


def swiglu(h, wg, wu, wd):
    return (jax.nn.silu(h @ wg) * (h @ wu)) @ wd


def moe(h, w_router, wg, wu, wd):
    probs = jax.nn.softmax((h @ w_router).astype(F32), -1)
    top_p, top_i = lax.top_k(probs, TOP_K)
    top_p = top_p / jnp.sum(top_p, -1, keepdims=True)
    gates = jnp.sum(jax.nn.one_hot(top_i, N_EXPERTS, dtype=F32) * top_p[..., None], -2).astype(h.dtype)
    out = jnp.zeros_like(h)
    for e in range(N_EXPERTS):
        out = out + gates[..., e:e + 1] * swiglu(h, wg[e], wu[e], wd[e])
    return out


def run_trunk(x, pos, mem_k, mem_v, conv_prev, dn_prev, attend_b, P):
    conv_states, dn_states = [], []
    ckv = None
    kpe = None
    for l in range(DEPTH):
        if l == N_A_LAYERS:
            ckv, kpe = shared_latent_kv(x, pos, P['g_kv'], P['w_kv_a'], P['g_ckv'])
        h = rmsnorm(x, P['g_mix'][l])
        if l < N_A_LAYERS:
            o_tok, mq, c_new, s_new = deltanet_mixer(h, P['w_in_a'][l], P['conv_w'][l], P['a_log'][l],
                                                     P['dt_bias'][l], P['g_onorm'][l], conv_prev[l], dn_prev[l])
            conv_states.append(c_new)
            dn_states.append(s_new)
        else:
            j = l - N_A_LAYERS
            o_tok, mq = mla_mixer(h, pos, ckv, kpe, P['w_in_b'][j], P['g_qlora'][j], P['w_q_b'][j],
                                  P['w_uk'], P['w_uv'], attend_b)
        o_mem = mem_attend(mq, mem_k[l], mem_v[l])
        x = x + jnp.concatenate([o_tok, o_mem], -1) @ P['w_out'][l]
        h = rmsnorm(x, P['g_ffn'][l])
        i = l // 2
        if l % 2 == 0:
            x = x + swiglu(h, P['w_gate'][i], P['w_up'][i], P['w_down'][i])
        else:
            x = x + moe(h, P['w_router'][i], P['we_gate'][i], P['we_up'][i], P['we_down'][i])
    y = rmsnorm(x, P['g_final'])
    return y, jnp.stack(conv_states), jnp.stack(dn_states), ckv, kpe


def setup_inputs(seed: int = 0) -> dict:
    key = jax.random.key(seed)
    ks = iter(jax.random.split(key, 64))

    def nrm(shape, scale):
        return jax.random.normal(next(ks), shape, F32) * scale

    def gain(shape):
        return 1.0 + nrm(shape, 0.02)

    n_pages = PAST_LEN // PAGE_SIZE
    n_used = DEC_BATCH * n_pages
    n_pool = n_used + n_used // 4
    inp = {}
    inp['x_prompt'] = nrm((BATCH, SEQ, D_MODEL), 1.0)
    inp['x_sample'] = nrm((DEC_BATCH, DEC_SEQ, D_MODEL), 1.0)
    inp['cache_mem_k'] = nrm((DEPTH, DEC_BATCH, N_MEM, MEM_HEADS, MEM_HD), 1.0)
    inp['cache_mem_v'] = nrm((DEPTH, DEC_BATCH, N_MEM, MEM_HEADS, MEM_HD), 1.0)
    inp['cache_ckv'] = nrm((n_pool, PAGE_SIZE, KV_LORA), 1.0)
    inp['cache_kpe'] = nrm((n_pool, PAGE_SIZE, QK_ROPE), 1.0)
    inp['state_delta'] = nrm((N_A_LAYERS, DEC_BATCH, DN_HEADS, DN_DK, DN_DV), 0.1)
    inp['state_conv'] = nrm((N_A_LAYERS, DEC_BATCH, CONV_W - 1, CONV_DIM), 1.0)
    inp['page_table'] = jax.random.permutation(next(ks), n_pool)[:n_used].reshape(DEC_BATCH, n_pages).astype(jnp.int32)
    inp['mem_prompt'] = nrm((BATCH, N_MEM, D_MODEL), 1.0)
    inp['g_mix'] = gain((DEPTH, D_MODEL))
    inp['g_ffn'] = gain((DEPTH, D_MODEL))
    inp['g_final'] = gain((D_MODEL,))
    inp['w_in_a'] = nrm((N_A_LAYERS, D_MODEL, IN_A), D_MODEL ** -0.5)
    inp['conv_w'] = nrm((N_A_LAYERS, CONV_W, CONV_DIM), CONV_W ** -0.5)
    inp['a_log'] = jnp.log(jax.random.uniform(next(ks), (N_A_LAYERS, DN_HEADS), F32, 1.0, 16.0))
    dt = jnp.exp(jax.random.uniform(next(ks), (N_A_LAYERS, DN_HEADS), F32, math.log(1e-3), math.log(1e-1)))
    inp['dt_bias'] = dt + jnp.log(-jnp.expm1(-dt))
    inp['g_onorm'] = gain((N_A_LAYERS, DN_DV))
    inp['w_in_b'] = nrm((N_B_LAYERS, D_MODEL, IN_B), D_MODEL ** -0.5)
    inp['g_qlora'] = gain((N_B_LAYERS, Q_LORA))
    inp['w_q_b'] = nrm((N_B_LAYERS, Q_LORA, MLA_HEADS * (QK_NOPE + QK_ROPE)), Q_LORA ** -0.5)
    inp['g_kv'] = gain((D_MODEL,))
    inp['w_kv_a'] = nrm((D_MODEL, KV_LORA + QK_ROPE), D_MODEL ** -0.5)
    inp['g_ckv'] = gain((KV_LORA,))
    inp['w_uk'] = nrm((KV_LORA, MLA_HEADS, QK_NOPE), KV_LORA ** -0.5)
    inp['w_uv'] = nrm((KV_LORA, MLA_HEADS, V_HEAD), KV_LORA ** -0.5)
    inp['g_mem'] = gain((DEPTH, D_MODEL))
    inp['w_mem_kv'] = nrm((DEPTH, D_MODEL, 2 * MEM_DIM), D_MODEL ** -0.5)
    inp['w_out'] = nrm((DEPTH, MIX_WIDTH, D_MODEL), MIX_WIDTH ** -0.5)
    inp['w_gate'] = nrm((N_DENSE, D_MODEL, D_FF), D_MODEL ** -0.5)
    inp['w_up'] = nrm((N_DENSE, D_MODEL, D_FF), D_MODEL ** -0.5)
    inp['w_down'] = nrm((N_DENSE, D_FF, D_MODEL), D_FF ** -0.5)
    inp['w_router'] = nrm((N_MOE, D_MODEL, N_EXPERTS), D_MODEL ** -0.5)
    inp['we_gate'] = nrm((N_MOE, N_EXPERTS, D_MODEL, D_FF_E), D_MODEL ** -0.5)
    inp['we_up'] = nrm((N_MOE, N_EXPERTS, D_MODEL, D_FF_E), D_MODEL ** -0.5)
    inp['we_down'] = nrm((N_MOE, N_EXPERTS, D_FF_E, D_MODEL), D_FF_E ** -0.5)
    return inp


def reference(x_prompt, x_sample, cache_mem_k, cache_mem_v, cache_ckv, cache_kpe, state_delta, state_conv,
              page_table, mem_prompt, g_mix, g_ffn, g_final, w_in_a, conv_w, a_log, dt_bias, g_onorm,
              w_in_b, g_qlora, w_q_b, g_kv, w_kv_a, g_ckv, w_uk, w_uv, g_mem, w_mem_kv, w_out,
              w_gate, w_up, w_down, w_router, we_gate, we_up, we_down):
    P = dict(g_mix=g_mix, g_ffn=g_ffn, g_final=g_final, w_in_a=w_in_a, conv_w=conv_w, a_log=a_log,
             dt_bias=dt_bias, g_onorm=g_onorm, w_in_b=w_in_b, g_qlora=g_qlora, w_q_b=w_q_b, g_kv=g_kv,
             w_kv_a=w_kv_a, g_ckv=g_ckv, w_uk=w_uk, w_uv=w_uv, w_out=w_out, w_gate=w_gate, w_up=w_up,
             w_down=w_down, w_router=w_router, we_gate=we_gate, we_up=we_up, we_down=we_down)
    bp, sp, _ = x_prompt.shape
    p_mem_k, p_mem_v = memory_kv(mem_prompt, g_mem, w_mem_kv)
    conv0 = jnp.zeros((N_A_LAYERS, bp, CONV_W - 1, CONV_DIM), x_prompt.dtype)
    s0 = jnp.zeros((N_A_LAYERS, bp, DN_HEADS, DN_DK, DN_DV), x_prompt.dtype)
    y_prompt, p_conv, p_delta, p_ckv, p_kpe = run_trunk(x_prompt, jnp.arange(sp), p_mem_k, p_mem_v,
                                                        conv0, s0, mla_prompt_attend, P)
    nb, npg = page_table.shape
    past = npg * PAGE_SIZE
    ckv_past = cache_ckv[page_table].reshape(nb, past, KV_LORA)
    kpe_past = cache_kpe[page_table].reshape(nb, past, QK_ROPE)

    def attend_sample(q_lat, q_pe, ckv_new, kpe_new):
        return mla_sample_attend(q_lat, q_pe, ckv_past, kpe_past, ckv_new, kpe_new)

    pos_s = past + jnp.arange(x_sample.shape[1])
    y_sample, s_conv, s_delta, s_ckv, s_kpe = run_trunk(x_sample, pos_s, cache_mem_k, cache_mem_v,
                                                        state_conv, state_delta, attend_sample, P)
    return (y_prompt, y_sample, p_delta, p_conv, p_ckv, p_kpe, p_mem_k, p_mem_v, s_delta, s_conv, s_ckv, s_kpe)
```

```python
import functools
import math

import jax
import jax.numpy as jnp
from jax import lax
from jax.experimental import pallas as pl
from jax.experimental.pallas import tpu as pltpu

F32 = jnp.float32
BF16 = jnp.bfloat16

D_MODEL = 1024
DEPTH = 4
N_A = DEPTH // 2
PAGE = 128
DN_H = 6
DN_D = 128
DN_QK = DN_H * DN_D
CONV_W = 4
CONV_DIM = 3 * DN_QK
DN_CHUNK = 256
MLA_H = 6
Q_LORA = 384
KV_LORA = 256
QK_NOPE = 128
QK_ROPE = 64
V_HEAD = 128
ROPE_THETA = 10000.0
MLA_SCALE = (QK_NOPE + QK_ROPE) ** -0.5
QK_CAT = KV_LORA + 128
N_MEM = 256
MEM_H = 4
MEM_HD = 64
MEM_DIM = MEM_H * MEM_HD
D_FF = 2816
N_EXP = 8
D_FF_E = 1408
EPS = 1e-6

LANES = 128
VMEM_LIMIT = 56 * 1024 * 1024
HI = lax.Precision.HIGHEST


def _cp(sem, vmem=VMEM_LIMIT):
    return pltpu.CompilerParams(dimension_semantics=sem, vmem_limit_bytes=vmem)


def _row_tile(m, pref=512):
    return pref if m % pref == 0 else m


def _rms(x, g):
    return x * lax.rsqrt(jnp.mean(x * x, -1, keepdims=True) + EPS) * g


def _silu(x):
    return x * jax.nn.sigmoid(x)


def _softplus(x):
    return jnp.maximum(x, 0.0) + jnp.log1p(jnp.exp(-jnp.abs(x)))


def _dot(a, b):
    return jnp.dot(a, b, preferred_element_type=F32)


def _dot_t(a, b):
    return lax.dot_general(a, b, (((1,), (1,)), ((), ())), preferred_element_type=F32)


def _softmax_rows(s):
    m = jnp.max(s, -1, keepdims=True)
    e = jnp.exp(s - m)
    return e / jnp.sum(e, -1, keepdims=True)


def _norm_proj_kernel(x_ref, g_ref, w_ref, *out_refs, splits):
    h = _rms(x_ref[...], g_ref[...]).astype(BF16)
    off = 0
    for o_ref, n in zip(out_refs, splits):
        o_ref[...] = _dot(h, w_ref[:, off:off + n]).astype(o_ref.dtype)
        off += n


def norm_proj(x, g, w, splits, dtypes, name):
    m, k = x.shape
    tm = _row_tile(m)
    n = w.shape[1]
    return pl.pallas_call(
        functools.partial(_norm_proj_kernel, splits=splits),
        grid=(m // tm,),
        in_specs=[pl.BlockSpec((tm, k), lambda i: (i, 0)),
                  pl.BlockSpec((1, k), lambda i: (0, 0)),
                  pl.BlockSpec((k, n), lambda i: (0, 0))],
        out_specs=[pl.BlockSpec((tm, s), lambda i: (i, 0)) for s in splits],
        out_shape=[jax.ShapeDtypeStruct((m, s), d) for s, d in zip(splits, dtypes)],
        compiler_params=_cp(("parallel",)),
        name=name,
    )(x, g.reshape(1, k), w)


def _mem_kv_kernel(m_ref, g_ref, w_ref, k_ref, v_ref):
    x = m_ref[...]
    mn = x * lax.rsqrt(jnp.mean(x * x, -1, keepdims=True) + EPS)
    kv = _dot((mn * g_ref[0]).astype(BF16), w_ref[0])
    k_ref[0] = kv[:, :MEM_DIM]
    v_ref[0] = kv[:, MEM_DIM:]


def mem_kv(mem, g_mem, w_mem_kv_bf):
    m = mem.shape[0]
    tm = _row_tile(m)
    nl = g_mem.shape[0]
    out = jax.ShapeDtypeStruct((nl, m, MEM_DIM), F32)
    return pl.pallas_call(
        _mem_kv_kernel,
        grid=(m // tm, nl),
        in_specs=[pl.BlockSpec((tm, D_MODEL), lambda i, l: (i, 0)),
                  pl.BlockSpec((1, 1, D_MODEL), lambda i, l: (l, 0, 0)),
                  pl.BlockSpec((1, D_MODEL, 2 * MEM_DIM), lambda i, l: (l, 0, 0))],
        out_specs=[pl.BlockSpec((1, tm, MEM_DIM), lambda i, l: (l, i, 0))] * 2,
        out_shape=[out, out],
        compiler_params=_cp(("parallel", "arbitrary")),
        name="mem_kv",
    )(mem, g_mem.reshape(nl, 1, D_MODEL), w_mem_kv_bf)


def _tri_inverse_minus_eye(lmat, row, col):
    def mm(a, b):
        return _dot(a.astype(BF16), b.astype(BF16))

    blk = (row // 16) == (col // 16)
    l1 = jnp.where(blk, lmat, 0.0)
    l2 = mm(l1, l1)
    l4 = mm(l2, l2)
    l8 = mm(l4, l4)
    q = -l1
    q = q + l2 + mm(q, l2)
    q = q + l4 + mm(q, l4)
    q = q + l8 + mm(q, l8)
    size = 16
    while size < DN_CHUNK:
        big = (row // (2 * size)) == (col // (2 * size))
        small = (row // size) == (col // size)
        c = jnp.where(big & jnp.logical_not(small), lmat, 0.0)
        y = c + mm(q, c)
        q = q - (y + mm(y, q))
        size *= 2
    return q


def _gdn_prompt_kernel(q_ref, k_ref, v_ref, z_ref, ba_ref, wq_ref, wk_ref, wv_ref,
                       gate_ref, gon_ref, o_ref, s_out_ref, halo_ref, s_scr, gt_scr, *, hps):
    hg = pl.program_id(1)
    c = pl.program_id(2)
    nc = pl.num_programs(2)
    C = DN_CHUNK
    W = hps * DN_D

    @pl.when(c == 0)
    def _():
        halo_ref[...] = jnp.zeros_like(halo_ref)
        s_scr[...] = jnp.zeros_like(s_scr)

    u = jnp.concatenate([q_ref[0], k_ref[0], v_ref[0]], axis=-1).astype(F32)
    wc = jnp.concatenate([wq_ref[...], wk_ref[...], wv_ref[...]], axis=-1)
    ext = jnp.concatenate([halo_ref[...], u], axis=0)
    y = u * wc[CONV_W - 1:CONV_W]
    for j in range(1, CONV_W):
        y = y + pltpu.roll(ext, j, axis=0)[8:] * wc[CONV_W - 1 - j:CONV_W - j]
    halo_ref[...] = u[C - 8:]
    y = _silu(y)

    ba = ba_ref[0]
    beta_all = jax.nn.sigmoid(ba)
    g_all = -jnp.exp(gate_ref[0:1]) * _softplus(ba + gate_ref[1:2])
    row = lax.broadcasted_iota(jnp.int32, (C, C), 0)
    col = lax.broadcasted_iota(jnp.int32, (C, C), 1)
    tri = row >= col
    gcum_all = jnp.dot(tri.astype(F32), g_all, precision=HI, preferred_element_type=F32)
    gt_scr[...] = gcum_all.T
    lane = lax.broadcasted_iota(jnp.int32, (C, LANES), 1)

    for i in range(hps):
        h = hg * hps + i
        beta = jnp.sum(jnp.where(lane == h, beta_all, 0.0), -1, keepdims=True)
        gc = jnp.sum(jnp.where(lane == DN_H + h, gcum_all, 0.0), -1, keepdims=True)
        gr = gt_scr[pl.ds(DN_H + h, 1), :]
        gl = gr[:, C - 1:C]
        qh = y[:, i * DN_D:(i + 1) * DN_D]
        kh = y[:, W + i * DN_D:W + (i + 1) * DN_D]
        vh = y[:, 2 * W + i * DN_D:2 * W + (i + 1) * DN_D]
        qh = qh * lax.rsqrt(jnp.sum(qh * qh, -1, keepdims=True) + EPS) * (DN_D ** -0.5)
        kh = kh * lax.rsqrt(jnp.sum(kh * kh, -1, keepdims=True) + EPS)
        decay = jnp.exp(jnp.where(tri, gc - gr, -jnp.inf))
        kb = kh * beta
        vb = vh * beta
        k_bf = kh.astype(BF16)
        lmat = jnp.where(row > col, _dot_t(kb.astype(BF16), k_bf) * decay, 0.0)
        qinv = _tri_inverse_minus_eye(lmat, row, col)
        egc = jnp.exp(gc)
        rhs = jnp.concatenate([vb, kb * egc], axis=-1)
        sol = rhs + _dot(qinv.astype(BF16), rhs.astype(BF16))
        un = sol[:, :DN_D]
        wn = sol[:, DN_D:]
        qk = _dot_t(qh.astype(BF16), k_bf) * decay
        qg = qh * egc
        kg = kh * jnp.exp(gl - gc)
        st = s_scr[i]
        st_bf = st.astype(BF16)
        v_new = un - _dot(wn.astype(BF16), st_bf)
        vn_bf = v_new.astype(BF16)
        o = _dot(qg.astype(BF16), st_bf) + _dot(qk.astype(BF16), vn_bf)
        s_scr[i] = st * jnp.exp(gl) + lax.dot_general(
            kg.astype(BF16), vn_bf, (((0,), (0,)), ((), ())), preferred_element_type=F32)
        zh = z_ref[0, :, i * DN_D:(i + 1) * DN_D].astype(F32)
        o_ref[0, :, i * DN_D:(i + 1) * DN_D] = (_rms(o, gon_ref[...]) * _silu(zh)).astype(o_ref.dtype)

    @pl.when(c == nc - 1)
    def _():
        s_out_ref[0] = s_scr[...]


def gdn_prompt(qkv, z, ba, conv_w, gate_rows, g_onorm, b, s, hps=2):
    C = DN_CHUNK
    W = hps * DN_D
    ng = DN_H // hps
    col = lambda part: (lambda bi, hg, c: (bi, c, part * ng + hg))
    wcol = lambda part: (lambda bi, hg, c: (0, part * ng + hg))
    return pl.pallas_call(
        functools.partial(_gdn_prompt_kernel, hps=hps),
        grid=(b, ng, s // C),
        in_specs=[pl.BlockSpec((1, C, W), col(0)), pl.BlockSpec((1, C, W), col(1)),
                  pl.BlockSpec((1, C, W), col(2)),
                  pl.BlockSpec((1, C, W), lambda bi, hg, c: (bi, c, hg)),
                  pl.BlockSpec((1, C, LANES), lambda bi, hg, c: (bi, c, 0)),
                  pl.BlockSpec((CONV_W, W), wcol(0)), pl.BlockSpec((CONV_W, W), wcol(1)),
                  pl.BlockSpec((CONV_W, W), wcol(2)),
                  pl.BlockSpec((2, LANES), lambda bi, hg, c: (0, 0)),
                  pl.BlockSpec((1, DN_D), lambda bi, hg, c: (0, 0))],
        out_specs=[pl.BlockSpec((1, C, W), lambda bi, hg, c: (bi, c, hg)),
                   pl.BlockSpec((1, hps, DN_D, DN_D), lambda bi, hg, c: (bi, hg, 0, 0))],
        out_shape=[jax.ShapeDtypeStruct((b, s, DN_QK), BF16),
                   jax.ShapeDtypeStruct((b, DN_H, DN_D, DN_D), F32)],
        scratch_shapes=[pltpu.VMEM((8, 3 * W), F32), pltpu.VMEM((hps, DN_D, DN_D), F32),
                        pltpu.VMEM((LANES, C), F32)],
        compiler_params=_cp(("parallel", "parallel", "arbitrary")),
        name="gdn_prompt",
    )(qkv, qkv, qkv, z, ba, conv_w, conv_w, conv_w, gate_rows, g_onorm.reshape(1, DN_D))


def _gdn_step_kernel(u_ref, z_ref, ba_ref, cs_ref, s_ref, w_ref, gate_ref, gon_ref,
                     o_ref, cs_out_ref, s_out_ref):
    u = u_ref[0].astype(F32)
    prev = cs_ref[0]
    w = w_ref[...]
    y = u * w[CONV_W - 1:CONV_W]
    for j in range(CONV_W - 1):
        y = y + prev[j:j + 1] * w[j:j + 1]
    y = _silu(y)
    cs_out_ref[0, 0:CONV_W - 2, :] = cs_ref[0, 1:CONV_W - 1, :]
    cs_out_ref[0, CONV_W - 2:CONV_W - 1, :] = u
    ba = ba_ref[0]
    beta_all = jax.nn.sigmoid(ba)
    g_all = -jnp.exp(gate_ref[0:1]) * _softplus(ba + gate_ref[1:2])
    eye = (lax.broadcasted_iota(jnp.int32, (DN_D, DN_D), 0)
           == lax.broadcasted_iota(jnp.int32, (DN_D, DN_D), 1))

    def to_col(r):
        return jnp.sum(jnp.where(eye, jnp.broadcast_to(r, (DN_D, DN_D)), 0.0), -1, keepdims=True)

    for h in range(DN_H):
        beta = beta_all[:, h:h + 1]
        eg = jnp.exp(g_all[:, DN_H + h:DN_H + h + 1])
        qh = y[:, h * DN_D:(h + 1) * DN_D]
        kh = y[:, DN_QK + h * DN_D:DN_QK + (h + 1) * DN_D]
        vh = y[:, 2 * DN_QK + h * DN_D:2 * DN_QK + (h + 1) * DN_D]
        qh = qh * lax.rsqrt(jnp.sum(qh * qh, -1, keepdims=True) + EPS) * (DN_D ** -0.5)
        kh = kh * lax.rsqrt(jnp.sum(kh * kh, -1, keepdims=True) + EPS)
        st = s_ref[0, h]
        kcol = to_col(kh)
        qcol = to_col(qh)
        ks = jnp.sum(kcol * st, 0, keepdims=True)
        qs = jnp.sum(qcol * st, 0, keepdims=True)
        v_new = beta * vh - (beta * eg) * ks
        o = eg * qs + jnp.sum(qh * kh, -1, keepdims=True) * v_new
        s_out_ref[0, h] = st * eg + kcol * v_new
        zh = z_ref[0, :, h * DN_D:(h + 1) * DN_D].astype(F32)
        o_ref[0, :, h * DN_D:(h + 1) * DN_D] = (_rms(o, gon_ref[...]) * _silu(zh)).astype(o_ref.dtype)


def gdn_step(qkv, z, ba, conv_state, s_state, conv_w, gate_rows, g_onorm):
    nb = qkv.shape[0]
    i3 = lambda bi: (bi, 0, 0)
    return pl.pallas_call(
        _gdn_step_kernel,
        grid=(nb,),
        in_specs=[pl.BlockSpec((1, 1, CONV_DIM), i3), pl.BlockSpec((1, 1, DN_QK), i3),
                  pl.BlockSpec((1, 1, LANES), i3), pl.BlockSpec((1, CONV_W - 1, CONV_DIM), i3),
                  pl.BlockSpec((1, DN_H, DN_D, DN_D), lambda bi: (bi, 0, 0, 0)),
                  pl.BlockSpec((CONV_W, CONV_DIM), lambda bi: (0, 0)),
                  pl.BlockSpec((2, LANES), lambda bi: (0, 0)),
                  pl.BlockSpec((1, DN_D), lambda bi: (0, 0))],
        out_specs=[pl.BlockSpec((1, 1, DN_QK), i3), pl.BlockSpec((1, CONV_W - 1, CONV_DIM), i3),
                   pl.BlockSpec((1, DN_H, DN_D, DN_D), lambda bi: (bi, 0, 0, 0))],
        out_shape=[jax.ShapeDtypeStruct((nb, 1, DN_QK), F32),
                   jax.ShapeDtypeStruct((nb, CONV_W - 1, CONV_DIM), F32),
                   jax.ShapeDtypeStruct((nb, DN_H, DN_D, DN_D), F32)],
        compiler_params=_cp(("parallel",)),
        name="gdn_step",
    )(qkv.reshape(nb, 1, CONV_DIM), z.reshape(nb, 1, DN_QK), ba.reshape(nb, 1, LANES),
      conv_state, s_state, conv_w, gate_rows, g_onorm.reshape(1, DN_D))


def _mem_attend_rows(mq, mk, mv):
    lane_head = lax.broadcasted_iota(jnp.int32, (1, MEM_DIM), 1) // MEM_HD
    out = jnp.zeros((mq.shape[0], MEM_DIM), F32)
    for h in range(MEM_H):
        sel = lane_head == h
        s = _dot_t(mq, jnp.where(sel, mk, 0.0).astype(BF16)) * (MEM_HD ** -0.5)
        p = _softmax_rows(s).astype(BF16)
        out = out + _dot(p, jnp.where(sel, mv, 0.0).astype(BF16))
    return out


def _mix_out_kernel(x_ref, ot_ref, mq_ref, mk_ref, mv_ref, w_ref, o_ref):
    om = _mem_attend_rows(mq_ref[0], mk_ref[0], mv_ref[0]).astype(BF16)
    nt = ot_ref.shape[-1]
    o_ref[0] = x_ref[0] + _dot(ot_ref[0], w_ref[:nt]) + _dot(om, w_ref[nt:])


def mix_out(x, o_tok, mq, mk, mv, w_out_bf, tm=512):
    b, s, _ = x.shape
    nt = o_tok.shape[-1]
    blk = lambda n: pl.BlockSpec((1, tm, n), lambda bi, i: (bi, i, 0))
    kv = pl.BlockSpec((1, N_MEM, MEM_DIM), lambda bi, i: (bi, 0, 0))
    return pl.pallas_call(
        _mix_out_kernel,
        grid=(b, s // tm),
        in_specs=[blk(D_MODEL), blk(nt), blk(MEM_DIM), kv, kv,
                  pl.BlockSpec((nt + MEM_DIM, D_MODEL), lambda bi, i: (0, 0))],
        out_specs=blk(D_MODEL),
        out_shape=jax.ShapeDtypeStruct((b, s, D_MODEL), F32),
        compiler_params=_cp(("parallel", "parallel")),
        name="mix_out",
    )(x, o_tok, mq, mk, mv, w_out_bf)


def _mem_attend_step_kernel(mq_ref, mk_ref, mv_ref, o_ref):
    lane_head = lax.broadcasted_iota(jnp.int32, (8, MEM_DIM), 1) // MEM_HD
    rowi = lax.broadcasted_iota(jnp.int32, (8, MEM_DIM), 0)
    sel = lane_head == rowi
    q8 = jnp.where(sel, jnp.broadcast_to(mq_ref[0], (8, MEM_DIM)), 0.0).astype(BF16)
    s = _dot_t(q8, mk_ref[0].astype(BF16)) * (MEM_HD ** -0.5)
    p = _softmax_rows(s).astype(BF16)
    o8 = _dot(p, mv_ref[0].astype(BF16))
    o_ref[0] = jnp.sum(jnp.where(sel, o8, 0.0), 0, keepdims=True).astype(o_ref.dtype)


def mem_attend_step(mq, mk, mv):
    nb = mq.shape[0]
    i3 = lambda bi: (bi, 0, 0)
    return pl.pallas_call(
        _mem_attend_step_kernel,
        grid=(nb,),
        in_specs=[pl.BlockSpec((1, 1, MEM_DIM), i3), pl.BlockSpec((1, N_MEM, MEM_DIM), i3),
                  pl.BlockSpec((1, N_MEM, MEM_DIM), i3)],
        out_specs=pl.BlockSpec((1, 1, MEM_DIM), i3),
        out_shape=jax.ShapeDtypeStruct((nb, 1, MEM_DIM), F32),
        compiler_params=_cp(("parallel",)),
        name="mem_attend_step",
    )(mq.reshape(nb, 1, MEM_DIM), mk, mv)


def _out_proj_kernel(x_ref, ot_ref, om_ref, w_ref, o_ref):
    nt = ot_ref.shape[-1]
    o_ref[...] = (x_ref[...] + _dot(ot_ref[...].astype(BF16), w_ref[:nt])
                  + _dot(om_ref[...].astype(BF16), w_ref[nt:]))


def out_proj(x, o_tok, o_mem, w_out_bf):
    m = x.shape[0]
    nt = o_tok.shape[-1]
    full = lambda a: pl.BlockSpec(a.shape, lambda i: (0, 0))
    return pl.pallas_call(
        _out_proj_kernel,
        grid=(1,),
        in_specs=[full(x), full(o_tok), full(o_mem), full(w_out_bf)],
        out_specs=pl.BlockSpec((m, D_MODEL), lambda i: (0, 0)),
        out_shape=jax.ShapeDtypeStruct((m, D_MODEL), F32),
        compiler_params=_cp(("arbitrary",)),
        name="out_proj",
    )(x, o_tok, o_mem, w_out_bf)


def _ffn_kernel(x_ref, g_ref, wg_ref, wu_ref, wd_ref, o_ref, h_scr, acc_scr):
    f = pl.program_id(1)

    @pl.when(f == 0)
    def _():
        x = x_ref[...]
        h_scr[...] = _rms(x, g_ref[...]).astype(BF16)
        acc_scr[...] = x

    h = h_scr[...]
    t = (_silu(_dot(h, wg_ref[...])) * _dot(h, wu_ref[...])).astype(BF16)
    acc_scr[...] += _dot(t, wd_ref[...])

    @pl.when(f == pl.num_programs(1) - 1)
    def _():
        o_ref[...] = acc_scr[...]


def ffn(x, g, wg, wu, wd, tf=1408):
    m = x.shape[0]
    tm = _row_tile(m)
    return pl.pallas_call(
        _ffn_kernel,
        grid=(m // tm, D_FF // tf),
        in_specs=[pl.BlockSpec((tm, D_MODEL), lambda i, f: (i, 0)),
                  pl.BlockSpec((1, D_MODEL), lambda i, f: (0, 0)),
                  pl.BlockSpec((D_MODEL, tf), lambda i, f: (0, f)),
                  pl.BlockSpec((D_MODEL, tf), lambda i, f: (0, f)),
                  pl.BlockSpec((tf, D_MODEL), lambda i, f: (f, 0))],
        out_specs=pl.BlockSpec((tm, D_MODEL), lambda i, f: (i, 0)),
        out_shape=jax.ShapeDtypeStruct((m, D_MODEL), F32),
        scratch_shapes=[pltpu.VMEM((tm, D_MODEL), BF16), pltpu.VMEM((tm, D_MODEL), F32)],
        compiler_params=_cp(("parallel", "arbitrary")),
        name="ffn",
    )(x, g.reshape(1, D_MODEL), wg, wu, wd)


def _top2_gates(logits):
    lane = lax.broadcasted_iota(jnp.int32, logits.shape, 1)
    probs = _softmax_rows(logits)
    p1 = jnp.max(probs, -1, keepdims=True)
    i1 = jnp.min(jnp.where(probs == p1, lane, LANES), -1, keepdims=True)
    m1 = lane == i1
    rest = jnp.where(m1, -1.0, probs)
    p2 = jnp.max(rest, -1, keepdims=True)
    i2 = jnp.min(jnp.where(rest == p2, lane, LANES), -1, keepdims=True)
    m2 = lane == i2
    tot = p1 + p2
    return jnp.where(m1, p1 / tot, 0.0) + jnp.where(m2, p2 / tot, 0.0)


def _moe_kernel(x_ref, g_ref, wr_ref, wg_ref, wu_ref, wd_ref, gf_ref, o_ref,
                h_scr, acc_scr, gate_scr, *, final_norm):
    e = pl.program_id(1)

    @pl.when(e == 0)
    def _():
        x = x_ref[...]
        h = _rms(x, g_ref[...])
        h_scr[...] = h.astype(BF16)
        acc_scr[...] = x
        logits = jnp.dot(h, wr_ref[...], precision=HI, preferred_element_type=F32)
        lane = lax.broadcasted_iota(jnp.int32, logits.shape, 1)
        gate_scr[...] = _top2_gates(jnp.where(lane < N_EXP, logits, -jnp.inf))

    h = h_scr[...]
    lane = lax.broadcasted_iota(jnp.int32, gate_scr.shape, 1)
    gate = jnp.sum(jnp.where(lane == e, gate_scr[...], 0.0), -1, keepdims=True)
    t = (_silu(_dot(h, wg_ref[0])) * _dot(h, wu_ref[0])).astype(BF16)
    acc_scr[...] += gate * _dot(t, wd_ref[0])

    @pl.when(e == pl.num_programs(1) - 1)
    def _():
        y = acc_scr[...]
        o_ref[...] = _rms(y, gf_ref[...]) if final_norm else y


def moe(x, g, w_router_pad, wg, wu, wd, g_final, final_norm):
    m = x.shape[0]
    tm = _row_tile(m)
    return pl.pallas_call(
        functools.partial(_moe_kernel, final_norm=final_norm),
        grid=(m // tm, N_EXP),
        in_specs=[pl.BlockSpec((tm, D_MODEL), lambda i, e: (i, 0)),
                  pl.BlockSpec((1, D_MODEL), lambda i, e: (0, 0)),
                  pl.BlockSpec((D_MODEL, LANES), lambda i, e: (0, 0)),
                  pl.BlockSpec((1, D_MODEL, D_FF_E), lambda i, e: (e, 0, 0)),
                  pl.BlockSpec((1, D_MODEL, D_FF_E), lambda i, e: (e, 0, 0)),
                  pl.BlockSpec((1, D_FF_E, D_MODEL), lambda i, e: (e, 0, 0)),
                  pl.BlockSpec((1, D_MODEL), lambda i, e: (0, 0))],
        out_specs=pl.BlockSpec((tm, D_MODEL), lambda i, e: (i, 0)),
        out_shape=jax.ShapeDtypeStruct((m, D_MODEL), F32),
        scratch_shapes=[pltpu.VMEM((tm, D_MODEL), BF16), pltpu.VMEM((tm, D_MODEL), F32),
                        pltpu.VMEM((tm, LANES), F32)],
        compiler_params=_cp(("parallel", "arbitrary")),
        name="moe",
    )(x, g.reshape(1, D_MODEL), w_router_pad, wg, wu, wd, g_final.reshape(1, D_MODEL))


def _latent_kv_kernel(x_ref, g_ref, w_ref, gc_ref, cos_ref, sin_ref, ckv_ref, kpe_ref, kc_ref):
    h = _rms(x_ref[0], g_ref[...]).astype(BF16)
    kv = _dot(h, w_ref[...])
    ckv = _rms(kv[:, :KV_LORA], gc_ref[...])
    pe = kv[:, KV_LORA:KV_LORA + LANES] * cos_ref[...] + kv[:, KV_LORA + LANES:] * sin_ref[...]
    ckv_ref[0] = ckv
    kpe_ref[0] = pe[:, :QK_ROPE]
    kc_ref[0] = jnp.concatenate([ckv, pe], axis=-1).astype(BF16)


def latent_kv(x, g_kv, w_kv_bf, g_ckv, cos_t, sin_t):
    b, s, _ = x.shape
    tm = _row_tile(s)
    blk = lambda n: pl.BlockSpec((1, tm, n), lambda bi, i: (bi, i, 0))
    tab = pl.BlockSpec((tm, LANES), lambda bi, i: (i, 0))
    const = lambda a: pl.BlockSpec(a.shape, lambda bi, i: (0, 0))
    g_kv = g_kv.reshape(1, D_MODEL)
    g_ckv = g_ckv.reshape(1, KV_LORA)
    return pl.pallas_call(
        _latent_kv_kernel,
        grid=(b, s // tm),
        in_specs=[blk(D_MODEL), const(g_kv), const(w_kv_bf), const(g_ckv), tab, tab],
        out_specs=[blk(KV_LORA), blk(QK_ROPE), blk(QK_CAT)],
        out_shape=[jax.ShapeDtypeStruct((b, s, KV_LORA), F32),
                   jax.ShapeDtypeStruct((b, s, QK_ROPE), F32),
                   jax.ShapeDtypeStruct((b, s, QK_CAT), BF16)],
        compiler_params=_cp(("parallel", "parallel")),
        name="latent_kv",
    )(x, g_kv, w_kv_bf, g_ckv, cos_t, sin_t)


def _mla_q_kernel(x_ref, g_ref, win_ref, gq_ref, wqb_ref, wuk_ref, cos_ref, sin_ref,
                  q_ref, mq_ref):
    h = _rms(x_ref[0], g_ref[...]).astype(BF16)
    proj = _dot(h, win_ref[...])
    mq_ref[0] = proj[:, Q_LORA:].astype(mq_ref.dtype)
    ql = _rms(proj[:, :Q_LORA], gq_ref[...]).astype(BF16)
    nh = MLA_H * QK_NOPE
    for hd in range(MLA_H):
        sl = slice(hd * LANES, (hd + 1) * LANES)
        nope = _dot(ql, wqb_ref[:, sl])
        pa = _dot(ql, wqb_ref[:, nh + hd * LANES:nh + (hd + 1) * LANES])
        pb = _dot(ql, wqb_ref[:, 2 * nh + hd * LANES:2 * nh + (hd + 1) * LANES])
        q_lat = _dot(nope.astype(BF16), wuk_ref[hd])
        q_pe = pa * cos_ref[...] + pb * sin_ref[...]
        q_ref[0, hd] = (jnp.concatenate([q_lat, q_pe], axis=-1) * MLA_SCALE).astype(q_ref.dtype)


def mla_q(x, g_mix, w_in_bf, g_ql, w_qb_bf, w_ukt_bf, cos_t, sin_t):
    b, s, _ = x.shape
    tm = _row_tile(s)
    const2 = lambda a: pl.BlockSpec(a.shape, lambda bi, i: (0, 0))
    tab = pl.BlockSpec((tm, LANES), lambda bi, i: (i, 0))
    g_mix = g_mix.reshape(1, D_MODEL)
    g_ql = g_ql.reshape(1, Q_LORA)
    return pl.pallas_call(
        _mla_q_kernel,
        grid=(b, s // tm),
        in_specs=[pl.BlockSpec((1, tm, D_MODEL), lambda bi, i: (bi, i, 0)),
                  const2(g_mix), const2(w_in_bf), const2(g_ql), const2(w_qb_bf),
                  pl.BlockSpec(w_ukt_bf.shape, lambda bi, i: (0, 0, 0)), tab, tab],
        out_specs=[pl.BlockSpec((1, MLA_H, tm, QK_CAT), lambda bi, i: (bi, 0, i, 0)),
                   pl.BlockSpec((1, tm, MEM_DIM), lambda bi, i: (bi, i, 0))],
        out_shape=[jax.ShapeDtypeStruct((b, MLA_H, s, QK_CAT), BF16),
                   jax.ShapeDtypeStruct((b, s, MEM_DIM), BF16)],
        compiler_params=_cp(("parallel", "parallel")),
        name="mla_q",
    )(x, g_mix, w_in_bf, g_ql, w_qb_bf, w_ukt_bf, cos_t, sin_t)


def _mla_prompt_kernel(q_ref, k_ref, wuv_ref, o_ref, m_scr, l_scr, acc_scr, *, tq, tk):
    qi = pl.program_id(1)
    ki = pl.program_id(2)
    rows = MLA_H * tq

    @pl.when(ki == 0)
    def _():
        m_scr[...] = jnp.full_like(m_scr, -jnp.inf)
        l_scr[...] = jnp.zeros_like(l_scr)
        acc_scr[...] = jnp.zeros_like(acc_scr)

    def step(masked):
        q = q_ref[0].reshape(rows, QK_CAT)
        k = k_ref[0]
        s = _dot_t(q, k)
        if masked:
            qpos = qi * tq + (lax.broadcasted_iota(jnp.int32, (rows, tk), 0) & (tq - 1))
            kpos = ki * tk + lax.broadcasted_iota(jnp.int32, (rows, tk), 1)
            s = jnp.where(kpos <= qpos, s, -jnp.inf)
        m_new = jnp.maximum(m_scr[...], jnp.max(s, -1, keepdims=True))
        alpha = jnp.exp(m_scr[...] - m_new)
        p = jnp.exp(s - m_new)
        l_scr[...] = alpha * l_scr[...] + jnp.sum(p, -1, keepdims=True)
        acc_scr[...] = alpha * acc_scr[...] + _dot(p.astype(BF16), k[:, :KV_LORA])
        m_scr[...] = m_new

    pl.when((ki + 1) * tk - 1 <= qi * tq)(lambda: step(False))
    pl.when(jnp.logical_and((ki + 1) * tk - 1 > qi * tq, ki * tk < (qi + 1) * tq))(lambda: step(True))

    @pl.when(ki == pl.num_programs(2) - 1)
    def _():
        o_lat = (acc_scr[...] / l_scr[...]).astype(BF16)
        for hd in range(MLA_H):
            o_ref[0, :, hd * V_HEAD:(hd + 1) * V_HEAD] = _dot(
                o_lat[hd * tq:(hd + 1) * tq], wuv_ref[hd]).astype(o_ref.dtype)


def mla_prompt_attend(q, kc, w_uv_bf, tq=256, tk=256):
    b, _, s, _ = q.shape
    rows = MLA_H * tq
    kmap = lambda bi, qi, ki: (bi, jnp.minimum(ki, ((qi + 1) * tq - 1) // tk), 0)
    return pl.pallas_call(
        functools.partial(_mla_prompt_kernel, tq=tq, tk=tk),
        grid=(b, s // tq, s // tk),
        in_specs=[pl.BlockSpec((1, MLA_H, tq, QK_CAT), lambda bi, qi, ki: (bi, 0, qi, 0)),
                  pl.BlockSpec((1, tk, QK_CAT), kmap),
                  pl.BlockSpec(w_uv_bf.shape, lambda bi, qi, ki: (0, 0, 0))],
        out_specs=pl.BlockSpec((1, tq, MLA_H * V_HEAD), lambda bi, qi, ki: (bi, qi, 0)),
        out_shape=jax.ShapeDtypeStruct((b, s, MLA_H * V_HEAD), BF16),
        scratch_shapes=[pltpu.VMEM((rows, 1), F32), pltpu.VMEM((rows, 1), F32),
                        pltpu.VMEM((rows, KV_LORA), F32)],
        compiler_params=_cp(("parallel", "parallel", "arbitrary")),
        name="mla_prompt_attend",
    )(q, kc, w_uv_bf)


def _mla_step_kernel(pt_ref, q_ref, knew_ref, wuv_ref, *rest, pps):
    ckv_refs = rest[:pps]
    kpe_refs = rest[pps:2 * pps]
    o_ref, m_scr, l_scr, acc_scr = rest[2 * pps:]
    j = pl.program_id(1)
    q = q_ref[0]
    q_lat = q[:, :KV_LORA]
    q_pe = q[:, KV_LORA:KV_LORA + QK_ROPE]

    @pl.when(j == 0)
    def _():
        s0 = _dot_t(q, knew_ref[0])[:, 0:1]
        m_scr[...] = s0
        l_scr[...] = jnp.ones_like(l_scr)
        acc_scr[...] = jnp.broadcast_to(knew_ref[0][0:1, :KV_LORA].astype(F32), acc_scr.shape)

    ks = [r[0].astype(BF16) for r in ckv_refs]
    ss = [_dot_t(q_lat, kb) + _dot_t(q_pe, r[0].astype(BF16)) for kb, r in zip(ks, kpe_refs)]
    m_new = m_scr[...]
    for s in ss:
        m_new = jnp.maximum(m_new, jnp.max(s, -1, keepdims=True))
    alpha = jnp.exp(m_scr[...] - m_new)
    l_new = alpha * l_scr[...]
    acc = alpha * acc_scr[...]
    for s, kb in zip(ss, ks):
        p = jnp.exp(s - m_new)
        l_new = l_new + jnp.sum(p, -1, keepdims=True)
        acc = acc + _dot(p.astype(BF16), kb)
    m_scr[...] = m_new
    l_scr[...] = l_new
    acc_scr[...] = acc

    @pl.when(j == pl.num_programs(1) - 1)
    def _():
        o_lat = (acc_scr[...] / l_scr[...]).astype(BF16)
        full = _dot(o_lat, wuv_ref[...])
        rowi = lax.broadcasted_iota(jnp.int32, full.shape, 0)
        grp = lax.broadcasted_iota(jnp.int32, full.shape, 1) // V_HEAD
        o_ref[0] = jnp.sum(jnp.where(rowi == grp, full, 0.0), 0, keepdims=True).astype(o_ref.dtype)


def mla_step_attend(q8, k_new, w_uv_flat_bf, cache_ckv, cache_kpe, page_table, pps=16):
    nb, npg = page_table.shape
    steps = npg // pps
    page = lambda p: (lambda bi, j, pt: (pt[bi, j * pps + p], 0, 0))
    grid_spec = pltpu.PrefetchScalarGridSpec(
        num_scalar_prefetch=1,
        grid=(nb, steps),
        in_specs=[pl.BlockSpec((1, 8, QK_CAT), lambda bi, j, pt: (bi, 0, 0)),
                  pl.BlockSpec((1, 8, QK_CAT), lambda bi, j, pt: (bi, 0, 0)),
                  pl.BlockSpec(w_uv_flat_bf.shape, lambda bi, j, pt: (0, 0))]
                 + [pl.BlockSpec((1, PAGE, KV_LORA), page(p)) for p in range(pps)]
                 + [pl.BlockSpec((1, PAGE, QK_ROPE), page(p)) for p in range(pps)],
        out_specs=pl.BlockSpec((1, 1, MLA_H * V_HEAD), lambda bi, j, pt: (bi, 0, 0)),
        scratch_shapes=[pltpu.VMEM((8, 1), F32), pltpu.VMEM((8, 1), F32),
                        pltpu.VMEM((8, KV_LORA), F32)],
    )
    return pl.pallas_call(
        functools.partial(_mla_step_kernel, pps=pps),
        grid_spec=grid_spec,
        out_shape=jax.ShapeDtypeStruct((nb, 1, MLA_H * V_HEAD), F32),
        compiler_params=_cp(("parallel", "arbitrary")),
        name="mla_step_attend",
    )(page_table, q8, k_new, w_uv_flat_bf, *([cache_ckv] * pps), *([cache_kpe] * pps))


def _swap_halves(w):
    half = w.shape[-1] // 2
    return jnp.concatenate([w[..., half:], w[..., :half]], -1)


def _pad_lanes(w, n=LANES):
    return jnp.pad(w, [(0, 0)] * (w.ndim - 1) + [(0, n - w.shape[-1])])


def _prep_weights(P):
    W = {}
    o1 = CONV_DIM
    o2 = o1 + DN_QK
    o3 = o2 + 2 * DN_H
    wa = P['w_in_a']
    W['in_a'] = jnp.concatenate(
        [wa[..., :o2], _pad_lanes(wa[..., o2:o3]), wa[..., o3:]], -1).astype(BF16)
    gate = jnp.zeros((N_A, 2, LANES), F32)
    gate = gate.at[:, 0, DN_H:2 * DN_H].set(P['a_log']).at[:, 1, DN_H:2 * DN_H].set(P['dt_bias'])
    W['gate_rows'] = gate
    W['in_b'] = P['w_in_b'].astype(BF16)
    wqb = P['w_q_b'].reshape(-1, Q_LORA, MLA_H, QK_NOPE + QK_ROPE)
    nope = wqb[..., :QK_NOPE].reshape(-1, Q_LORA, MLA_H * QK_NOPE)
    pe = wqb[..., QK_NOPE:]
    pa = _pad_lanes(pe).reshape(-1, Q_LORA, MLA_H * LANES)
    pb = _pad_lanes(_swap_halves(pe)).reshape(-1, Q_LORA, MLA_H * LANES)
    W['q_b'] = jnp.concatenate([nope, pa, pb], -1).astype(BF16)
    wkv = P['w_kv_a']
    kpe_w = wkv[:, KV_LORA:]
    W['kv_a'] = jnp.concatenate(
        [wkv[:, :KV_LORA], _pad_lanes(kpe_w), _pad_lanes(_swap_halves(kpe_w))], -1).astype(BF16)
    W['uk_t'] = jnp.transpose(P['w_uk'], (1, 2, 0)).astype(BF16)
    W['uv'] = jnp.transpose(P['w_uv'], (1, 0, 2)).astype(BF16)
    W['uv_flat'] = P['w_uv'].reshape(KV_LORA, MLA_H * V_HEAD).astype(BF16)
    W['mem_kv'] = P['w_mem_kv'].astype(BF16)
    W['out'] = P['w_out'].astype(BF16)
    W['gate'] = P['w_gate'].astype(BF16)
    W['up'] = P['w_up'].astype(BF16)
    W['down'] = P['w_down'].astype(BF16)
    W['router'] = _pad_lanes(P['w_router'])
    W['e_gate'] = P['we_gate'].astype(BF16)
    W['e_up'] = P['we_up'].astype(BF16)
    W['e_down'] = P['we_down'].astype(BF16)
    return W


def _rope_tables(pos):
    half = QK_ROPE // 2
    inv = ROPE_THETA ** (-jnp.arange(half, dtype=F32) / half)
    ang = pos.astype(F32)[:, None] * inv[None, :]
    cos = jnp.cos(ang)
    sin = jnp.sin(ang)
    zero = jnp.zeros((pos.shape[0], LANES - QK_ROPE), F32)
    return (jnp.concatenate([cos, cos, zero], -1), jnp.concatenate([-sin, sin, zero], -1))


def _channel_mixer(x2, l, P, W):
    i = l // 2
    if l % 2 == 0:
        return ffn(x2, P['g_ffn'][l], W['gate'][i], W['up'][i], W['down'][i])
    return moe(x2, P['g_ffn'][l], W['router'][i], W['e_gate'][i], W['e_up'][i], W['e_down'][i],
               P['g_final'], final_norm=(l == DEPTH - 1))


def _prompt_trunk(x, mem_k, mem_v, P, W):
    b, s, _ = x.shape
    m = b * s
    cos_t, sin_t = _rope_tables(jnp.arange(s))
    conv_states, dn_states = [], []
    kc = ckv = kpe = None
    for l in range(DEPTH):
        if l < N_A:
            qkv, z, ba, mq = norm_proj(x.reshape(m, D_MODEL), P['g_mix'][l], W['in_a'][l],
                                       (CONV_DIM, DN_QK, LANES, MEM_DIM), (BF16, BF16, F32, BF16),
                                       "in_proj_a")
            qkv = qkv.reshape(b, s, CONV_DIM)
            o_tok, s_new = gdn_prompt(qkv, z.reshape(b, s, DN_QK), ba.reshape(b, s, LANES),
                                      P['conv_w'][l], W['gate_rows'][l], P['g_onorm'][l], b, s)
            conv_states.append(qkv[:, s - (CONV_W - 1):, :].astype(F32))
            dn_states.append(s_new)
            mq = mq.reshape(b, s, MEM_DIM)
        else:
            j = l - N_A
            if l == N_A:
                ckv, kpe, kc = latent_kv(x, P['g_kv'], W['kv_a'], P['g_ckv'], cos_t, sin_t)
            q, mq = mla_q(x, P['g_mix'][l], W['in_b'][j], P['g_qlora'][j], W['q_b'][j], W['uk_t'],
                          cos_t, sin_t)
            o_tok = mla_prompt_attend(q, kc, W['uv'])
        x = mix_out(x, o_tok, mq, mem_k[l], mem_v[l], W['out'][l])
        x = _channel_mixer(x.reshape(m, D_MODEL), l, P, W).reshape(b, s, D_MODEL)
    return x, jnp.stack(conv_states), jnp.stack(dn_states), ckv, kpe


def _sample_trunk(x, mem_k, mem_v, conv_prev, dn_prev, cache_ckv, cache_kpe, page_table, P, W):
    nb = x.shape[0]
    past = page_table.shape[1] * PAGE
    cos_t, sin_t = _rope_tables(jnp.full((nb,), past, jnp.int32))
    x2 = x.reshape(nb, D_MODEL)
    conv_states, dn_states = [], []
    kc = ckv = kpe = None
    for l in range(DEPTH):
        if l < N_A:
            qkv, z, ba, mq = norm_proj(x2, P['g_mix'][l], W['in_a'][l],
                                       (CONV_DIM, DN_QK, LANES, MEM_DIM), (F32, F32, F32, F32),
                                       "in_proj_a_step")
            o_tok, conv_new, s_new = gdn_step(qkv, z, ba, conv_prev[l], dn_prev[l], P['conv_w'][l],
                                              W['gate_rows'][l], P['g_onorm'][l])
            conv_states.append(conv_new)
            dn_states.append(s_new)
        else:
            j = l - N_A
            x3 = x2.reshape(1, nb, D_MODEL)
            if l == N_A:
                ckv, kpe, kc = latent_kv(x3, P['g_kv'], W['kv_a'], P['g_ckv'], cos_t, sin_t)
                ckv = ckv.reshape(nb, 1, KV_LORA)
                kpe = kpe.reshape(nb, 1, QK_ROPE)
                kc = jnp.broadcast_to(kc.reshape(nb, 1, QK_CAT), (nb, 8, QK_CAT))
            q, mq = mla_q(x3, P['g_mix'][l], W['in_b'][j], P['g_qlora'][j], W['q_b'][j], W['uk_t'],
                          cos_t, sin_t)
            q8 = jnp.pad(jnp.transpose(q[0], (1, 0, 2)), ((0, 0), (0, 8 - MLA_H), (0, 0)))
            o_tok = mla_step_attend(q8, kc, W['uv_flat'], cache_ckv, cache_kpe, page_table)
            mq = mq.reshape(nb, MEM_DIM).astype(F32)
        o_mem = mem_attend_step(mq, mem_k[l], mem_v[l])
        x2 = out_proj(x2, o_tok.reshape(nb, -1), o_mem.reshape(nb, MEM_DIM), W['out'][l])
        x2 = _channel_mixer(x2, l, P, W)
    return (x2.reshape(nb, 1, D_MODEL), jnp.stack(conv_states), jnp.stack(dn_states), ckv, kpe)


def kernel(x_prompt, x_sample, cache_mem_k, cache_mem_v, cache_ckv, cache_kpe, state_delta, state_conv,
           page_table, mem_prompt, g_mix, g_ffn, g_final, w_in_a, conv_w, a_log, dt_bias, g_onorm,
           w_in_b, g_qlora, w_q_b, g_kv, w_kv_a, g_ckv, w_uk, w_uv, g_mem, w_mem_kv, w_out,
           w_gate, w_up, w_down, w_router, we_gate, we_up, we_down):
    P = dict(g_mix=g_mix, g_ffn=g_ffn, g_final=g_final, w_in_a=w_in_a, conv_w=conv_w, a_log=a_log,
             dt_bias=dt_bias, g_onorm=g_onorm, w_in_b=w_in_b, g_qlora=g_qlora, w_q_b=w_q_b, g_kv=g_kv,
             w_kv_a=w_kv_a, g_ckv=g_ckv, w_uk=w_uk, w_uv=w_uv, w_mem_kv=w_mem_kv, w_out=w_out,
             w_gate=w_gate, w_up=w_up, w_down=w_down, w_router=w_router, we_gate=we_gate,
             we_up=we_up, we_down=we_down)
    W = _prep_weights(P)
    bp = x_prompt.shape[0]
    nl = g_mem.shape[0]
    mk, mv = mem_kv(mem_prompt.reshape(bp * N_MEM, D_MODEL), g_mem, W['mem_kv'])
    mk = mk.reshape(nl, bp, N_MEM, MEM_DIM)
    mv = mv.reshape(nl, bp, N_MEM, MEM_DIM)
    y_p, p_conv, p_delta, p_ckv, p_kpe = _prompt_trunk(x_prompt, mk, mv, P, W)
    nb = x_sample.shape[0]
    cmk = cache_mem_k.reshape(nl, nb, N_MEM, MEM_DIM)
    cmv = cache_mem_v.reshape(nl, nb, N_MEM, MEM_DIM)
    y_s, s_conv, s_delta, s_ckv, s_kpe = _sample_trunk(
        x_sample, cmk, cmv, state_conv, state_delta, cache_ckv, cache_kpe, page_table, P, W)
    p_mem_k = mk.reshape(nl, bp, N_MEM, MEM_H, MEM_HD)
    p_mem_v = mv.reshape(nl, bp, N_MEM, MEM_H, MEM_HD)
    return (y_p, y_s, p_delta, p_conv, p_ckv, p_kpe, p_mem_k, p_mem_v, s_delta, s_conv, s_ckv, s_kpe)
```

```python
import functools
import math

import jax
import jax.numpy as jnp
from jax import lax
from jax.experimental import pallas as pl
from jax.experimental.pallas import tpu as pltpu

F32 = jnp.float32
BF16 = jnp.bfloat16

D_MODEL = 1024
DEPTH = 4
N_A = DEPTH // 2
PAGE = 128
DN_H = 6
DN_D = 128
DN_QK = DN_H * DN_D
CONV_W = 4
CONV_DIM = 3 * DN_QK
DN_CHUNK = 256
MLA_H = 6
Q_LORA = 384
KV_LORA = 256
QK_NOPE = 128
QK_ROPE = 64
V_HEAD = 128
ROPE_THETA = 10000.0
MLA_SCALE = (QK_NOPE + QK_ROPE) ** -0.5
QK_CAT = KV_LORA + 128
N_MEM = 256
MEM_H = 4
MEM_HD = 64
MEM_DIM = MEM_H * MEM_HD
D_FF = 2816
N_EXP = 8
D_FF_E = 1408
EPS = 1e-6

LANES = 128
VMEM_LIMIT = 56 * 1024 * 1024
HI = lax.Precision.HIGHEST


def _cp(sem, vmem=VMEM_LIMIT):
    return pltpu.CompilerParams(dimension_semantics=sem, vmem_limit_bytes=vmem)


def _row_tile(m, pref=512):
    return pref if m % pref == 0 else m


def _rms(x, g):
    return x * lax.rsqrt(jnp.mean(x * x, -1, keepdims=True) + EPS) * g


def _silu(x):
    return x * jax.nn.sigmoid(x)


def _softplus(x):
    return jnp.maximum(x, 0.0) + jnp.log1p(jnp.exp(-jnp.abs(x)))


def _dot(a, b):
    return jnp.dot(a, b, preferred_element_type=F32)


def _mm(a, w):
    if w.dtype == F32:
        return jnp.dot(a.astype(F32), w, precision=HI, preferred_element_type=F32)
    return jnp.dot(a.astype(BF16), w, preferred_element_type=F32)


def _dot_t(a, b):
    return lax.dot_general(a, b, (((1,), (1,)), ((), ())), preferred_element_type=F32)


def _softmax_rows(s):
    m = jnp.max(s, -1, keepdims=True)
    e = jnp.exp(s - m)
    return e / jnp.sum(e, -1, keepdims=True)


def _norm_proj_kernel(x_ref, g_ref, w_ref, *out_refs, splits):
    h = _rms(x_ref[...], g_ref[...]).astype(w_ref.dtype)
    off = 0
    for o_ref, n in zip(out_refs, splits):
        o_ref[...] = _mm(h, w_ref[:, off:off + n]).astype(o_ref.dtype)
        off += n


def norm_proj(x, g, w, splits, dtypes, name):
    m, k = x.shape
    tm = _row_tile(m)
    n = w.shape[1]
    return pl.pallas_call(
        functools.partial(_norm_proj_kernel, splits=splits),
        grid=(m // tm,),
        in_specs=[pl.BlockSpec((tm, k), lambda i: (i, 0)),
                  pl.BlockSpec((1, k), lambda i: (0, 0)),
                  pl.BlockSpec((k, n), lambda i: (0, 0))],
        out_specs=[pl.BlockSpec((tm, s), lambda i: (i, 0)) for s in splits],
        out_shape=[jax.ShapeDtypeStruct((m, s), d) for s, d in zip(splits, dtypes)],
        compiler_params=_cp(("parallel",)),
        name=name,
    )(x, g.reshape(1, k), w)


def _mem_kv_kernel(m_ref, g_ref, w_ref, k_ref, v_ref):
    x = m_ref[...]
    mn = x * lax.rsqrt(jnp.mean(x * x, -1, keepdims=True) + EPS)
    kv = _dot((mn * g_ref[0]).astype(BF16), w_ref[0])
    k_ref[0] = kv[:, :MEM_DIM]
    v_ref[0] = kv[:, MEM_DIM:]


def mem_kv(mem, g_mem, w_mem_kv_bf):
    m = mem.shape[0]
    tm = _row_tile(m)
    nl = g_mem.shape[0]
    out = jax.ShapeDtypeStruct((nl, m, MEM_DIM), F32)
    return pl.pallas_call(
        _mem_kv_kernel,
        grid=(m // tm, nl),
        in_specs=[pl.BlockSpec((tm, D_MODEL), lambda i, l: (i, 0)),
                  pl.BlockSpec((1, 1, D_MODEL), lambda i, l: (l, 0, 0)),
                  pl.BlockSpec((1, D_MODEL, 2 * MEM_DIM), lambda i, l: (l, 0, 0))],
        out_specs=[pl.BlockSpec((1, tm, MEM_DIM), lambda i, l: (l, i, 0))] * 2,
        out_shape=[out, out],
        compiler_params=_cp(("parallel", "arbitrary")),
        name="mem_kv",
    )(mem, g_mem.reshape(nl, 1, D_MODEL), w_mem_kv_bf)


def _bmm(a, b):
    return jnp.einsum('hij,hjk->hik', a, b, preferred_element_type=F32)


def _bmm_t(a, b):
    return jnp.einsum('hid,hjd->hij', a, b, preferred_element_type=F32)


def _tri_inverse_minus_eye(lmat, row, col):
    def same_block(bits):
        return jnp.right_shift(row, bits) == jnp.right_shift(col, bits)

    l1 = jnp.where(same_block(4), lmat, 0.0)
    l1b = l1.astype(BF16)
    l2 = _bmm(l1b, l1b)
    l2b = l2.astype(BF16)
    l4 = _bmm(l2b, l2b)
    l4b = l4.astype(BF16)
    l8 = _bmm(l4b, l4b)
    q = -l1
    q = q + l2 + _bmm(q.astype(BF16), l2b)
    q = q + l4 + _bmm(q.astype(BF16), l4b)
    q = q + l8 + _bmm(q.astype(BF16), l8.astype(BF16))
    bits = 4
    while (1 << bits) < DN_CHUNK:
        cross = jnp.logical_and(same_block(bits + 1), jnp.logical_not(same_block(bits)))
        c = jnp.where(cross, lmat, 0.0)
        qb = q.astype(BF16)
        y = c + _bmm(qb, c.astype(BF16))
        q = q - (y + _bmm(y.astype(BF16), qb))
        bits += 1
    return q


def _gdn_prompt_kernel(u_ref, z_ref, ba_ref, w_ref, gate_ref, gon_ref, o_ref, s_out_ref,
                       halo_ref, s_scr):
    c = pl.program_id(1)
    C = DN_CHUNK

    @pl.when(c == 0)
    def _():
        halo_ref[...] = jnp.zeros_like(halo_ref)
        s_scr[...] = jnp.zeros_like(s_scr)

    u = u_ref[0].astype(F32)
    wc = w_ref[...]
    ext = jnp.concatenate([halo_ref[...], u], axis=0)
    y = u * wc[CONV_W - 1:CONV_W]
    for j in range(1, CONV_W):
        y = y + pltpu.roll(ext, j, axis=0)[8:] * wc[CONV_W - 1 - j:CONV_W - j]
    halo_ref[...] = u[C - 8:]
    y = _silu(y)

    ba = ba_ref[0]
    beta_all = jax.nn.sigmoid(ba)
    g_all = -jnp.exp(gate_ref[0:1]) * _softplus(ba + gate_ref[1:2])
    row = lax.broadcasted_iota(jnp.int32, (C, C), 0)
    col = lax.broadcasted_iota(jnp.int32, (C, C), 1)
    tri = row >= col
    gcum_all = jnp.dot(tri.astype(F32), g_all, precision=HI, preferred_element_type=F32)
    gcum_t = gcum_all.T

    heads = range(DN_H)
    per_head = lambda a, off: jnp.stack([a[:, off + h * DN_D:off + (h + 1) * DN_D] for h in heads])
    beta = jnp.stack([beta_all[:, h:h + 1] for h in heads])
    gc = jnp.stack([gcum_all[:, DN_H + h:DN_H + h + 1] for h in heads])
    gr = jnp.stack([gcum_t[DN_H + h:DN_H + h + 1, :] for h in heads])
    gl = gr[:, :, C - 1:C]
    qh = per_head(y, 0)
    kh = per_head(y, DN_QK)
    vh = per_head(y, 2 * DN_QK)
    qh = qh * lax.rsqrt(jnp.sum(qh * qh, -1, keepdims=True) + EPS) * (DN_D ** -0.5)
    kh = kh * lax.rsqrt(jnp.sum(kh * kh, -1, keepdims=True) + EPS)
    decay = jnp.exp(jnp.where(tri, gc - gr, -jnp.inf))
    kb = kh * beta
    vb = vh * beta
    k_bf = kh.astype(BF16)
    lmat = jnp.where(row > col, _bmm_t(kb.astype(BF16), k_bf) * decay, 0.0)
    qinv = _tri_inverse_minus_eye(lmat, row, col)
    egc = jnp.exp(gc)
    rhs = jnp.concatenate([vb, kb * egc], axis=-1)
    sol = rhs + _bmm(qinv.astype(BF16), rhs.astype(BF16))
    un = sol[:, :, :DN_D]
    wn = sol[:, :, DN_D:]
    qk = _bmm_t(qh.astype(BF16), k_bf) * decay
    qg = qh * egc
    kg = (kh * jnp.exp(gl - gc)).astype(BF16)
    st = s_scr[...]
    ws = _bmm(jnp.concatenate([wn, qg], axis=1).astype(BF16), st.astype(BF16))
    v_new = un - ws[:, :C]
    vn_bf = v_new.astype(BF16)
    o = ws[:, C:] + _bmm(qk.astype(BF16), vn_bf)
    st = st * jnp.exp(gl)
    on = _rms(o, gon_ref[...])
    for h in heads:
        s_scr[h] = st[h] + lax.dot_general(
            kg[h], vn_bf[h], (((0,), (0,)), ((), ())), preferred_element_type=F32)
        zh = z_ref[0, :, h * DN_D:(h + 1) * DN_D].astype(F32)
        o_ref[0, :, h * DN_D:(h + 1) * DN_D] = (on[h] * _silu(zh)).astype(o_ref.dtype)

    @pl.when(c == pl.num_programs(1) - 1)
    def _():
        s_out_ref[0] = s_scr[...]


def gdn_prompt(qkv, z, ba, conv_w, gate_rows, g_onorm):
    b, s, _ = qkv.shape
    C = DN_CHUNK
    blk = lambda n: pl.BlockSpec((1, C, n), lambda bi, c: (bi, c, 0))
    const = lambda a: pl.BlockSpec(a.shape, lambda bi, c: (0, 0))
    g_onorm = g_onorm.reshape(1, DN_D)
    return pl.pallas_call(
        _gdn_prompt_kernel,
        grid=(b, s // C),
        in_specs=[blk(CONV_DIM), blk(DN_QK), blk(LANES), const(conv_w), const(gate_rows),
                  const(g_onorm)],
        out_specs=[blk(DN_QK),
                   pl.BlockSpec((1, DN_H, DN_D, DN_D), lambda bi, c: (bi, 0, 0, 0))],
        out_shape=[jax.ShapeDtypeStruct((b, s, DN_QK), BF16),
                   jax.ShapeDtypeStruct((b, DN_H, DN_D, DN_D), F32)],
        scratch_shapes=[pltpu.VMEM((8, CONV_DIM), F32), pltpu.VMEM((DN_H, DN_D, DN_D), F32)],
        compiler_params=_cp(("parallel", "arbitrary")),
        name="gdn_prompt",
    )(qkv, z, ba, conv_w, gate_rows, g_onorm)


def _gdn_step_kernel(u_ref, z_ref, ba_ref, cs_ref, s_ref, w_ref, gate_ref, gon_ref,
                     o_ref, cs_out_ref, s_out_ref):
    u = u_ref[0].astype(F32)
    prev = cs_ref[0]
    w = w_ref[...]
    y = u * w[CONV_W - 1:CONV_W]
    for j in range(CONV_W - 1):
        y = y + prev[j:j + 1] * w[j:j + 1]
    y = _silu(y)
    cs_out_ref[0, 0:CONV_W - 2, :] = cs_ref[0, 1:CONV_W - 1, :]
    cs_out_ref[0, CONV_W - 2:CONV_W - 1, :] = u
    ba = ba_ref[0]
    beta_all = jax.nn.sigmoid(ba)
    g_all = -jnp.exp(gate_ref[0:1]) * _softplus(ba + gate_ref[1:2])
    eye = (lax.broadcasted_iota(jnp.int32, (DN_D, DN_D), 0)
           == lax.broadcasted_iota(jnp.int32, (DN_D, DN_D), 1))

    def to_col(r):
        return jnp.sum(jnp.where(eye, jnp.broadcast_to(r, (DN_D, DN_D)), 0.0), -1, keepdims=True)

    for h in range(DN_H):
        beta = beta_all[:, h:h + 1]
        eg = jnp.exp(g_all[:, DN_H + h:DN_H + h + 1])
        qh = y[:, h * DN_D:(h + 1) * DN_D]
        kh = y[:, DN_QK + h * DN_D:DN_QK + (h + 1) * DN_D]
        vh = y[:, 2 * DN_QK + h * DN_D:2 * DN_QK + (h + 1) * DN_D]
        qh = qh * lax.rsqrt(jnp.sum(qh * qh, -1, keepdims=True) + EPS) * (DN_D ** -0.5)
        kh = kh * lax.rsqrt(jnp.sum(kh * kh, -1, keepdims=True) + EPS)
        st = s_ref[0, h]
        kcol = to_col(kh)
        qcol = to_col(qh)
        ks = jnp.sum(kcol * st, 0, keepdims=True)
        qs = jnp.sum(qcol * st, 0, keepdims=True)
        v_new = beta * vh - (beta * eg) * ks
        o = eg * qs + jnp.sum(qh * kh, -1, keepdims=True) * v_new
        s_out_ref[0, h] = st * eg + kcol * v_new
        zh = z_ref[0, :, h * DN_D:(h + 1) * DN_D].astype(F32)
        o_ref[0, :, h * DN_D:(h + 1) * DN_D] = (_rms(o, gon_ref[...]) * _silu(zh)).astype(o_ref.dtype)


def gdn_step(qkv, z, ba, conv_state, s_state, conv_w, gate_rows, g_onorm):
    nb = qkv.shape[0]
    i3 = lambda bi: (bi, 0, 0)
    return pl.pallas_call(
        _gdn_step_kernel,
        grid=(nb,),
        in_specs=[pl.BlockSpec((1, 1, CONV_DIM), i3), pl.BlockSpec((1, 1, DN_QK), i3),
                  pl.BlockSpec((1, 1, LANES), i3), pl.BlockSpec((1, CONV_W - 1, CONV_DIM), i3),
                  pl.BlockSpec((1, DN_H, DN_D, DN_D), lambda bi: (bi, 0, 0, 0)),
                  pl.BlockSpec((CONV_W, CONV_DIM), lambda bi: (0, 0)),
                  pl.BlockSpec((2, LANES), lambda bi: (0, 0)),
                  pl.BlockSpec((1, DN_D), lambda bi: (0, 0))],
        out_specs=[pl.BlockSpec((1, 1, DN_QK), i3), pl.BlockSpec((1, CONV_W - 1, CONV_DIM), i3),
                   pl.BlockSpec((1, DN_H, DN_D, DN_D), lambda bi: (bi, 0, 0, 0))],
        out_shape=[jax.ShapeDtypeStruct((nb, 1, DN_QK), F32),
                   jax.ShapeDtypeStruct((nb, CONV_W - 1, CONV_DIM), F32),
                   jax.ShapeDtypeStruct((nb, DN_H, DN_D, DN_D), F32)],
        compiler_params=_cp(("parallel",)),
        name="gdn_step",
    )(qkv.reshape(nb, 1, CONV_DIM), z.reshape(nb, 1, DN_QK), ba.reshape(nb, 1, LANES),
      conv_state, s_state, conv_w, gate_rows, g_onorm.reshape(1, DN_D))


def _mem_attend_rows(mq, mk, mv):
    lane_head = lax.broadcasted_iota(jnp.int32, (1, MEM_DIM), 1) // MEM_HD
    out = jnp.zeros((mq.shape[0], MEM_DIM), F32)
    for h in range(MEM_H):
        sel = lane_head == h
        s = _dot_t(mq, jnp.where(sel, mk, 0.0).astype(BF16)) * (MEM_HD ** -0.5)
        p = _softmax_rows(s).astype(BF16)
        out = out + _dot(p, jnp.where(sel, mv, 0.0).astype(BF16))
    return out


def _mix_out_kernel(x_ref, ot_ref, mq_ref, mk_ref, mv_ref, w_ref, o_ref):
    om = _mem_attend_rows(mq_ref[0], mk_ref[0], mv_ref[0]).astype(BF16)
    nt = ot_ref.shape[-1]
    o_ref[0] = x_ref[0] + _dot(ot_ref[0], w_ref[:nt]) + _dot(om, w_ref[nt:])


def mix_out(x, o_tok, mq, mk, mv, w_out_bf, tm=512):
    b, s, _ = x.shape
    nt = o_tok.shape[-1]
    blk = lambda n: pl.BlockSpec((1, tm, n), lambda bi, i: (bi, i, 0))
    kv = pl.BlockSpec((1, N_MEM, MEM_DIM), lambda bi, i: (bi, 0, 0))
    return pl.pallas_call(
        _mix_out_kernel,
        grid=(b, s // tm),
        in_specs=[blk(D_MODEL), blk(nt), blk(MEM_DIM), kv, kv,
                  pl.BlockSpec((nt + MEM_DIM, D_MODEL), lambda bi, i: (0, 0))],
        out_specs=blk(D_MODEL),
        out_shape=jax.ShapeDtypeStruct((b, s, D_MODEL), F32),
        compiler_params=_cp(("parallel", "parallel")),
        name="mix_out",
    )(x, o_tok, mq, mk, mv, w_out_bf)


def _mem_attend_step_kernel(mq_ref, mk_ref, mv_ref, o_ref, *, precise):
    lane_head = lax.broadcasted_iota(jnp.int32, (8, MEM_DIM), 1) // MEM_HD
    rowi = lax.broadcasted_iota(jnp.int32, (8, MEM_DIM), 0)
    sel = lane_head == rowi
    q8 = jnp.where(sel, jnp.broadcast_to(mq_ref[0], (8, MEM_DIM)), 0.0)
    if precise:
        s = lax.dot_general(q8, mk_ref[0], (((1,), (1,)), ((), ())), precision=HI,
                            preferred_element_type=F32) * (MEM_HD ** -0.5)
        o8 = _mm(_softmax_rows(s), mv_ref[0])
    else:
        s = _dot_t(q8.astype(BF16), mk_ref[0].astype(BF16)) * (MEM_HD ** -0.5)
        o8 = _dot(_softmax_rows(s).astype(BF16), mv_ref[0].astype(BF16))
    o_ref[0] = jnp.sum(jnp.where(sel, o8, 0.0), 0, keepdims=True).astype(o_ref.dtype)


def mem_attend_step(mq, mk, mv, precise):
    nb = mq.shape[0]
    i3 = lambda bi: (bi, 0, 0)
    return pl.pallas_call(
        functools.partial(_mem_attend_step_kernel, precise=precise),
        grid=(nb,),
        in_specs=[pl.BlockSpec((1, 1, MEM_DIM), i3), pl.BlockSpec((1, N_MEM, MEM_DIM), i3),
                  pl.BlockSpec((1, N_MEM, MEM_DIM), i3)],
        out_specs=pl.BlockSpec((1, 1, MEM_DIM), i3),
        out_shape=jax.ShapeDtypeStruct((nb, 1, MEM_DIM), F32),
        compiler_params=_cp(("parallel",)),
        name="mem_attend_step",
    )(mq.reshape(nb, 1, MEM_DIM), mk, mv)


def _out_proj_kernel(x_ref, ot_ref, om_ref, w_ref, o_ref):
    nt = ot_ref.shape[-1]
    o_ref[...] = x_ref[...] + _mm(ot_ref[...], w_ref[:nt]) + _mm(om_ref[...], w_ref[nt:])


def out_proj(x, o_tok, o_mem, w_out_bf):
    m = x.shape[0]
    nt = o_tok.shape[-1]
    full = lambda a: pl.BlockSpec(a.shape, lambda i: (0, 0))
    return pl.pallas_call(
        _out_proj_kernel,
        grid=(1,),
        in_specs=[full(x), full(o_tok), full(o_mem), full(w_out_bf)],
        out_specs=pl.BlockSpec((m, D_MODEL), lambda i: (0, 0)),
        out_shape=jax.ShapeDtypeStruct((m, D_MODEL), F32),
        compiler_params=_cp(("arbitrary",)),
        name="out_proj",
    )(x, o_tok, o_mem, w_out_bf)


def _ffn_kernel(x_ref, g_ref, wg_ref, wu_ref, wd_ref, o_ref, h_scr, acc_scr):
    f = pl.program_id(1)

    @pl.when(f == 0)
    def _():
        x = x_ref[...]
        h_scr[...] = _rms(x, g_ref[...]).astype(h_scr.dtype)
        acc_scr[...] = x

    h = h_scr[...]
    t = _silu(_mm(h, wg_ref[...])) * _mm(h, wu_ref[...])
    acc_scr[...] += _mm(t, wd_ref[...])

    @pl.when(f == pl.num_programs(1) - 1)
    def _():
        o_ref[...] = acc_scr[...]


def ffn(x, g, wg, wu, wd, tf=1408):
    m = x.shape[0]
    tm = _row_tile(m)
    return pl.pallas_call(
        _ffn_kernel,
        grid=(m // tm, D_FF // tf),
        in_specs=[pl.BlockSpec((tm, D_MODEL), lambda i, f: (i, 0)),
                  pl.BlockSpec((1, D_MODEL), lambda i, f: (0, 0)),
                  pl.BlockSpec((D_MODEL, tf), lambda i, f: (0, f)),
                  pl.BlockSpec((D_MODEL, tf), lambda i, f: (0, f)),
                  pl.BlockSpec((tf, D_MODEL), lambda i, f: (f, 0))],
        out_specs=pl.BlockSpec((tm, D_MODEL), lambda i, f: (i, 0)),
        out_shape=jax.ShapeDtypeStruct((m, D_MODEL), F32),
        scratch_shapes=[pltpu.VMEM((tm, D_MODEL), wg.dtype), pltpu.VMEM((tm, D_MODEL), F32)],
        compiler_params=_cp(("parallel", "arbitrary")),
        name="ffn",
    )(x, g.reshape(1, D_MODEL), wg, wu, wd)


def _top2_gates(logits):
    lane = lax.broadcasted_iota(jnp.int32, logits.shape, 1)
    probs = _softmax_rows(logits)
    p1 = jnp.max(probs, -1, keepdims=True)
    i1 = jnp.min(jnp.where(probs == p1, lane, LANES), -1, keepdims=True)
    m1 = lane == i1
    rest = jnp.where(m1, -1.0, probs)
    p2 = jnp.max(rest, -1, keepdims=True)
    i2 = jnp.min(jnp.where(rest == p2, lane, LANES), -1, keepdims=True)
    m2 = lane == i2
    tot = p1 + p2
    return jnp.where(m1, p1 / tot, 0.0) + jnp.where(m2, p2 / tot, 0.0)


def _moe_kernel(x_ref, g_ref, wr_ref, wg_ref, wu_ref, wd_ref, gf_ref, o_ref,
                h_scr, acc_scr, gate_scr, *, final_norm):
    e = pl.program_id(1)

    @pl.when(e == 0)
    def _():
        x = x_ref[...]
        h = _rms(x, g_ref[...])
        h_scr[...] = h.astype(BF16)
        acc_scr[...] = x
        logits = jnp.dot(h, wr_ref[...], precision=HI, preferred_element_type=F32)
        lane = lax.broadcasted_iota(jnp.int32, logits.shape, 1)
        gate_scr[...] = _top2_gates(jnp.where(lane < N_EXP, logits, -jnp.inf))

    h = h_scr[...]
    lane = lax.broadcasted_iota(jnp.int32, gate_scr.shape, 1)
    gate = jnp.sum(jnp.where(lane == e, gate_scr[...], 0.0), -1, keepdims=True)
    t = (_silu(_dot(h, wg_ref[0])) * _dot(h, wu_ref[0])).astype(BF16)
    acc_scr[...] += gate * _dot(t, wd_ref[0])

    @pl.when(e == pl.num_programs(1) - 1)
    def _():
        y = acc_scr[...]
        o_ref[...] = _rms(y, gf_ref[...]) if final_norm else y


def moe(x, g, w_router_pad, wg, wu, wd, g_final, final_norm):
    m = x.shape[0]
    tm = _row_tile(m)
    return pl.pallas_call(
        functools.partial(_moe_kernel, final_norm=final_norm),
        grid=(m // tm, N_EXP),
        in_specs=[pl.BlockSpec((tm, D_MODEL), lambda i, e: (i, 0)),
                  pl.BlockSpec((1, D_MODEL), lambda i, e: (0, 0)),
                  pl.BlockSpec((D_MODEL, LANES), lambda i, e: (0, 0)),
                  pl.BlockSpec((1, D_MODEL, D_FF_E), lambda i, e: (e, 0, 0)),
                  pl.BlockSpec((1, D_MODEL, D_FF_E), lambda i, e: (e, 0, 0)),
                  pl.BlockSpec((1, D_FF_E, D_MODEL), lambda i, e: (e, 0, 0)),
                  pl.BlockSpec((1, D_MODEL), lambda i, e: (0, 0))],
        out_specs=pl.BlockSpec((tm, D_MODEL), lambda i, e: (i, 0)),
        out_shape=jax.ShapeDtypeStruct((m, D_MODEL), F32),
        scratch_shapes=[pltpu.VMEM((tm, D_MODEL), BF16), pltpu.VMEM((tm, D_MODEL), F32),
                        pltpu.VMEM((tm, LANES), F32)],
        compiler_params=_cp(("parallel", "arbitrary")),
        name="moe",
    )(x, g.reshape(1, D_MODEL), w_router_pad, wg, wu, wd, g_final.reshape(1, D_MODEL))


def _latent_kv_kernel(x_ref, g_ref, w_ref, gc_ref, cos_ref, sin_ref, ckv_ref, kpe_ref, kc_ref,
                      *maybe_kt_ref):
    h = _rms(x_ref[0], g_ref[...]).astype(BF16)
    kv = _dot(h, w_ref[...])
    ckv = _rms(kv[:, :KV_LORA], gc_ref[...])
    pe = kv[:, KV_LORA:KV_LORA + LANES] * cos_ref[...] + kv[:, KV_LORA + LANES:] * sin_ref[...]
    ckv_ref[0] = ckv
    kpe_ref[0] = pe[:, :QK_ROPE]
    kc = jnp.concatenate([ckv, pe], axis=-1)
    kc_ref[0] = kc.astype(BF16)
    for kt_ref in maybe_kt_ref:
        kt_ref[0] = kc.T.astype(BF16)


def latent_kv(x, g_kv, w_kv_bf, g_ckv, cos_t, sin_t, with_transposed):
    b, s, _ = x.shape
    tm = _row_tile(s)
    blk = lambda n: pl.BlockSpec((1, tm, n), lambda bi, i: (bi, i, 0))
    tab = pl.BlockSpec((tm, LANES), lambda bi, i: (i, 0))
    const = lambda a: pl.BlockSpec(a.shape, lambda bi, i: (0, 0))
    g_kv = g_kv.reshape(1, D_MODEL)
    g_ckv = g_ckv.reshape(1, KV_LORA)
    out_specs = [blk(KV_LORA), blk(QK_ROPE), blk(QK_CAT)]
    out_shape = [jax.ShapeDtypeStruct((b, s, KV_LORA), F32),
                 jax.ShapeDtypeStruct((b, s, QK_ROPE), F32),
                 jax.ShapeDtypeStruct((b, s, QK_CAT), BF16)]
    if with_transposed:
        out_specs.append(pl.BlockSpec((1, QK_CAT, tm), lambda bi, i: (bi, 0, i)))
        out_shape.append(jax.ShapeDtypeStruct((b, QK_CAT, s), BF16))
    return pl.pallas_call(
        _latent_kv_kernel,
        grid=(b, s // tm),
        in_specs=[blk(D_MODEL), const(g_kv), const(w_kv_bf), const(g_ckv), tab, tab],
        out_specs=out_specs,
        out_shape=out_shape,
        compiler_params=_cp(("parallel", "parallel")),
        name="latent_kv",
    )(x, g_kv, w_kv_bf, g_ckv, cos_t, sin_t)


def _mla_q_kernel(x_ref, g_ref, win_ref, gq_ref, wqb_ref, wuk_ref, cos_ref, sin_ref,
                  q_ref, mq_ref):
    h = _rms(x_ref[0], g_ref[...]).astype(BF16)
    proj = _dot(h, win_ref[...])
    mq_ref[0] = proj[:, Q_LORA:].astype(mq_ref.dtype)
    ql = _rms(proj[:, :Q_LORA], gq_ref[...]).astype(BF16)
    nh = MLA_H * QK_NOPE
    for hd in range(MLA_H):
        sl = slice(hd * LANES, (hd + 1) * LANES)
        nope = _dot(ql, wqb_ref[:, sl])
        pa = _dot(ql, wqb_ref[:, nh + hd * LANES:nh + (hd + 1) * LANES])
        pb = _dot(ql, wqb_ref[:, 2 * nh + hd * LANES:2 * nh + (hd + 1) * LANES])
        q_lat = _dot(nope.astype(BF16), wuk_ref[hd])
        q_pe = pa * cos_ref[...] + pb * sin_ref[...]
        q_ref[0, hd] = (jnp.concatenate([q_lat, q_pe], axis=-1)
                        * (MLA_SCALE * math.log2(math.e))).astype(q_ref.dtype)


def mla_q(x, g_mix, w_in_bf, g_ql, w_qb_bf, w_ukt_bf, cos_t, sin_t):
    b, s, _ = x.shape
    tm = _row_tile(s)
    const2 = lambda a: pl.BlockSpec(a.shape, lambda bi, i: (0, 0))
    tab = pl.BlockSpec((tm, LANES), lambda bi, i: (i, 0))
    g_mix = g_mix.reshape(1, D_MODEL)
    g_ql = g_ql.reshape(1, Q_LORA)
    return pl.pallas_call(
        _mla_q_kernel,
        grid=(b, s // tm),
        in_specs=[pl.BlockSpec((1, tm, D_MODEL), lambda bi, i: (bi, i, 0)),
                  const2(g_mix), const2(w_in_bf), const2(g_ql), const2(w_qb_bf),
                  pl.BlockSpec(w_ukt_bf.shape, lambda bi, i: (0, 0, 0)), tab, tab],
        out_specs=[pl.BlockSpec((1, MLA_H, tm, QK_CAT), lambda bi, i: (bi, 0, i, 0)),
                   pl.BlockSpec((1, tm, MEM_DIM), lambda bi, i: (bi, i, 0))],
        out_shape=[jax.ShapeDtypeStruct((b, MLA_H, s, QK_CAT), BF16),
                   jax.ShapeDtypeStruct((b, s, MEM_DIM), BF16)],
        compiler_params=_cp(("parallel", "parallel")),
        name="mla_q",
    )(x, g_mix, w_in_bf, g_ql, w_qb_bf, w_ukt_bf, cos_t, sin_t)


def _mla_prompt_kernel(qi_ref, ki_ref, q_ref, kt_ref, v_ref, wuv_ref, o_ref, m_scr, l_scr, acc_scr,
                       *, tq, tk):
    t = pl.program_id(1)
    qi = qi_ref[t]
    ki = ki_ref[t]
    reps = tk // LANES

    def wide(a, n):
        return jnp.concatenate([a] * n, axis=-1)

    @pl.when(ki == 0)
    def _():
        m_scr[...] = jnp.full_like(m_scr, -jnp.inf)
        l_scr[...] = jnp.zeros_like(l_scr)
        acc_scr[...] = jnp.zeros_like(acc_scr)

    def step(masked):
        kt = kt_ref[0]
        v = v_ref[0]
        rows = MLA_H * tq
        s = _dot(q_ref[0].reshape(rows, QK_CAT), kt)
        if masked:
            keep = (lax.broadcasted_iota(jnp.int32, (rows, tk), 1)
                    <= (lax.broadcasted_iota(jnp.int32, (rows, tk), 0) & (tq - 1)))
            s = jnp.where(keep, s, -jnp.inf)
        m_prev = m_scr[...]
        m_new = jnp.maximum(m_prev, jnp.max(s, -1, keepdims=True))
        alpha = jnp.exp2(m_prev - m_new)
        p = jnp.exp2(s - wide(m_new, reps))
        l_scr[...] = alpha * l_scr[...] + jnp.sum(p, -1, keepdims=True)
        acc_scr[...] = wide(alpha, KV_LORA // LANES) * acc_scr[...] + _dot(p.astype(BF16), v)
        m_scr[...] = m_new

    pl.when(ki < qi)(lambda: step(False))

    @pl.when(ki == qi)
    def _():
        step(True)
        for hd in range(MLA_H):
            rs = slice(hd * tq, (hd + 1) * tq)
            o_lat = (acc_scr[rs] / wide(l_scr[rs], KV_LORA // LANES)).astype(BF16)
            o_ref[0, :, hd * V_HEAD:(hd + 1) * V_HEAD] = _dot(o_lat, wuv_ref[hd]).astype(o_ref.dtype)


def mla_prompt_attend(q, kc, kc_t, w_uv_bf, tile=256):
    b, _, s, _ = q.shape
    tq = tk = tile
    n = s // tile
    pairs = [(qi, ki) for qi in range(n) for ki in range(qi + 1)]
    qi_tab = jnp.asarray([p[0] for p in pairs], jnp.int32)
    ki_tab = jnp.asarray([p[1] for p in pairs], jnp.int32)
    grid_spec = pltpu.PrefetchScalarGridSpec(
        num_scalar_prefetch=2,
        grid=(b, len(pairs)),
        in_specs=[pl.BlockSpec((1, MLA_H, tq, QK_CAT), lambda bi, t, qt, kt: (bi, 0, qt[t], 0)),
                  pl.BlockSpec((1, QK_CAT, tk), lambda bi, t, qt, kt: (bi, 0, kt[t])),
                  pl.BlockSpec((1, tk, KV_LORA), lambda bi, t, qt, kt: (bi, kt[t], 0)),
                  pl.BlockSpec(w_uv_bf.shape, lambda bi, t, qt, kt: (0, 0, 0))],
        out_specs=pl.BlockSpec((1, tq, MLA_H * V_HEAD), lambda bi, t, qt, kt: (bi, qt[t], 0)),
        scratch_shapes=[pltpu.VMEM((MLA_H * tq, LANES), F32), pltpu.VMEM((MLA_H * tq, LANES), F32),
                        pltpu.VMEM((MLA_H * tq, KV_LORA), F32)],
    )
    return pl.pallas_call(
        functools.partial(_mla_prompt_kernel, tq=tq, tk=tk),
        grid_spec=grid_spec,
        out_shape=jax.ShapeDtypeStruct((b, s, MLA_H * V_HEAD), BF16),
        compiler_params=_cp(("parallel", "arbitrary")),
        name="mla_prompt_attend",
    )(qi_tab, ki_tab, q, kc_t, kc, w_uv_bf)


def _mla_step_kernel(pt_ref, q_ref, knew_ref, wuv_ref, *rest, pps):
    ckv_refs = rest[:pps]
    kpe_refs = rest[pps:2 * pps]
    o_ref, m_scr, l_scr, acc_scr = rest[2 * pps:]
    j = pl.program_id(1)
    q = q_ref[0]
    q_lat = q[:, :KV_LORA]
    q_pe = q[:, KV_LORA:KV_LORA + QK_ROPE]

    @pl.when(j == 0)
    def _():
        s0 = _dot_t(q, knew_ref[0])[:, 0:1]
        m_scr[...] = s0
        l_scr[...] = jnp.ones_like(l_scr)
        acc_scr[...] = jnp.broadcast_to(knew_ref[0][0:1, :KV_LORA].astype(F32), acc_scr.shape)

    ks = [r[0].astype(BF16) for r in ckv_refs]
    ss = [_dot_t(q_lat, kb) + _dot_t(q_pe, r[0].astype(BF16)) for kb, r in zip(ks, kpe_refs)]
    m_new = m_scr[...]
    for s in ss:
        m_new = jnp.maximum(m_new, jnp.max(s, -1, keepdims=True))
    alpha = jnp.exp2(m_scr[...] - m_new)
    l_new = alpha * l_scr[...]
    acc = alpha * acc_scr[...]
    for s, kb in zip(ss, ks):
        p = jnp.exp2(s - m_new)
        l_new = l_new + jnp.sum(p, -1, keepdims=True)
        acc = acc + _dot(p.astype(BF16), kb)
    m_scr[...] = m_new
    l_scr[...] = l_new
    acc_scr[...] = acc

    @pl.when(j == pl.num_programs(1) - 1)
    def _():
        o_lat = (acc_scr[...] / l_scr[...]).astype(BF16)
        full = _dot(o_lat, wuv_ref[...])
        rowi = lax.broadcasted_iota(jnp.int32, full.shape, 0)
        grp = lax.broadcasted_iota(jnp.int32, full.shape, 1) // V_HEAD
        o_ref[0] = jnp.sum(jnp.where(rowi == grp, full, 0.0), 0, keepdims=True).astype(o_ref.dtype)


def mla_step_attend(q8, k_new, w_uv_flat_bf, cache_ckv, cache_kpe, page_table, pps=16):
    nb, npg = page_table.shape
    steps = npg // pps
    page = lambda p: (lambda bi, j, pt: (pt[bi, j * pps + p], 0, 0))
    grid_spec = pltpu.PrefetchScalarGridSpec(
        num_scalar_prefetch=1,
        grid=(nb, steps),
        in_specs=[pl.BlockSpec((1, 8, QK_CAT), lambda bi, j, pt: (bi, 0, 0)),
                  pl.BlockSpec((1, 8, QK_CAT), lambda bi, j, pt: (bi, 0, 0)),
                  pl.BlockSpec(w_uv_flat_bf.shape, lambda bi, j, pt: (0, 0))]
                 + [pl.BlockSpec((1, PAGE, KV_LORA), page(p)) for p in range(pps)]
                 + [pl.BlockSpec((1, PAGE, QK_ROPE), page(p)) for p in range(pps)],
        out_specs=pl.BlockSpec((1, 1, MLA_H * V_HEAD), lambda bi, j, pt: (bi, 0, 0)),
        scratch_shapes=[pltpu.VMEM((8, 1), F32), pltpu.VMEM((8, 1), F32),
                        pltpu.VMEM((8, KV_LORA), F32)],
    )
    return pl.pallas_call(
        functools.partial(_mla_step_kernel, pps=pps),
        grid_spec=grid_spec,
        out_shape=jax.ShapeDtypeStruct((nb, 1, MLA_H * V_HEAD), F32),
        compiler_params=_cp(("parallel", "arbitrary")),
        name="mla_step_attend",
    )(page_table, q8, k_new, w_uv_flat_bf, *([cache_ckv] * pps), *([cache_kpe] * pps))


def _swap_halves(w):
    half = w.shape[-1] // 2
    return jnp.concatenate([w[..., half:], w[..., :half]], -1)


def _pad_lanes(w, n=LANES):
    return jnp.pad(w, [(0, 0)] * (w.ndim - 1) + [(0, n - w.shape[-1])])


def _prep_weights(P):
    W = {}
    o1 = CONV_DIM
    o2 = o1 + DN_QK
    o3 = o2 + 2 * DN_H
    wa = P['w_in_a']
    W['in_a_f32'] = jnp.concatenate([wa[..., :o2], _pad_lanes(wa[..., o2:o3]), wa[..., o3:]], -1)
    W['in_a'] = W['in_a_f32'].astype(BF16)
    gate = jnp.zeros((N_A, 2, LANES), F32)
    gate = gate.at[:, 0, DN_H:2 * DN_H].set(P['a_log']).at[:, 1, DN_H:2 * DN_H].set(P['dt_bias'])
    W['gate_rows'] = gate
    W['in_b'] = P['w_in_b'].astype(BF16)
    wqb = P['w_q_b'].reshape(-1, Q_LORA, MLA_H, QK_NOPE + QK_ROPE)
    nope = wqb[..., :QK_NOPE].reshape(-1, Q_LORA, MLA_H * QK_NOPE)
    pe = wqb[..., QK_NOPE:]
    pa = _pad_lanes(pe).reshape(-1, Q_LORA, MLA_H * LANES)
    pb = _pad_lanes(_swap_halves(pe)).reshape(-1, Q_LORA, MLA_H * LANES)
    W['q_b'] = jnp.concatenate([nope, pa, pb], -1).astype(BF16)
    wkv = P['w_kv_a']
    kpe_w = wkv[:, KV_LORA:]
    W['kv_a'] = jnp.concatenate(
        [wkv[:, :KV_LORA], _pad_lanes(kpe_w), _pad_lanes(_swap_halves(kpe_w))], -1).astype(BF16)
    W['uk_t'] = jnp.transpose(P['w_uk'], (1, 2, 0)).astype(BF16)
    W['uv'] = jnp.transpose(P['w_uv'], (1, 0, 2)).astype(BF16)
    W['uv_flat'] = P['w_uv'].reshape(KV_LORA, MLA_H * V_HEAD).astype(BF16)
    W['mem_kv'] = P['w_mem_kv'].astype(BF16)
    W['out'] = P['w_out'].astype(BF16)
    W['gate'] = P['w_gate'].astype(BF16)
    W['up'] = P['w_up'].astype(BF16)
    W['down'] = P['w_down'].astype(BF16)
    W['router'] = _pad_lanes(P['w_router'])
    W['e_gate'] = P['we_gate'].astype(BF16)
    W['e_up'] = P['we_up'].astype(BF16)
    W['e_down'] = P['we_down'].astype(BF16)
    return W


def _rope_tables(pos):
    half = QK_ROPE // 2
    inv = ROPE_THETA ** (-jnp.arange(half, dtype=F32) / half)
    ang = pos.astype(F32)[:, None] * inv[None, :]
    cos = jnp.cos(ang)
    sin = jnp.sin(ang)
    zero = jnp.zeros((pos.shape[0], LANES - QK_ROPE), F32)
    return (jnp.concatenate([cos, cos, zero], -1), jnp.concatenate([-sin, sin, zero], -1))


def _channel_mixer(x2, l, P, W, precise=False):
    i = l // 2
    if l % 2 == 0:
        if precise:
            return ffn(x2, P['g_ffn'][l], P['w_gate'][i], P['w_up'][i], P['w_down'][i])
        return ffn(x2, P['g_ffn'][l], W['gate'][i], W['up'][i], W['down'][i])
    return moe(x2, P['g_ffn'][l], W['router'][i], W['e_gate'][i], W['e_up'][i], W['e_down'][i],
               P['g_final'], final_norm=(l == DEPTH - 1))


def _prompt_trunk(x, mem_k, mem_v, P, W):
    b, s, _ = x.shape
    m = b * s
    cos_t, sin_t = _rope_tables(jnp.arange(s))
    conv_states, dn_states = [], []
    kc = kc_t = ckv = kpe = None
    for l in range(DEPTH):
        if l < N_A:
            qkv, z, ba, mq = norm_proj(x.reshape(m, D_MODEL), P['g_mix'][l], W['in_a'][l],
                                       (CONV_DIM, DN_QK, LANES, MEM_DIM), (BF16, BF16, F32, BF16),
                                       "in_proj_a")
            qkv = qkv.reshape(b, s, CONV_DIM)
            o_tok, s_new = gdn_prompt(qkv, z.reshape(b, s, DN_QK), ba.reshape(b, s, LANES),
                                      P['conv_w'][l], W['gate_rows'][l], P['g_onorm'][l])
            conv_states.append(qkv[:, s - (CONV_W - 1):, :].astype(F32))
            dn_states.append(s_new)
            mq = mq.reshape(b, s, MEM_DIM)
        else:
            j = l - N_A
            if l == N_A:
                ckv, kpe, kc, kc_t = latent_kv(x, P['g_kv'], W['kv_a'], P['g_ckv'], cos_t, sin_t,
                                               with_transposed=True)
            q, mq = mla_q(x, P['g_mix'][l], W['in_b'][j], P['g_qlora'][j], W['q_b'][j], W['uk_t'],
                          cos_t, sin_t)
            o_tok = mla_prompt_attend(q, kc, kc_t, W['uv'])
        x = mix_out(x, o_tok, mq, mem_k[l], mem_v[l], W['out'][l])
        x = _channel_mixer(x.reshape(m, D_MODEL), l, P, W).reshape(b, s, D_MODEL)
    return x, jnp.stack(conv_states), jnp.stack(dn_states), ckv, kpe


def _sample_trunk(x, mem_k, mem_v, conv_prev, dn_prev, cache_ckv, cache_kpe, page_table, P, W):
    nb = x.shape[0]
    past = page_table.shape[1] * PAGE
    cos_t, sin_t = _rope_tables(jnp.full((nb,), past, jnp.int32))
    x2 = x.reshape(nb, D_MODEL)
    conv_states, dn_states = [], []
    kc = ckv = kpe = None
    for l in range(DEPTH):
        if l < N_A:
            qkv, z, ba, mq = norm_proj(x2, P['g_mix'][l], W['in_a_f32'][l],
                                       (CONV_DIM, DN_QK, LANES, MEM_DIM), (F32, F32, F32, F32),
                                       "in_proj_a_step")
            o_tok, conv_new, s_new = gdn_step(qkv, z, ba, conv_prev[l], dn_prev[l], P['conv_w'][l],
                                              W['gate_rows'][l], P['g_onorm'][l])
            conv_states.append(conv_new)
            dn_states.append(s_new)
        else:
            j = l - N_A
            x3 = x2.reshape(1, nb, D_MODEL)
            if l == N_A:
                ckv, kpe, kc = latent_kv(x3, P['g_kv'], W['kv_a'], P['g_ckv'], cos_t, sin_t,
                                         with_transposed=False)
                ckv = ckv.reshape(nb, 1, KV_LORA)
                kpe = kpe.reshape(nb, 1, QK_ROPE)
                kc = jnp.broadcast_to(kc.reshape(nb, 1, QK_CAT), (nb, 8, QK_CAT))
            q, mq = mla_q(x3, P['g_mix'][l], W['in_b'][j], P['g_qlora'][j], W['q_b'][j], W['uk_t'],
                          cos_t, sin_t)
            q8 = jnp.pad(jnp.transpose(q[0], (1, 0, 2)), ((0, 0), (0, 8 - MLA_H), (0, 0)))
            o_tok = mla_step_attend(q8, kc, W['uv_flat'], cache_ckv, cache_kpe, page_table)
            mq = mq.reshape(nb, MEM_DIM).astype(F32)
        precise = l < N_A
        o_mem = mem_attend_step(mq, mem_k[l], mem_v[l], precise)
        x2 = out_proj(x2, o_tok.reshape(nb, -1), o_mem.reshape(nb, MEM_DIM),
                      P['w_out'][l] if precise else W['out'][l])
        x2 = _channel_mixer(x2, l, P, W, precise)
    return (x2.reshape(nb, 1, D_MODEL), jnp.stack(conv_states), jnp.stack(dn_states), ckv, kpe)


def kernel(x_prompt, x_sample, cache_mem_k, cache_mem_v, cache_ckv, cache_kpe, state_delta, state_conv,
           page_table, mem_prompt, g_mix, g_ffn, g_final, w_in_a, conv_w, a_log, dt_bias, g_onorm,
           w_in_b, g_qlora, w_q_b, g_kv, w_kv_a, g_ckv, w_uk, w_uv, g_mem, w_mem_kv, w_out,
           w_gate, w_up, w_down, w_router, we_gate, we_up, we_down):
    P = dict(g_mix=g_mix, g_ffn=g_ffn, g_final=g_final, w_in_a=w_in_a, conv_w=conv_w, a_log=a_log,
             dt_bias=dt_bias, g_onorm=g_onorm, w_in_b=w_in_b, g_qlora=g_qlora, w_q_b=w_q_b, g_kv=g_kv,
             w_kv_a=w_kv_a, g_ckv=g_ckv, w_uk=w_uk, w_uv=w_uv, w_mem_kv=w_mem_kv, w_out=w_out,
             w_gate=w_gate, w_up=w_up, w_down=w_down, w_router=w_router, we_gate=we_gate,
             we_up=we_up, we_down=we_down)
    W = _prep_weights(P)
    bp = x_prompt.shape[0]
    nl = g_mem.shape[0]
    mk, mv = mem_kv(mem_prompt.reshape(bp * N_MEM, D_MODEL), g_mem, W['mem_kv'])
    mk = mk.reshape(nl, bp, N_MEM, MEM_DIM)
    mv = mv.reshape(nl, bp, N_MEM, MEM_DIM)
    y_p, p_conv, p_delta, p_ckv, p_kpe = _prompt_trunk(x_prompt, mk, mv, P, W)
    nb = x_sample.shape[0]
    cmk = cache_mem_k.reshape(nl, nb, N_MEM, MEM_DIM)
    cmv = cache_mem_v.reshape(nl, nb, N_MEM, MEM_DIM)
    y_s, s_conv, s_delta, s_ckv, s_kpe = _sample_trunk(
        x_sample, cmk, cmv, state_conv, state_delta, cache_ckv, cache_kpe, page_table, P, W)
    p_mem_k = mk.reshape(nl, bp, N_MEM, MEM_H, MEM_HD)
    p_mem_v = mv.reshape(nl, bp, N_MEM, MEM_H, MEM_HD)
    return (y_p, y_s, p_delta, p_conv, p_ckv, p_kpe, p_mem_k, p_mem_v, s_delta, s_conv, s_ckv, s_kpe)
```

```python
import functools
import math

import jax
import jax.numpy as jnp
from jax import lax
from jax.experimental import pallas as pl
from jax.experimental.pallas import tpu as pltpu

F32 = jnp.float32
BF16 = jnp.bfloat16

D_MODEL = 1024
DEPTH = 4
N_A = DEPTH // 2
PAGE = 128
DN_H = 6
DN_D = 128
DN_QK = DN_H * DN_D
CONV_W = 4
CONV_DIM = 3 * DN_QK
DN_CHUNK = 256
MLA_H = 6
Q_LORA = 384
KV_LORA = 256
QK_NOPE = 128
QK_ROPE = 64
V_HEAD = 128
ROPE_THETA = 10000.0
MLA_SCALE = (QK_NOPE + QK_ROPE) ** -0.5
QK_CAT = KV_LORA + 128
N_MEM = 256
MEM_H = 4
MEM_HD = 64
MEM_DIM = MEM_H * MEM_HD
D_FF = 2816
N_EXP = 8
D_FF_E = 1408
MOE_TILE = 1024
MOE_ROWS = 128
EPS = 1e-6

LANES = 128
VMEM_LIMIT = 56 * 1024 * 1024
HI = lax.Precision.HIGHEST


def _cp(sem, vmem=VMEM_LIMIT):
    return pltpu.CompilerParams(dimension_semantics=sem, vmem_limit_bytes=vmem)


def _row_tile(m, pref=512):
    return pref if m % pref == 0 else m


def _rms(x, g):
    return x * lax.rsqrt(jnp.mean(x * x, -1, keepdims=True) + EPS) * g


def _silu(x):
    return x * jax.nn.sigmoid(x)


def _softplus(x):
    return jnp.maximum(x, 0.0) + jnp.log1p(jnp.exp(-jnp.abs(x)))


def _dot(a, b):
    return jnp.dot(a, b, preferred_element_type=F32)


def _mm(a, w):
    if w.dtype == F32:
        return jnp.dot(a.astype(F32), w, precision=HI, preferred_element_type=F32)
    return jnp.dot(a.astype(BF16), w, preferred_element_type=F32)


def _dot_t(a, b):
    return lax.dot_general(a, b, (((1,), (1,)), ((), ())), preferred_element_type=F32)


def _softmax_rows(s):
    m = jnp.max(s, -1, keepdims=True)
    e = jnp.exp(s - m)
    return e / jnp.sum(e, -1, keepdims=True)


def _norm_proj_kernel(x_ref, g_ref, w_ref, *out_refs, splits):
    h = _rms(x_ref[...], g_ref[...]).astype(w_ref.dtype)
    off = 0
    for o_ref, n in zip(out_refs, splits):
        o_ref[...] = _mm(h, w_ref[:, off:off + n]).astype(o_ref.dtype)
        off += n


def norm_proj(x, g, w, splits, dtypes, name):
    m, k = x.shape
    tm = _row_tile(m)
    n = w.shape[1]
    return pl.pallas_call(
        functools.partial(_norm_proj_kernel, splits=splits),
        grid=(m // tm,),
        in_specs=[pl.BlockSpec((tm, k), lambda i: (i, 0)),
                  pl.BlockSpec((1, k), lambda i: (0, 0)),
                  pl.BlockSpec((k, n), lambda i: (0, 0))],
        out_specs=[pl.BlockSpec((tm, s), lambda i: (i, 0)) for s in splits],
        out_shape=[jax.ShapeDtypeStruct((m, s), d) for s, d in zip(splits, dtypes)],
        compiler_params=_cp(("parallel",)),
        name=name,
    )(x, g.reshape(1, k), w)


def _mem_kv_kernel(m_ref, g_ref, w_ref, k_ref, v_ref):
    x = m_ref[...]
    mn = x * lax.rsqrt(jnp.mean(x * x, -1, keepdims=True) + EPS)
    kv = _dot((mn * g_ref[0]).astype(BF16), w_ref[0])
    k_ref[0] = kv[:, :MEM_DIM]
    v_ref[0] = kv[:, MEM_DIM:]


def mem_kv(mem, g_mem, w_mem_kv_bf):
    m = mem.shape[0]
    tm = _row_tile(m)
    nl = g_mem.shape[0]
    out = jax.ShapeDtypeStruct((nl, m, MEM_DIM), F32)
    return pl.pallas_call(
        _mem_kv_kernel,
        grid=(m // tm, nl),
        in_specs=[pl.BlockSpec((tm, D_MODEL), lambda i, l: (i, 0)),
                  pl.BlockSpec((1, 1, D_MODEL), lambda i, l: (l, 0, 0)),
                  pl.BlockSpec((1, D_MODEL, 2 * MEM_DIM), lambda i, l: (l, 0, 0))],
        out_specs=[pl.BlockSpec((1, tm, MEM_DIM), lambda i, l: (l, i, 0))] * 2,
        out_shape=[out, out],
        compiler_params=_cp(("parallel", "arbitrary")),
        name="mem_kv",
    )(mem, g_mem.reshape(nl, 1, D_MODEL), w_mem_kv_bf)


def _bmm(a, b):
    return jnp.einsum('hij,hjk->hik', a, b, preferred_element_type=F32)


def _bmm_t(a, b):
    return jnp.einsum('hid,hjd->hij', a, b, preferred_element_type=F32)


def _tri_inverse_minus_eye(lmat, row, col):
    def same_block(bits):
        return jnp.right_shift(row, bits) == jnp.right_shift(col, bits)

    l1 = jnp.where(same_block(4), lmat, 0.0)
    l1b = l1.astype(BF16)
    l2 = _bmm(l1b, l1b)
    l2b = l2.astype(BF16)
    l4 = _bmm(l2b, l2b)
    l4b = l4.astype(BF16)
    l8 = _bmm(l4b, l4b)
    q = -l1
    q = q + l2 + _bmm(q.astype(BF16), l2b)
    q = q + l4 + _bmm(q.astype(BF16), l4b)
    q = q + l8 + _bmm(q.astype(BF16), l8.astype(BF16))
    bits = 4
    while (1 << bits) < DN_CHUNK:
        cross = jnp.logical_and(same_block(bits + 1), jnp.logical_not(same_block(bits)))
        c = jnp.where(cross, lmat, 0.0)
        qb = q.astype(BF16)
        y = c + _bmm(qb, c.astype(BF16))
        q = q - (y + _bmm(y.astype(BF16), qb))
        bits += 1
    return q


def _gdn_prompt_kernel(u_ref, z_ref, ba_ref, w_ref, gate_ref, gon_ref, o_ref, s_out_ref,
                       halo_ref, s_scr):
    c = pl.program_id(1)
    C = DN_CHUNK

    @pl.when(c == 0)
    def _():
        halo_ref[...] = jnp.zeros_like(halo_ref)
        s_scr[...] = jnp.zeros_like(s_scr)

    u = u_ref[0].astype(F32)
    wc = w_ref[...]
    ext = jnp.concatenate([halo_ref[...], u], axis=0)
    y = u * wc[CONV_W - 1:CONV_W]
    for j in range(1, CONV_W):
        y = y + pltpu.roll(ext, j, axis=0)[8:] * wc[CONV_W - 1 - j:CONV_W - j]
    halo_ref[...] = u[C - 8:]
    y = _silu(y)

    ba = ba_ref[0]
    beta_all = jax.nn.sigmoid(ba)
    g_all = -jnp.exp(gate_ref[0:1]) * _softplus(ba + gate_ref[1:2])
    row = lax.broadcasted_iota(jnp.int32, (C, C), 0)
    col = lax.broadcasted_iota(jnp.int32, (C, C), 1)
    tri = row >= col
    gcum_all = jnp.dot(tri.astype(F32), g_all, precision=HI, preferred_element_type=F32)
    gcum_t = gcum_all.T

    heads = range(DN_H)
    per_head = lambda a, off: jnp.stack([a[:, off + h * DN_D:off + (h + 1) * DN_D] for h in heads])
    beta = jnp.stack([beta_all[:, h:h + 1] for h in heads])
    gc = jnp.stack([gcum_all[:, DN_H + h:DN_H + h + 1] for h in heads])
    gr = jnp.stack([gcum_t[DN_H + h:DN_H + h + 1, :] for h in heads])
    gl = gr[:, :, C - 1:C]
    qh = per_head(y, 0)
    kh = per_head(y, DN_QK)
    vh = per_head(y, 2 * DN_QK)
    qh = qh * lax.rsqrt(jnp.sum(qh * qh, -1, keepdims=True) + EPS) * (DN_D ** -0.5)
    kh = kh * lax.rsqrt(jnp.sum(kh * kh, -1, keepdims=True) + EPS)
    decay = jnp.exp(jnp.where(tri, gc - gr, -jnp.inf))
    kb = kh * beta
    vb = vh * beta
    k_bf = kh.astype(BF16)
    lmat = jnp.where(row > col, _bmm_t(kb.astype(BF16), k_bf) * decay, 0.0)
    qinv = _tri_inverse_minus_eye(lmat, row, col)
    egc = jnp.exp(gc)
    rhs = jnp.concatenate([vb, kb * egc], axis=-1)
    sol = rhs + _bmm(qinv.astype(BF16), rhs.astype(BF16))
    un = sol[:, :, :DN_D]
    wn = sol[:, :, DN_D:]
    qk = _bmm_t(qh.astype(BF16), k_bf) * decay
    qg = qh * egc
    kg = (kh * jnp.exp(gl - gc)).astype(BF16)
    st = s_scr[...]
    ws = _bmm(jnp.concatenate([wn, qg], axis=1).astype(BF16), st.astype(BF16))
    v_new = un - ws[:, :C]
    vn_bf = v_new.astype(BF16)
    o = ws[:, C:] + _bmm(qk.astype(BF16), vn_bf)
    st = st * jnp.exp(gl)
    on = _rms(o, gon_ref[...])
    for h in heads:
        s_scr[h] = st[h] + lax.dot_general(
            kg[h], vn_bf[h], (((0,), (0,)), ((), ())), preferred_element_type=F32)
        zh = z_ref[0, :, h * DN_D:(h + 1) * DN_D].astype(F32)
        o_ref[0, :, h * DN_D:(h + 1) * DN_D] = (on[h] * _silu(zh)).astype(o_ref.dtype)

    @pl.when(c == pl.num_programs(1) - 1)
    def _():
        s_out_ref[0] = s_scr[...]


def gdn_prompt(qkv, z, ba, conv_w, gate_rows, g_onorm):
    b, s, _ = qkv.shape
    C = DN_CHUNK
    blk = lambda n: pl.BlockSpec((1, C, n), lambda bi, c: (bi, c, 0))
    const = lambda a: pl.BlockSpec(a.shape, lambda bi, c: (0, 0))
    g_onorm = g_onorm.reshape(1, DN_D)
    return pl.pallas_call(
        _gdn_prompt_kernel,
        grid=(b, s // C),
        in_specs=[blk(CONV_DIM), blk(DN_QK), blk(LANES), const(conv_w), const(gate_rows),
                  const(g_onorm)],
        out_specs=[blk(DN_QK),
                   pl.BlockSpec((1, DN_H, DN_D, DN_D), lambda bi, c: (bi, 0, 0, 0))],
        out_shape=[jax.ShapeDtypeStruct((b, s, DN_QK), BF16),
                   jax.ShapeDtypeStruct((b, DN_H, DN_D, DN_D), F32)],
        scratch_shapes=[pltpu.VMEM((8, CONV_DIM), F32), pltpu.VMEM((DN_H, DN_D, DN_D), F32)],
        compiler_params=_cp(("parallel", "arbitrary")),
        name="gdn_prompt",
    )(qkv, z, ba, conv_w, gate_rows, g_onorm)


def _gdn_step_kernel(u_ref, z_ref, ba_ref, cs_ref, s_ref, w_ref, gate_ref, gon_ref,
                     o_ref, cs_out_ref, s_out_ref):
    u = u_ref[0].astype(F32)
    prev = cs_ref[0]
    w = w_ref[...]
    y = u * w[CONV_W - 1:CONV_W]
    for j in range(CONV_W - 1):
        y = y + prev[j:j + 1] * w[j:j + 1]
    y = _silu(y)
    cs_out_ref[0, 0:CONV_W - 2, :] = cs_ref[0, 1:CONV_W - 1, :]
    cs_out_ref[0, CONV_W - 2:CONV_W - 1, :] = u
    ba = ba_ref[0]
    beta_all = jax.nn.sigmoid(ba)
    g_all = -jnp.exp(gate_ref[0:1]) * _softplus(ba + gate_ref[1:2])
    eye = (lax.broadcasted_iota(jnp.int32, (DN_D, DN_D), 0)
           == lax.broadcasted_iota(jnp.int32, (DN_D, DN_D), 1))

    def to_col(r):
        return jnp.sum(jnp.where(eye, jnp.broadcast_to(r, (DN_D, DN_D)), 0.0), -1, keepdims=True)

    for h in range(DN_H):
        beta = beta_all[:, h:h + 1]
        eg = jnp.exp(g_all[:, DN_H + h:DN_H + h + 1])
        qh = y[:, h * DN_D:(h + 1) * DN_D]
        kh = y[:, DN_QK + h * DN_D:DN_QK + (h + 1) * DN_D]
        vh = y[:, 2 * DN_QK + h * DN_D:2 * DN_QK + (h + 1) * DN_D]
        qh = qh * lax.rsqrt(jnp.sum(qh * qh, -1, keepdims=True) + EPS) * (DN_D ** -0.5)
        kh = kh * lax.rsqrt(jnp.sum(kh * kh, -1, keepdims=True) + EPS)
        st = s_ref[0, h]
        kcol = to_col(kh)
        qcol = to_col(qh)
        ks = jnp.sum(kcol * st, 0, keepdims=True)
        qs = jnp.sum(qcol * st, 0, keepdims=True)
        v_new = beta * vh - (beta * eg) * ks
        o = eg * qs + jnp.sum(qh * kh, -1, keepdims=True) * v_new
        s_out_ref[0, h] = st * eg + kcol * v_new
        zh = z_ref[0, :, h * DN_D:(h + 1) * DN_D].astype(F32)
        o_ref[0, :, h * DN_D:(h + 1) * DN_D] = (_rms(o, gon_ref[...]) * _silu(zh)).astype(o_ref.dtype)


def gdn_step(qkv, z, ba, conv_state, s_state, conv_w, gate_rows, g_onorm):
    nb = qkv.shape[0]
    i3 = lambda bi: (bi, 0, 0)
    return pl.pallas_call(
        _gdn_step_kernel,
        grid=(nb,),
        in_specs=[pl.BlockSpec((1, 1, CONV_DIM), i3), pl.BlockSpec((1, 1, DN_QK), i3),
                  pl.BlockSpec((1, 1, LANES), i3), pl.BlockSpec((1, CONV_W - 1, CONV_DIM), i3),
                  pl.BlockSpec((1, DN_H, DN_D, DN_D), lambda bi: (bi, 0, 0, 0)),
                  pl.BlockSpec((CONV_W, CONV_DIM), lambda bi: (0, 0)),
                  pl.BlockSpec((2, LANES), lambda bi: (0, 0)),
                  pl.BlockSpec((1, DN_D), lambda bi: (0, 0))],
        out_specs=[pl.BlockSpec((1, 1, DN_QK), i3), pl.BlockSpec((1, CONV_W - 1, CONV_DIM), i3),
                   pl.BlockSpec((1, DN_H, DN_D, DN_D), lambda bi: (bi, 0, 0, 0))],
        out_shape=[jax.ShapeDtypeStruct((nb, 1, DN_QK), F32),
                   jax.ShapeDtypeStruct((nb, CONV_W - 1, CONV_DIM), F32),
                   jax.ShapeDtypeStruct((nb, DN_H, DN_D, DN_D), F32)],
        compiler_params=_cp(("parallel",)),
        name="gdn_step",
    )(qkv.reshape(nb, 1, CONV_DIM), z.reshape(nb, 1, DN_QK), ba.reshape(nb, 1, LANES),
      conv_state, s_state, conv_w, gate_rows, g_onorm.reshape(1, DN_D))


def _mem_attend_rows(mq, mk, mv):
    lane_head = lax.broadcasted_iota(jnp.int32, (1, MEM_DIM), 1) // MEM_HD
    out = jnp.zeros((mq.shape[0], MEM_DIM), F32)
    for h in range(MEM_H):
        sel = lane_head == h
        s = _dot_t(mq, jnp.where(sel, mk, 0.0).astype(BF16)) * (MEM_HD ** -0.5)
        p = _softmax_rows(s).astype(BF16)
        out = out + _dot(p, jnp.where(sel, mv, 0.0).astype(BF16))
    return out


def _mix_out_kernel(x_ref, ot_ref, mq_ref, mk_ref, mv_ref, w_ref, o_ref):
    om = _mem_attend_rows(mq_ref[0], mk_ref[0], mv_ref[0]).astype(BF16)
    nt = ot_ref.shape[-1]
    o_ref[0] = x_ref[0] + _dot(ot_ref[0], w_ref[:nt]) + _dot(om, w_ref[nt:])


def mix_out(x, o_tok, mq, mk, mv, w_out_bf, tm=512):
    b, s, _ = x.shape
    nt = o_tok.shape[-1]
    blk = lambda n: pl.BlockSpec((1, tm, n), lambda bi, i: (bi, i, 0))
    kv = pl.BlockSpec((1, N_MEM, MEM_DIM), lambda bi, i: (bi, 0, 0))
    return pl.pallas_call(
        _mix_out_kernel,
        grid=(b, s // tm),
        in_specs=[blk(D_MODEL), blk(nt), blk(MEM_DIM), kv, kv,
                  pl.BlockSpec((nt + MEM_DIM, D_MODEL), lambda bi, i: (0, 0))],
        out_specs=blk(D_MODEL),
        out_shape=jax.ShapeDtypeStruct((b, s, D_MODEL), F32),
        compiler_params=_cp(("parallel", "parallel")),
        name="mix_out",
    )(x, o_tok, mq, mk, mv, w_out_bf)


def _mem_attend_step_kernel(mq_ref, mk_ref, mv_ref, o_ref, *, precise):
    lane_head = lax.broadcasted_iota(jnp.int32, (8, MEM_DIM), 1) // MEM_HD
    rowi = lax.broadcasted_iota(jnp.int32, (8, MEM_DIM), 0)
    sel = lane_head == rowi
    q8 = jnp.where(sel, jnp.broadcast_to(mq_ref[0], (8, MEM_DIM)), 0.0)
    if precise:
        s = lax.dot_general(q8, mk_ref[0], (((1,), (1,)), ((), ())), precision=HI,
                            preferred_element_type=F32) * (MEM_HD ** -0.5)
        o8 = _mm(_softmax_rows(s), mv_ref[0])
    else:
        s = _dot_t(q8.astype(BF16), mk_ref[0].astype(BF16)) * (MEM_HD ** -0.5)
        o8 = _dot(_softmax_rows(s).astype(BF16), mv_ref[0].astype(BF16))
    o_ref[0] = jnp.sum(jnp.where(sel, o8, 0.0), 0, keepdims=True).astype(o_ref.dtype)


def mem_attend_step(mq, mk, mv, precise):
    nb = mq.shape[0]
    i3 = lambda bi: (bi, 0, 0)
    return pl.pallas_call(
        functools.partial(_mem_attend_step_kernel, precise=precise),
        grid=(nb,),
        in_specs=[pl.BlockSpec((1, 1, MEM_DIM), i3), pl.BlockSpec((1, N_MEM, MEM_DIM), i3),
                  pl.BlockSpec((1, N_MEM, MEM_DIM), i3)],
        out_specs=pl.BlockSpec((1, 1, MEM_DIM), i3),
        out_shape=jax.ShapeDtypeStruct((nb, 1, MEM_DIM), F32),
        compiler_params=_cp(("parallel",)),
        name="mem_attend_step",
    )(mq.reshape(nb, 1, MEM_DIM), mk, mv)


def _out_proj_kernel(x_ref, ot_ref, om_ref, w_ref, o_ref):
    nt = ot_ref.shape[-1]
    o_ref[...] = x_ref[...] + _mm(ot_ref[...], w_ref[:nt]) + _mm(om_ref[...], w_ref[nt:])


def out_proj(x, o_tok, o_mem, w_out_bf):
    m = x.shape[0]
    nt = o_tok.shape[-1]
    full = lambda a: pl.BlockSpec(a.shape, lambda i: (0, 0))
    return pl.pallas_call(
        _out_proj_kernel,
        grid=(1,),
        in_specs=[full(x), full(o_tok), full(o_mem), full(w_out_bf)],
        out_specs=pl.BlockSpec((m, D_MODEL), lambda i: (0, 0)),
        out_shape=jax.ShapeDtypeStruct((m, D_MODEL), F32),
        compiler_params=_cp(("arbitrary",)),
        name="out_proj",
    )(x, o_tok, o_mem, w_out_bf)


def _ffn_kernel(x_ref, g_ref, wg_ref, wu_ref, wd_ref, o_ref, h_scr, acc_scr):
    f = pl.program_id(1)

    @pl.when(f == 0)
    def _():
        x = x_ref[...]
        h_scr[...] = _rms(x, g_ref[...]).astype(h_scr.dtype)
        acc_scr[...] = x

    h = h_scr[...]
    t = _silu(_mm(h, wg_ref[...])) * _mm(h, wu_ref[...])
    acc_scr[...] += _mm(t, wd_ref[...])

    @pl.when(f == pl.num_programs(1) - 1)
    def _():
        o_ref[...] = acc_scr[...]


def ffn(x, g, wg, wu, wd, tf=1408):
    m = x.shape[0]
    tm = _row_tile(m)
    return pl.pallas_call(
        _ffn_kernel,
        grid=(m // tm, D_FF // tf),
        in_specs=[pl.BlockSpec((tm, D_MODEL), lambda i, f: (i, 0)),
                  pl.BlockSpec((1, D_MODEL), lambda i, f: (0, 0)),
                  pl.BlockSpec((D_MODEL, tf), lambda i, f: (0, f)),
                  pl.BlockSpec((D_MODEL, tf), lambda i, f: (0, f)),
                  pl.BlockSpec((tf, D_MODEL), lambda i, f: (f, 0))],
        out_specs=pl.BlockSpec((tm, D_MODEL), lambda i, f: (i, 0)),
        out_shape=jax.ShapeDtypeStruct((m, D_MODEL), F32),
        scratch_shapes=[pltpu.VMEM((tm, D_MODEL), wg.dtype), pltpu.VMEM((tm, D_MODEL), F32)],
        compiler_params=_cp(("parallel", "arbitrary")),
        name="ffn",
    )(x, g.reshape(1, D_MODEL), wg, wu, wd)


def _top2_gates(logits):
    lane = lax.broadcasted_iota(jnp.int32, logits.shape, 1)
    probs = _softmax_rows(logits)
    p1 = jnp.max(probs, -1, keepdims=True)
    i1 = jnp.min(jnp.where(probs == p1, lane, LANES), -1, keepdims=True)
    m1 = lane == i1
    rest = jnp.where(m1, -1.0, probs)
    p2 = jnp.max(rest, -1, keepdims=True)
    i2 = jnp.min(jnp.where(rest == p2, lane, LANES), -1, keepdims=True)
    m2 = lane == i2
    tot = p1 + p2
    return jnp.where(m1, p1 / tot, 0.0) + jnp.where(m2, p2 / tot, 0.0)


def _moe_kernel(x_ref, g_ref, wr_ref, wg_ref, wu_ref, wd_ref, gf_ref, o_ref,
                h_scr, acc_scr, gate_scr, *, final_norm):
    e = pl.program_id(1)

    @pl.when(e == 0)
    def _():
        x = x_ref[...]
        h = _rms(x, g_ref[...])
        h_scr[...] = h.astype(BF16)
        acc_scr[...] = x
        logits = jnp.dot(h, wr_ref[...], precision=HI, preferred_element_type=F32)
        lane = lax.broadcasted_iota(jnp.int32, logits.shape, 1)
        gate_scr[...] = _top2_gates(jnp.where(lane < N_EXP, logits, -jnp.inf))

    h = h_scr[...]
    lane = lax.broadcasted_iota(jnp.int32, gate_scr.shape, 1)
    gate = jnp.sum(jnp.where(lane == e, gate_scr[...], 0.0), -1, keepdims=True)
    t = (_silu(_dot(h, wg_ref[0])) * _dot(h, wu_ref[0])).astype(BF16)
    acc_scr[...] += gate * _dot(t, wd_ref[0])

    @pl.when(e == pl.num_programs(1) - 1)
    def _():
        y = acc_scr[...]
        o_ref[...] = _rms(y, gf_ref[...]) if final_norm else y


def moe(x, g, w_router_pad, wg, wu, wd, g_final, final_norm):
    m = x.shape[0]
    tm = _row_tile(m)
    return pl.pallas_call(
        functools.partial(_moe_kernel, final_norm=final_norm),
        grid=(m // tm, N_EXP),
        in_specs=[pl.BlockSpec((tm, D_MODEL), lambda i, e: (i, 0)),
                  pl.BlockSpec((1, D_MODEL), lambda i, e: (0, 0)),
                  pl.BlockSpec((D_MODEL, LANES), lambda i, e: (0, 0)),
                  pl.BlockSpec((1, D_MODEL, D_FF_E), lambda i, e: (e, 0, 0)),
                  pl.BlockSpec((1, D_MODEL, D_FF_E), lambda i, e: (e, 0, 0)),
                  pl.BlockSpec((1, D_FF_E, D_MODEL), lambda i, e: (e, 0, 0)),
                  pl.BlockSpec((1, D_MODEL), lambda i, e: (0, 0))],
        out_specs=pl.BlockSpec((tm, D_MODEL), lambda i, e: (i, 0)),
        out_shape=jax.ShapeDtypeStruct((m, D_MODEL), F32),
        scratch_shapes=[pltpu.VMEM((tm, D_MODEL), BF16), pltpu.VMEM((tm, D_MODEL), F32),
                        pltpu.VMEM((tm, LANES), F32)],
        compiler_params=_cp(("parallel", "arbitrary")),
        name="moe",
    )(x, g.reshape(1, D_MODEL), w_router_pad, wg, wu, wd, g_final.reshape(1, D_MODEL))


def _moe_routed_kernel(x_ref, g_ref, wr_ref, wg_ref, wu_ref, wd_ref, gf_ref, o_ref,
                       h_scr, acc_scr, rank_scr, gate_t_scr, rank_t_scr, cnt_scr,
                       *, final_norm, rb):
    e = pl.program_id(1)
    tm = x_ref.shape[0]
    sub = 256

    @pl.when(e == 0)
    def _():
        x = x_ref[...]
        h = _rms(x, g_ref[...])
        h_scr[...] = h.astype(BF16)
        acc_scr[...] = x
        logits = jnp.dot(h, wr_ref[...], precision=HI, preferred_element_type=F32)
        lane = lax.broadcasted_iota(jnp.int32, logits.shape, 1)
        gates = _top2_gates(jnp.where(lane < N_EXP, logits, -jnp.inf))
        routed = gates > 0.0
        hit = jnp.where(routed, 1.0, 0.0).astype(BF16)
        col = lax.broadcasted_iota(jnp.int32, (sub, tm), 1)
        for r0 in range(0, tm, sub):
            row = r0 + lax.broadcasted_iota(jnp.int32, (sub, tm), 0)
            before = jnp.where(col < row, 1.0, 0.0).astype(BF16)
            rank = _dot(before, hit)
            rank_scr[r0:r0 + sub] = jnp.where(routed[r0:r0 + sub], rank, -1.0)
        gate_t_scr[...] = gates.T
        rank_t_scr[...] = rank_scr[...].T
        cnt_scr[...] = jnp.broadcast_to(jnp.sum(hit.astype(F32), 0, keepdims=True), cnt_scr.shape)

    lane = lax.broadcasted_iota(jnp.int32, (tm, LANES), 1)
    rank_col = jnp.sum(jnp.where(lane == e, rank_scr[...], 0.0), -1, keepdims=True)
    rank_row = rank_t_scr[pl.ds(e, 1), :]
    gate_row = gate_t_scr[pl.ds(e, 1), :]
    lane1 = lax.broadcasted_iota(jnp.int32, (1, LANES), 1)
    count = jnp.sum(jnp.where(lane1 == e, cnt_scr[0:1], 0.0)).astype(jnp.int32)

    def block(j, carry):
        base = (j * rb).astype(F32)
        pick = rank_row == base + lax.broadcasted_iota(jnp.int32, (rb, tm), 0).astype(F32)
        hg = _dot(jnp.where(pick, 1.0, 0.0).astype(BF16), h_scr[...]).astype(BF16)
        gate = jnp.sum(jnp.where(pick, gate_row, 0.0), -1, keepdims=True)
        t = (_silu(_dot(hg, wg_ref[0])) * _dot(hg, wu_ref[0])).astype(BF16)
        y = (gate * _dot(t, wd_ref[0])).astype(BF16)
        put = rank_col == base + lax.broadcasted_iota(jnp.int32, (tm, rb), 1).astype(F32)
        acc_scr[...] += _dot(jnp.where(put, 1.0, 0.0).astype(BF16), y)
        return carry

    lax.fori_loop(0, (count + rb - 1) // rb, block, 0)

    @pl.when(e == pl.num_programs(1) - 1)
    def _():
        y = acc_scr[...]
        o_ref[...] = _rms(y, gf_ref[...]) if final_norm else y


def moe_routed(x, g, w_router_pad, wg, wu, wd, g_final, final_norm, tm=MOE_TILE, rb=MOE_ROWS):
    m = x.shape[0]
    return pl.pallas_call(
        functools.partial(_moe_routed_kernel, final_norm=final_norm, rb=rb),
        grid=(m // tm, N_EXP),
        in_specs=[pl.BlockSpec((tm, D_MODEL), lambda i, e: (i, 0)),
                  pl.BlockSpec((1, D_MODEL), lambda i, e: (0, 0)),
                  pl.BlockSpec((D_MODEL, LANES), lambda i, e: (0, 0)),
                  pl.BlockSpec((1, D_MODEL, D_FF_E), lambda i, e: (e, 0, 0)),
                  pl.BlockSpec((1, D_MODEL, D_FF_E), lambda i, e: (e, 0, 0)),
                  pl.BlockSpec((1, D_FF_E, D_MODEL), lambda i, e: (e, 0, 0)),
                  pl.BlockSpec((1, D_MODEL), lambda i, e: (0, 0))],
        out_specs=pl.BlockSpec((tm, D_MODEL), lambda i, e: (i, 0)),
        out_shape=jax.ShapeDtypeStruct((m, D_MODEL), F32),
        scratch_shapes=[pltpu.VMEM((tm, D_MODEL), BF16), pltpu.VMEM((tm, D_MODEL), F32),
                        pltpu.VMEM((tm, LANES), F32),
                        pltpu.VMEM((LANES, tm), F32), pltpu.VMEM((LANES, tm), F32),
                        pltpu.VMEM((8, LANES), F32)],
        compiler_params=_cp(("parallel", "arbitrary")),
        name="moe_routed",
    )(x, g.reshape(1, D_MODEL), w_router_pad, wg, wu, wd, g_final.reshape(1, D_MODEL))


def _latent_kv_kernel(x_ref, g_ref, w_ref, gc_ref, cos_ref, sin_ref, ckv_ref, kpe_ref, kc_ref,
                      *maybe_kt_ref):
    h = _rms(x_ref[0], g_ref[...]).astype(BF16)
    kv = _dot(h, w_ref[...])
    ckv = _rms(kv[:, :KV_LORA], gc_ref[...])
    pe = kv[:, KV_LORA:KV_LORA + LANES] * cos_ref[...] + kv[:, KV_LORA + LANES:] * sin_ref[...]
    ckv_ref[0] = ckv
    kpe_ref[0] = pe[:, :QK_ROPE]
    kc = jnp.concatenate([ckv, pe], axis=-1)
    kc_ref[0] = kc.astype(BF16)
    for kt_ref in maybe_kt_ref:
        kt_ref[0] = kc.T.astype(BF16)


def latent_kv(x, g_kv, w_kv_bf, g_ckv, cos_t, sin_t, with_transposed):
    b, s, _ = x.shape
    tm = _row_tile(s)
    blk = lambda n: pl.BlockSpec((1, tm, n), lambda bi, i: (bi, i, 0))
    tab = pl.BlockSpec((tm, LANES), lambda bi, i: (i, 0))
    const = lambda a: pl.BlockSpec(a.shape, lambda bi, i: (0, 0))
    g_kv = g_kv.reshape(1, D_MODEL)
    g_ckv = g_ckv.reshape(1, KV_LORA)
    out_specs = [blk(KV_LORA), blk(QK_ROPE), blk(QK_CAT)]
    out_shape = [jax.ShapeDtypeStruct((b, s, KV_LORA), F32),
                 jax.ShapeDtypeStruct((b, s, QK_ROPE), F32),
                 jax.ShapeDtypeStruct((b, s, QK_CAT), BF16)]
    if with_transposed:
        out_specs.append(pl.BlockSpec((1, QK_CAT, tm), lambda bi, i: (bi, 0, i)))
        out_shape.append(jax.ShapeDtypeStruct((b, QK_CAT, s), BF16))
    return pl.pallas_call(
        _latent_kv_kernel,
        grid=(b, s // tm),
        in_specs=[blk(D_MODEL), const(g_kv), const(w_kv_bf), const(g_ckv), tab, tab],
        out_specs=out_specs,
        out_shape=out_shape,
        compiler_params=_cp(("parallel", "parallel")),
        name="latent_kv",
    )(x, g_kv, w_kv_bf, g_ckv, cos_t, sin_t)


def _mla_q_kernel(x_ref, g_ref, win_ref, gq_ref, wqb_ref, wuk_ref, cos_ref, sin_ref,
                  q_ref, mq_ref):
    h = _rms(x_ref[0], g_ref[...]).astype(BF16)
    proj = _dot(h, win_ref[...])
    mq_ref[0] = proj[:, Q_LORA:].astype(mq_ref.dtype)
    ql = _rms(proj[:, :Q_LORA], gq_ref[...]).astype(BF16)
    nh = MLA_H * QK_NOPE
    for hd in range(MLA_H):
        sl = slice(hd * LANES, (hd + 1) * LANES)
        nope = _dot(ql, wqb_ref[:, sl])
        pa = _dot(ql, wqb_ref[:, nh + hd * LANES:nh + (hd + 1) * LANES])
        pb = _dot(ql, wqb_ref[:, 2 * nh + hd * LANES:2 * nh + (hd + 1) * LANES])
        q_lat = _dot(nope.astype(BF16), wuk_ref[hd])
        q_pe = pa * cos_ref[...] + pb * sin_ref[...]
        q_ref[0, hd] = (jnp.concatenate([q_lat, q_pe], axis=-1)
                        * (MLA_SCALE * math.log2(math.e))).astype(q_ref.dtype)


def mla_q(x, g_mix, w_in_bf, g_ql, w_qb_bf, w_ukt_bf, cos_t, sin_t):
    b, s, _ = x.shape
    tm = _row_tile(s)
    const2 = lambda a: pl.BlockSpec(a.shape, lambda bi, i: (0, 0))
    tab = pl.BlockSpec((tm, LANES), lambda bi, i: (i, 0))
    g_mix = g_mix.reshape(1, D_MODEL)
    g_ql = g_ql.reshape(1, Q_LORA)
    return pl.pallas_call(
        _mla_q_kernel,
        grid=(b, s // tm),
        in_specs=[pl.BlockSpec((1, tm, D_MODEL), lambda bi, i: (bi, i, 0)),
                  const2(g_mix), const2(w_in_bf), const2(g_ql), const2(w_qb_bf),
                  pl.BlockSpec(w_ukt_bf.shape, lambda bi, i: (0, 0, 0)), tab, tab],
        out_specs=[pl.BlockSpec((1, MLA_H, tm, QK_CAT), lambda bi, i: (bi, 0, i, 0)),
                   pl.BlockSpec((1, tm, MEM_DIM), lambda bi, i: (bi, i, 0))],
        out_shape=[jax.ShapeDtypeStruct((b, MLA_H, s, QK_CAT), BF16),
                   jax.ShapeDtypeStruct((b, s, MEM_DIM), BF16)],
        compiler_params=_cp(("parallel", "parallel")),
        name="mla_q",
    )(x, g_mix, w_in_bf, g_ql, w_qb_bf, w_ukt_bf, cos_t, sin_t)


def _mla_prompt_kernel(qi_ref, ki_ref, q_ref, kt_ref, v_ref, wuv_ref, o_ref, m_scr, l_scr, acc_scr,
                       *, tq, tk):
    t = pl.program_id(1)
    qi = qi_ref[t]
    ki = ki_ref[t]
    reps = tk // LANES

    def wide(a, n):
        return jnp.concatenate([a] * n, axis=-1)

    @pl.when(ki == 0)
    def _():
        m_scr[...] = jnp.full_like(m_scr, -jnp.inf)
        l_scr[...] = jnp.zeros_like(l_scr)
        acc_scr[...] = jnp.zeros_like(acc_scr)

    def step(masked):
        kt = kt_ref[0]
        v = v_ref[0]
        rows = MLA_H * tq
        s = _dot(q_ref[0].reshape(rows, QK_CAT), kt)
        if masked:
            keep = (lax.broadcasted_iota(jnp.int32, (rows, tk), 1)
                    <= (lax.broadcasted_iota(jnp.int32, (rows, tk), 0) & (tq - 1)))
            s = jnp.where(keep, s, -jnp.inf)
        m_prev = m_scr[...]
        m_new = jnp.maximum(m_prev, jnp.max(s, -1, keepdims=True))
        alpha = jnp.exp2(m_prev - m_new)
        p = jnp.exp2(s - wide(m_new, reps))
        l_scr[...] = alpha * l_scr[...] + jnp.sum(p, -1, keepdims=True)
        acc_scr[...] = wide(alpha, KV_LORA // LANES) * acc_scr[...] + _dot(p.astype(BF16), v)
        m_scr[...] = m_new

    pl.when(ki < qi)(lambda: step(False))

    @pl.when(ki == qi)
    def _():
        step(True)
        for hd in range(MLA_H):
            rs = slice(hd * tq, (hd + 1) * tq)
            o_lat = (acc_scr[rs] / wide(l_scr[rs], KV_LORA // LANES)).astype(BF16)
            o_ref[0, :, hd * V_HEAD:(hd + 1) * V_HEAD] = _dot(o_lat, wuv_ref[hd]).astype(o_ref.dtype)


def mla_prompt_attend(q, kc, kc_t, w_uv_bf, tile=256):
    b, _, s, _ = q.shape
    tq = tk = tile
    n = s // tile
    pairs = [(qi, ki) for qi in range(n) for ki in range(qi + 1)]
    qi_tab = jnp.asarray([p[0] for p in pairs], jnp.int32)
    ki_tab = jnp.asarray([p[1] for p in pairs], jnp.int32)
    grid_spec = pltpu.PrefetchScalarGridSpec(
        num_scalar_prefetch=2,
        grid=(b, len(pairs)),
        in_specs=[pl.BlockSpec((1, MLA_H, tq, QK_CAT), lambda bi, t, qt, kt: (bi, 0, qt[t], 0)),
                  pl.BlockSpec((1, QK_CAT, tk), lambda bi, t, qt, kt: (bi, 0, kt[t])),
                  pl.BlockSpec((1, tk, KV_LORA), lambda bi, t, qt, kt: (bi, kt[t], 0)),
                  pl.BlockSpec(w_uv_bf.shape, lambda bi, t, qt, kt: (0, 0, 0))],
        out_specs=pl.BlockSpec((1, tq, MLA_H * V_HEAD), lambda bi, t, qt, kt: (bi, qt[t], 0)),
        scratch_shapes=[pltpu.VMEM((MLA_H * tq, LANES), F32), pltpu.VMEM((MLA_H * tq, LANES), F32),
                        pltpu.VMEM((MLA_H * tq, KV_LORA), F32)],
    )
    return pl.pallas_call(
        functools.partial(_mla_prompt_kernel, tq=tq, tk=tk),
        grid_spec=grid_spec,
        out_shape=jax.ShapeDtypeStruct((b, s, MLA_H * V_HEAD), BF16),
        compiler_params=_cp(("parallel", "arbitrary")),
        name="mla_prompt_attend",
    )(qi_tab, ki_tab, q, kc_t, kc, w_uv_bf)


def _mla_step_kernel(pt_ref, q_ref, knew_ref, wuv_ref, *rest, pps):
    ckv_refs = rest[:pps]
    kpe_refs = rest[pps:2 * pps]
    o_ref, m_scr, l_scr, acc_scr = rest[2 * pps:]
    j = pl.program_id(1)
    q = q_ref[0]
    q_lat = q[:, :KV_LORA]
    q_pe = q[:, KV_LORA:KV_LORA + QK_ROPE]

    @pl.when(j == 0)
    def _():
        s0 = _dot_t(q, knew_ref[0])[:, 0:1]
        m_scr[...] = s0
        l_scr[...] = jnp.ones_like(l_scr)
        acc_scr[...] = jnp.broadcast_to(knew_ref[0][0:1, :KV_LORA].astype(F32), acc_scr.shape)

    ks = [r[0].astype(BF16) for r in ckv_refs]
    ss = [_dot_t(q_lat, kb) + _dot_t(q_pe, r[0].astype(BF16)) for kb, r in zip(ks, kpe_refs)]
    m_new = m_scr[...]
    for s in ss:
        m_new = jnp.maximum(m_new, jnp.max(s, -1, keepdims=True))
    alpha = jnp.exp2(m_scr[...] - m_new)
    l_new = alpha * l_scr[...]
    acc = alpha * acc_scr[...]
    for s, kb in zip(ss, ks):
        p = jnp.exp2(s - m_new)
        l_new = l_new + jnp.sum(p, -1, keepdims=True)
        acc = acc + _dot(p.astype(BF16), kb)
    m_scr[...] = m_new
    l_scr[...] = l_new
    acc_scr[...] = acc

    @pl.when(j == pl.num_programs(1) - 1)
    def _():
        o_lat = (acc_scr[...] / l_scr[...]).astype(BF16)
        full = _dot(o_lat, wuv_ref[...])
        rowi = lax.broadcasted_iota(jnp.int32, full.shape, 0)
        grp = lax.broadcasted_iota(jnp.int32, full.shape, 1) // V_HEAD
        o_ref[0] = jnp.sum(jnp.where(rowi == grp, full, 0.0), 0, keepdims=True).astype(o_ref.dtype)


def mla_step_attend(q8, k_new, w_uv_flat_bf, cache_ckv, cache_kpe, page_table, pps=16):
    nb, npg = page_table.shape
    steps = npg // pps
    page = lambda p: (lambda bi, j, pt: (pt[bi, j * pps + p], 0, 0))
    grid_spec = pltpu.PrefetchScalarGridSpec(
        num_scalar_prefetch=1,
        grid=(nb, steps),
        in_specs=[pl.BlockSpec((1, 8, QK_CAT), lambda bi, j, pt: (bi, 0, 0)),
                  pl.BlockSpec((1, 8, QK_CAT), lambda bi, j, pt: (bi, 0, 0)),
                  pl.BlockSpec(w_uv_flat_bf.shape, lambda bi, j, pt: (0, 0))]
                 + [pl.BlockSpec((1, PAGE, KV_LORA), page(p)) for p in range(pps)]
                 + [pl.BlockSpec((1, PAGE, QK_ROPE), page(p)) for p in range(pps)],
        out_specs=pl.BlockSpec((1, 1, MLA_H * V_HEAD), lambda bi, j, pt: (bi, 0, 0)),
        scratch_shapes=[pltpu.VMEM((8, 1), F32), pltpu.VMEM((8, 1), F32),
                        pltpu.VMEM((8, KV_LORA), F32)],
    )
    return pl.pallas_call(
        functools.partial(_mla_step_kernel, pps=pps),
        grid_spec=grid_spec,
        out_shape=jax.ShapeDtypeStruct((nb, 1, MLA_H * V_HEAD), F32),
        compiler_params=_cp(("parallel", "arbitrary")),
        name="mla_step_attend",
    )(page_table, q8, k_new, w_uv_flat_bf, *([cache_ckv] * pps), *([cache_kpe] * pps))


def _swap_halves(w):
    half = w.shape[-1] // 2
    return jnp.concatenate([w[..., half:], w[..., :half]], -1)


def _pad_lanes(w, n=LANES):
    return jnp.pad(w, [(0, 0)] * (w.ndim - 1) + [(0, n - w.shape[-1])])


def _prep_weights(P):
    W = {}
    o1 = CONV_DIM
    o2 = o1 + DN_QK
    o3 = o2 + 2 * DN_H
    wa = P['w_in_a']
    W['in_a_f32'] = jnp.concatenate([wa[..., :o2], _pad_lanes(wa[..., o2:o3]), wa[..., o3:]], -1)
    W['in_a'] = W['in_a_f32'].astype(BF16)
    gate = jnp.zeros((N_A, 2, LANES), F32)
    gate = gate.at[:, 0, DN_H:2 * DN_H].set(P['a_log']).at[:, 1, DN_H:2 * DN_H].set(P['dt_bias'])
    W['gate_rows'] = gate
    W['in_b'] = P['w_in_b'].astype(BF16)
    wqb = P['w_q_b'].reshape(-1, Q_LORA, MLA_H, QK_NOPE + QK_ROPE)
    nope = wqb[..., :QK_NOPE].reshape(-1, Q_LORA, MLA_H * QK_NOPE)
    pe = wqb[..., QK_NOPE:]
    pa = _pad_lanes(pe).reshape(-1, Q_LORA, MLA_H * LANES)
    pb = _pad_lanes(_swap_halves(pe)).reshape(-1, Q_LORA, MLA_H * LANES)
    W['q_b'] = jnp.concatenate([nope, pa, pb], -1).astype(BF16)
    wkv = P['w_kv_a']
    kpe_w = wkv[:, KV_LORA:]
    W['kv_a'] = jnp.concatenate(
        [wkv[:, :KV_LORA], _pad_lanes(kpe_w), _pad_lanes(_swap_halves(kpe_w))], -1).astype(BF16)
    W['uk_t'] = jnp.transpose(P['w_uk'], (1, 2, 0)).astype(BF16)
    W['uv'] = jnp.transpose(P['w_uv'], (1, 0, 2)).astype(BF16)
    W['uv_flat'] = P['w_uv'].reshape(KV_LORA, MLA_H * V_HEAD).astype(BF16)
    W['mem_kv'] = P['w_mem_kv'].astype(BF16)
    W['out'] = P['w_out'].astype(BF16)
    W['gate'] = P['w_gate'].astype(BF16)
    W['up'] = P['w_up'].astype(BF16)
    W['down'] = P['w_down'].astype(BF16)
    W['router'] = _pad_lanes(P['w_router'])
    W['e_gate'] = P['we_gate'].astype(BF16)
    W['e_up'] = P['we_up'].astype(BF16)
    W['e_down'] = P['we_down'].astype(BF16)
    return W


def _rope_tables(pos):
    half = QK_ROPE // 2
    inv = ROPE_THETA ** (-jnp.arange(half, dtype=F32) / half)
    ang = pos.astype(F32)[:, None] * inv[None, :]
    cos = jnp.cos(ang)
    sin = jnp.sin(ang)
    zero = jnp.zeros((pos.shape[0], LANES - QK_ROPE), F32)
    return (jnp.concatenate([cos, cos, zero], -1), jnp.concatenate([-sin, sin, zero], -1))


def _channel_mixer(x2, l, P, W, precise=False):
    i = l // 2
    if l % 2 == 0:
        if precise:
            return ffn(x2, P['g_ffn'][l], P['w_gate'][i], P['w_up'][i], P['w_down'][i])
        return ffn(x2, P['g_ffn'][l], W['gate'][i], W['up'][i], W['down'][i])
    fn = moe_routed if x2.shape[0] % MOE_TILE == 0 else moe
    return fn(x2, P['g_ffn'][l], W['router'][i], W['e_gate'][i], W['e_up'][i], W['e_down'][i],
              P['g_final'], final_norm=(l == DEPTH - 1))


def _prompt_trunk(x, mem_k, mem_v, P, W):
    b, s, _ = x.shape
    m = b * s
    cos_t, sin_t = _rope_tables(jnp.arange(s))
    conv_states, dn_states = [], []
    kc = kc_t = ckv = kpe = None
    for l in range(DEPTH):
        if l < N_A:
            qkv, z, ba, mq = norm_proj(x.reshape(m, D_MODEL), P['g_mix'][l], W['in_a'][l],
                                       (CONV_DIM, DN_QK, LANES, MEM_DIM), (BF16, BF16, F32, BF16),
                                       "in_proj_a")
            qkv = qkv.reshape(b, s, CONV_DIM)
            o_tok, s_new = gdn_prompt(qkv, z.reshape(b, s, DN_QK), ba.reshape(b, s, LANES),
                                      P['conv_w'][l], W['gate_rows'][l], P['g_onorm'][l])
            conv_states.append(qkv[:, s - (CONV_W - 1):, :].astype(F32))
            dn_states.append(s_new)
            mq = mq.reshape(b, s, MEM_DIM)
        else:
            j = l - N_A
            if l == N_A:
                ckv, kpe, kc, kc_t = latent_kv(x, P['g_kv'], W['kv_a'], P['g_ckv'], cos_t, sin_t,
                                               with_transposed=True)
            q, mq = mla_q(x, P['g_mix'][l], W['in_b'][j], P['g_qlora'][j], W['q_b'][j], W['uk_t'],
                          cos_t, sin_t)
            o_tok = mla_prompt_attend(q, kc, kc_t, W['uv'])
        x = mix_out(x, o_tok, mq, mem_k[l], mem_v[l], W['out'][l])
        x = _channel_mixer(x.reshape(m, D_MODEL), l, P, W).reshape(b, s, D_MODEL)
    return x, jnp.stack(conv_states), jnp.stack(dn_states), ckv, kpe


def _sample_trunk(x, mem_k, mem_v, conv_prev, dn_prev, cache_ckv, cache_kpe, page_table, P, W):
    nb = x.shape[0]
    past = page_table.shape[1] * PAGE
    cos_t, sin_t = _rope_tables(jnp.full((nb,), past, jnp.int32))
    x2 = x.reshape(nb, D_MODEL)
    conv_states, dn_states = [], []
    kc = ckv = kpe = None
    for l in range(DEPTH):
        if l < N_A:
            qkv, z, ba, mq = norm_proj(x2, P['g_mix'][l], W['in_a_f32'][l],
                                       (CONV_DIM, DN_QK, LANES, MEM_DIM), (F32, F32, F32, F32),
                                       "in_proj_a_step")
            o_tok, conv_new, s_new = gdn_step(qkv, z, ba, conv_prev[l], dn_prev[l], P['conv_w'][l],
                                              W['gate_rows'][l], P['g_onorm'][l])
            conv_states.append(conv_new)
            dn_states.append(s_new)
        else:
            j = l - N_A
            x3 = x2.reshape(1, nb, D_MODEL)
            if l == N_A:
                ckv, kpe, kc = latent_kv(x3, P['g_kv'], W['kv_a'], P['g_ckv'], cos_t, sin_t,
                                         with_transposed=False)
                ckv = ckv.reshape(nb, 1, KV_LORA)
                kpe = kpe.reshape(nb, 1, QK_ROPE)
                kc = jnp.broadcast_to(kc.reshape(nb, 1, QK_CAT), (nb, 8, QK_CAT))
            q, mq = mla_q(x3, P['g_mix'][l], W['in_b'][j], P['g_qlora'][j], W['q_b'][j], W['uk_t'],
                          cos_t, sin_t)
            q8 = jnp.pad(jnp.transpose(q[0], (1, 0, 2)), ((0, 0), (0, 8 - MLA_H), (0, 0)))
            o_tok = mla_step_attend(q8, kc, W['uv_flat'], cache_ckv, cache_kpe, page_table)
            mq = mq.reshape(nb, MEM_DIM).astype(F32)
        precise = l < N_A
        o_mem = mem_attend_step(mq, mem_k[l], mem_v[l], precise)
        x2 = out_proj(x2, o_tok.reshape(nb, -1), o_mem.reshape(nb, MEM_DIM),
                      P['w_out'][l] if precise else W['out'][l])
        x2 = _channel_mixer(x2, l, P, W, precise)
    return (x2.reshape(nb, 1, D_MODEL), jnp.stack(conv_states), jnp.stack(dn_states), ckv, kpe)


def kernel(x_prompt, x_sample, cache_mem_k, cache_mem_v, cache_ckv, cache_kpe, state_delta, state_conv,
           page_table, mem_prompt, g_mix, g_ffn, g_final, w_in_a, conv_w, a_log, dt_bias, g_onorm,
           w_in_b, g_qlora, w_q_b, g_kv, w_kv_a, g_ckv, w_uk, w_uv, g_mem, w_mem_kv, w_out,
           w_gate, w_up, w_down, w_router, we_gate, we_up, we_down):
    P = dict(g_mix=g_mix, g_ffn=g_ffn, g_final=g_final, w_in_a=w_in_a, conv_w=conv_w, a_log=a_log,
             dt_bias=dt_bias, g_onorm=g_onorm, w_in_b=w_in_b, g_qlora=g_qlora, w_q_b=w_q_b, g_kv=g_kv,
             w_kv_a=w_kv_a, g_ckv=g_ckv, w_uk=w_uk, w_uv=w_uv, w_mem_kv=w_mem_kv, w_out=w_out,
             w_gate=w_gate, w_up=w_up, w_down=w_down, w_router=w_router, we_gate=we_gate,
             we_up=we_up, we_down=we_down)
    W = _prep_weights(P)
    bp = x_prompt.shape[0]
    nl = g_mem.shape[0]
    mk, mv = mem_kv(mem_prompt.reshape(bp * N_MEM, D_MODEL), g_mem, W['mem_kv'])
    mk = mk.reshape(nl, bp, N_MEM, MEM_DIM)
    mv = mv.reshape(nl, bp, N_MEM, MEM_DIM)
    y_p, p_conv, p_delta, p_ckv, p_kpe = _prompt_trunk(x_prompt, mk, mv, P, W)
    nb = x_sample.shape[0]
    cmk = cache_mem_k.reshape(nl, nb, N_MEM, MEM_DIM)
    cmv = cache_mem_v.reshape(nl, nb, N_MEM, MEM_DIM)
    y_s, s_conv, s_delta, s_ckv, s_kpe = _sample_trunk(
        x_sample, cmk, cmv, state_conv, state_delta, cache_ckv, cache_kpe, page_table, P, W)
    p_mem_k = mk.reshape(nl, bp, N_MEM, MEM_H, MEM_HD)
    p_mem_v = mv.reshape(nl, bp, N_MEM, MEM_H, MEM_HD)
    return (y_p, y_s, p_delta, p_conv, p_ckv, p_kpe, p_mem_k, p_mem_v, s_delta, s_conv, s_ckv, s_kpe)
```

```python
import functools
import math

import jax
import jax.numpy as jnp
from jax import lax
from jax.experimental import pallas as pl
from jax.experimental.pallas import tpu as pltpu

F32 = jnp.float32
BF16 = jnp.bfloat16

D_MODEL = 1024
DEPTH = 4
N_A = DEPTH // 2
PAGE = 128
DN_H = 6
DN_D = 128
DN_QK = DN_H * DN_D
CONV_W = 4
CONV_DIM = 3 * DN_QK
DN_CHUNK = 256
MLA_H = 6
Q_LORA = 384
KV_LORA = 256
QK_NOPE = 128
QK_ROPE = 64
V_HEAD = 128
ROPE_THETA = 10000.0
MLA_SCALE = (QK_NOPE + QK_ROPE) ** -0.5
QK_CAT = KV_LORA + 128
N_MEM = 256
MEM_H = 4
MEM_HD = 64
MEM_DIM = MEM_H * MEM_HD
D_FF = 2816
N_EXP = 8
D_FF_E = 1408
MOE_TILE = 1024
MOE_ROWS = 128
EPS = 1e-6

LANES = 128
VMEM_LIMIT = 56 * 1024 * 1024
HI = lax.Precision.HIGHEST


def _cp(sem, vmem=VMEM_LIMIT):
    return pltpu.CompilerParams(dimension_semantics=sem, vmem_limit_bytes=vmem)


def _row_tile(m, pref=512):
    return pref if m % pref == 0 else m


def _rms(x, g):
    return x * lax.rsqrt(jnp.mean(x * x, -1, keepdims=True) + EPS) * g


def _silu(x):
    return x * jax.nn.sigmoid(x)


def _softplus(x):
    return jnp.maximum(x, 0.0) + jnp.log1p(jnp.exp(-jnp.abs(x)))


def _dot(a, b):
    return jnp.dot(a, b, preferred_element_type=F32)


def _mm(a, w):
    if w.dtype == F32:
        return jnp.dot(a.astype(F32), w, precision=HI, preferred_element_type=F32)
    return jnp.dot(a.astype(BF16), w, preferred_element_type=F32)


def _dot_t(a, b):
    return lax.dot_general(a, b, (((1,), (1,)), ((), ())), preferred_element_type=F32)


def _softmax_rows(s):
    m = jnp.max(s, -1, keepdims=True)
    e = jnp.exp(s - m)
    return e / jnp.sum(e, -1, keepdims=True)


def _norm_proj_kernel(x_ref, g_ref, w_ref, *out_refs, splits):
    h = _rms(x_ref[...], g_ref[...]).astype(w_ref.dtype)
    off = 0
    for o_ref, n in zip(out_refs, splits):
        o_ref[...] = _mm(h, w_ref[:, off:off + n]).astype(o_ref.dtype)
        off += n


def norm_proj(x, g, w, layer, splits, dtypes, name):
    m, k = x.shape
    tm = _row_tile(m)
    n = w.shape[2]
    return pl.pallas_call(
        functools.partial(_norm_proj_kernel, splits=splits),
        grid=(m // tm,),
        in_specs=[pl.BlockSpec((tm, k), lambda i: (i, 0)),
                  pl.BlockSpec((1, k), lambda i: (0, 0)),
                  pl.BlockSpec((None, k, n), lambda i: (layer, 0, 0))],
        out_specs=[pl.BlockSpec((tm, s), lambda i: (i, 0)) for s in splits],
        out_shape=[jax.ShapeDtypeStruct((m, s), d) for s, d in zip(splits, dtypes)],
        compiler_params=_cp(("parallel",)),
        name=name,
    )(x, g.reshape(1, k), w)


def _mem_kv_kernel(m_ref, g_ref, w_ref, k_ref, v_ref):
    x = m_ref[...]
    mn = x * lax.rsqrt(jnp.mean(x * x, -1, keepdims=True) + EPS)
    kv = _dot((mn * g_ref[0]).astype(BF16), w_ref[0])
    k_ref[0] = kv[:, :MEM_DIM]
    v_ref[0] = kv[:, MEM_DIM:]


def mem_kv(mem, g_mem, w_mem_kv_bf):
    m = mem.shape[0]
    tm = _row_tile(m)
    nl = g_mem.shape[0]
    out = jax.ShapeDtypeStruct((nl, m, MEM_DIM), F32)
    return pl.pallas_call(
        _mem_kv_kernel,
        grid=(m // tm, nl),
        in_specs=[pl.BlockSpec((tm, D_MODEL), lambda i, l: (i, 0)),
                  pl.BlockSpec((1, 1, D_MODEL), lambda i, l: (l, 0, 0)),
                  pl.BlockSpec((1, D_MODEL, 2 * MEM_DIM), lambda i, l: (l, 0, 0))],
        out_specs=[pl.BlockSpec((1, tm, MEM_DIM), lambda i, l: (l, i, 0))] * 2,
        out_shape=[out, out],
        compiler_params=_cp(("parallel", "arbitrary")),
        name="mem_kv",
    )(mem, g_mem.reshape(nl, 1, D_MODEL), w_mem_kv_bf)


def _bmm(a, b):
    return jnp.einsum('hij,hjk->hik', a, b, preferred_element_type=F32)


def _bmm_t(a, b):
    return jnp.einsum('hid,hjd->hij', a, b, preferred_element_type=F32)


def _tri_inverse_minus_eye(lmat, row, col):
    def same_block(bits):
        return jnp.right_shift(row, bits) == jnp.right_shift(col, bits)

    l1 = jnp.where(same_block(4), lmat, 0.0)
    l1b = l1.astype(BF16)
    l2 = _bmm(l1b, l1b)
    l2b = l2.astype(BF16)
    l4 = _bmm(l2b, l2b)
    l4b = l4.astype(BF16)
    l8 = _bmm(l4b, l4b)
    q = -l1
    q = q + l2 + _bmm(q.astype(BF16), l2b)
    q = q + l4 + _bmm(q.astype(BF16), l4b)
    q = q + l8 + _bmm(q.astype(BF16), l8.astype(BF16))
    bits = 4
    while (1 << bits) < DN_CHUNK:
        cross = jnp.logical_and(same_block(bits + 1), jnp.logical_not(same_block(bits)))
        c = jnp.where(cross, lmat, 0.0)
        qb = q.astype(BF16)
        y = c + _bmm(qb, c.astype(BF16))
        q = q - (y + _bmm(y.astype(BF16), qb))
        bits += 1
    return q


def _gdn_prompt_kernel(u_ref, z_ref, ba_ref, w_ref, gate_ref, gon_ref, o_ref, s_out_ref,
                       halo_ref, s_scr):
    c = pl.program_id(1)
    C = DN_CHUNK

    @pl.when(c == 0)
    def _():
        halo_ref[...] = jnp.zeros_like(halo_ref)
        s_scr[...] = jnp.zeros_like(s_scr)

    u = u_ref[0].astype(F32)
    wc = w_ref[...]
    ext = jnp.concatenate([halo_ref[...], u], axis=0)
    y = u * wc[CONV_W - 1:CONV_W]
    for j in range(1, CONV_W):
        y = y + pltpu.roll(ext, j, axis=0)[8:] * wc[CONV_W - 1 - j:CONV_W - j]
    halo_ref[...] = u[C - 8:]
    y = _silu(y)

    ba = ba_ref[0]
    beta_all = jax.nn.sigmoid(ba)
    g_all = -jnp.exp(gate_ref[0:1]) * _softplus(ba + gate_ref[1:2])
    row = lax.broadcasted_iota(jnp.int32, (C, C), 0)
    col = lax.broadcasted_iota(jnp.int32, (C, C), 1)
    tri = row >= col
    gcum_all = jnp.dot(tri.astype(F32), g_all, precision=HI, preferred_element_type=F32)
    gcum_t = gcum_all.T

    heads = range(DN_H)
    per_head = lambda a, off: jnp.stack([a[:, off + h * DN_D:off + (h + 1) * DN_D] for h in heads])
    beta = jnp.stack([beta_all[:, h:h + 1] for h in heads])
    gc = jnp.stack([gcum_all[:, DN_H + h:DN_H + h + 1] for h in heads])
    gr = jnp.stack([gcum_t[DN_H + h:DN_H + h + 1, :] for h in heads])
    gl = gr[:, :, C - 1:C]
    qh = per_head(y, 0)
    kh = per_head(y, DN_QK)
    vh = per_head(y, 2 * DN_QK)
    qh = qh * lax.rsqrt(jnp.sum(qh * qh, -1, keepdims=True) + EPS) * (DN_D ** -0.5)
    kh = kh * lax.rsqrt(jnp.sum(kh * kh, -1, keepdims=True) + EPS)
    decay = jnp.exp(jnp.where(tri, gc - gr, -jnp.inf))
    kb = kh * beta
    vb = vh * beta
    k_bf = kh.astype(BF16)
    lmat = jnp.where(row > col, _bmm_t(kb.astype(BF16), k_bf) * decay, 0.0)
    qinv = _tri_inverse_minus_eye(lmat, row, col)
    egc = jnp.exp(gc)
    rhs = jnp.concatenate([vb, kb * egc], axis=-1)
    sol = rhs + _bmm(qinv.astype(BF16), rhs.astype(BF16))
    un = sol[:, :, :DN_D]
    wn = sol[:, :, DN_D:]
    qk = _bmm_t(qh.astype(BF16), k_bf) * decay
    qg = qh * egc
    kg = (kh * jnp.exp(gl - gc)).astype(BF16)
    st = s_scr[...]
    ws = _bmm(jnp.concatenate([wn, qg], axis=1).astype(BF16), st.astype(BF16))
    v_new = un - ws[:, :C]
    vn_bf = v_new.astype(BF16)
    o = ws[:, C:] + _bmm(qk.astype(BF16), vn_bf)
    st = st * jnp.exp(gl)
    on = _rms(o, gon_ref[...])
    for h in heads:
        s_scr[h] = st[h] + lax.dot_general(
            kg[h], vn_bf[h], (((0,), (0,)), ((), ())), preferred_element_type=F32)
        zh = z_ref[0, :, h * DN_D:(h + 1) * DN_D].astype(F32)
        o_ref[0, :, h * DN_D:(h + 1) * DN_D] = (on[h] * _silu(zh)).astype(o_ref.dtype)

    @pl.when(c == pl.num_programs(1) - 1)
    def _():
        s_out_ref[0] = s_scr[...]


def gdn_prompt(qkv, z, ba, conv_w, gate_rows, g_onorm):
    b, s, _ = qkv.shape
    C = DN_CHUNK
    blk = lambda n: pl.BlockSpec((1, C, n), lambda bi, c: (bi, c, 0))
    const = lambda a: pl.BlockSpec(a.shape, lambda bi, c: (0, 0))
    g_onorm = g_onorm.reshape(1, DN_D)
    return pl.pallas_call(
        _gdn_prompt_kernel,
        grid=(b, s // C),
        in_specs=[blk(CONV_DIM), blk(DN_QK), blk(LANES), const(conv_w), const(gate_rows),
                  const(g_onorm)],
        out_specs=[blk(DN_QK),
                   pl.BlockSpec((1, DN_H, DN_D, DN_D), lambda bi, c: (bi, 0, 0, 0))],
        out_shape=[jax.ShapeDtypeStruct((b, s, DN_QK), BF16),
                   jax.ShapeDtypeStruct((b, DN_H, DN_D, DN_D), F32)],
        scratch_shapes=[pltpu.VMEM((8, CONV_DIM), F32), pltpu.VMEM((DN_H, DN_D, DN_D), F32)],
        compiler_params=_cp(("parallel", "arbitrary")),
        name="gdn_prompt",
    )(qkv, z, ba, conv_w, gate_rows, g_onorm)


def _gdn_step_kernel(u_ref, z_ref, ba_ref, cs_ref, s_ref, w_ref, gate_ref, gon_ref,
                     o_ref, cs_out_ref, s_out_ref):
    u = u_ref[0].astype(F32)
    prev = cs_ref[0]
    w = w_ref[...]
    y = u * w[CONV_W - 1:CONV_W]
    for j in range(CONV_W - 1):
        y = y + prev[j:j + 1] * w[j:j + 1]
    y = _silu(y)
    cs_out_ref[0, 0:CONV_W - 2, :] = cs_ref[0, 1:CONV_W - 1, :]
    cs_out_ref[0, CONV_W - 2:CONV_W - 1, :] = u
    ba = ba_ref[0]
    beta_all = jax.nn.sigmoid(ba)
    g_all = -jnp.exp(gate_ref[0:1]) * _softplus(ba + gate_ref[1:2])
    eye = (lax.broadcasted_iota(jnp.int32, (DN_D, DN_D), 0)
           == lax.broadcasted_iota(jnp.int32, (DN_D, DN_D), 1))

    def to_col(r):
        return jnp.sum(jnp.where(eye, jnp.broadcast_to(r, (DN_D, DN_D)), 0.0), -1, keepdims=True)

    for h in range(DN_H):
        beta = beta_all[:, h:h + 1]
        eg = jnp.exp(g_all[:, DN_H + h:DN_H + h + 1])
        qh = y[:, h * DN_D:(h + 1) * DN_D]
        kh = y[:, DN_QK + h * DN_D:DN_QK + (h + 1) * DN_D]
        vh = y[:, 2 * DN_QK + h * DN_D:2 * DN_QK + (h + 1) * DN_D]
        qh = qh * lax.rsqrt(jnp.sum(qh * qh, -1, keepdims=True) + EPS) * (DN_D ** -0.5)
        kh = kh * lax.rsqrt(jnp.sum(kh * kh, -1, keepdims=True) + EPS)
        st = s_ref[0, h]
        kcol = to_col(kh)
        qcol = to_col(qh)
        ks = jnp.sum(kcol * st, 0, keepdims=True)
        qs = jnp.sum(qcol * st, 0, keepdims=True)
        v_new = beta * vh - (beta * eg) * ks
        o = eg * qs + jnp.sum(qh * kh, -1, keepdims=True) * v_new
        s_out_ref[0, h] = st * eg + kcol * v_new
        zh = z_ref[0, :, h * DN_D:(h + 1) * DN_D].astype(F32)
        o_ref[0, :, h * DN_D:(h + 1) * DN_D] = (_rms(o, gon_ref[...]) * _silu(zh)).astype(o_ref.dtype)


def gdn_step(qkv, z, ba, conv_state, s_state, layer, conv_w, gate_rows, g_onorm):
    nb = qkv.shape[0]
    i3 = lambda bi: (bi, 0, 0)
    return pl.pallas_call(
        _gdn_step_kernel,
        grid=(nb,),
        in_specs=[pl.BlockSpec((1, 1, CONV_DIM), i3), pl.BlockSpec((1, 1, DN_QK), i3),
                  pl.BlockSpec((1, 1, LANES), i3),
                  pl.BlockSpec((None, 1, CONV_W - 1, CONV_DIM), lambda bi: (layer, bi, 0, 0)),
                  pl.BlockSpec((None, 1, DN_H, DN_D, DN_D), lambda bi: (layer, bi, 0, 0, 0)),
                  pl.BlockSpec((CONV_W, CONV_DIM), lambda bi: (0, 0)),
                  pl.BlockSpec((2, LANES), lambda bi: (0, 0)),
                  pl.BlockSpec((1, DN_D), lambda bi: (0, 0))],
        out_specs=[pl.BlockSpec((1, 1, DN_QK), i3), pl.BlockSpec((1, CONV_W - 1, CONV_DIM), i3),
                   pl.BlockSpec((1, DN_H, DN_D, DN_D), lambda bi: (bi, 0, 0, 0))],
        out_shape=[jax.ShapeDtypeStruct((nb, 1, DN_QK), F32),
                   jax.ShapeDtypeStruct((nb, CONV_W - 1, CONV_DIM), F32),
                   jax.ShapeDtypeStruct((nb, DN_H, DN_D, DN_D), F32)],
        compiler_params=_cp(("parallel",)),
        name="gdn_step",
    )(qkv.reshape(nb, 1, CONV_DIM), z.reshape(nb, 1, DN_QK), ba.reshape(nb, 1, LANES),
      conv_state, s_state, conv_w, gate_rows, g_onorm.reshape(1, DN_D))


def _mem_attend_rows(mq, mk, mv):
    lane_head = lax.broadcasted_iota(jnp.int32, (1, MEM_DIM), 1) // MEM_HD
    out = jnp.zeros((mq.shape[0], MEM_DIM), F32)
    for h in range(MEM_H):
        sel = lane_head == h
        s = _dot_t(mq, jnp.where(sel, mk, 0.0).astype(BF16)) * (MEM_HD ** -0.5)
        p = _softmax_rows(s).astype(BF16)
        out = out + _dot(p, jnp.where(sel, mv, 0.0).astype(BF16))
    return out


def _mix_out_kernel(x_ref, ot_ref, mq_ref, mk_ref, mv_ref, w_ref, o_ref):
    om = _mem_attend_rows(mq_ref[0], mk_ref[0], mv_ref[0]).astype(BF16)
    nt = ot_ref.shape[-1]
    o_ref[0] = x_ref[0] + _dot(ot_ref[0], w_ref[:nt]) + _dot(om, w_ref[nt:])


def mix_out(x, o_tok, mq, mk, mv, w_out_bf, layer, tm=512):
    b, s, _ = x.shape
    nt = o_tok.shape[-1]
    blk = lambda n: pl.BlockSpec((1, tm, n), lambda bi, i: (bi, i, 0))
    kv = pl.BlockSpec((None, 1, N_MEM, MEM_DIM), lambda bi, i: (layer, bi, 0, 0))
    return pl.pallas_call(
        _mix_out_kernel,
        grid=(b, s // tm),
        in_specs=[blk(D_MODEL), blk(nt), blk(MEM_DIM), kv, kv,
                  pl.BlockSpec((None, nt + MEM_DIM, D_MODEL), lambda bi, i: (layer, 0, 0))],
        out_specs=blk(D_MODEL),
        out_shape=jax.ShapeDtypeStruct((b, s, D_MODEL), F32),
        compiler_params=_cp(("parallel", "parallel")),
        name="mix_out",
    )(x, o_tok, mq, mk, mv, w_out_bf)


def _mem_attend_step_kernel(mq_ref, mk_ref, mv_ref, o_ref, *, precise):
    lane_head = lax.broadcasted_iota(jnp.int32, (8, MEM_DIM), 1) // MEM_HD
    rowi = lax.broadcasted_iota(jnp.int32, (8, MEM_DIM), 0)
    sel = lane_head == rowi
    q8 = jnp.where(sel, jnp.broadcast_to(mq_ref[0], (8, MEM_DIM)), 0.0)
    kt = mk_ref[0]
    vt = mv_ref[0]
    if precise:
        s = _mm(q8, kt) * (MEM_HD ** -0.5)
        o8 = lax.dot_general(_softmax_rows(s), vt, (((1,), (1,)), ((), ())), precision=HI,
                             preferred_element_type=F32)
    else:
        s = _dot(q8.astype(BF16), kt.astype(BF16)) * (MEM_HD ** -0.5)
        o8 = _dot_t(_softmax_rows(s).astype(BF16), vt.astype(BF16))
    o_ref[0] = jnp.sum(jnp.where(sel, o8, 0.0), 0, keepdims=True).astype(o_ref.dtype)


def mem_attend_step(mq, mk_t, mv_t, layer, precise):
    nb = mq.shape[0]
    i3 = lambda bi: (bi, 0, 0)
    kv = pl.BlockSpec((None, 1, MEM_DIM, N_MEM), lambda bi: (layer, bi, 0, 0))
    return pl.pallas_call(
        functools.partial(_mem_attend_step_kernel, precise=precise),
        grid=(nb,),
        in_specs=[pl.BlockSpec((1, 1, MEM_DIM), i3), kv, kv],
        out_specs=pl.BlockSpec((1, 1, MEM_DIM), i3),
        out_shape=jax.ShapeDtypeStruct((nb, 1, MEM_DIM), F32),
        compiler_params=_cp(("parallel",)),
        name="mem_attend_step",
    )(mq.reshape(nb, 1, MEM_DIM), mk_t, mv_t)


def _out_proj_kernel(x_ref, ot_ref, om_ref, w_ref, o_ref):
    nt = ot_ref.shape[-1]
    o_ref[...] = x_ref[...] + _mm(ot_ref[...], w_ref[:nt]) + _mm(om_ref[...], w_ref[nt:])


def out_proj(x, o_tok, o_mem, w_out, layer):
    m = x.shape[0]
    nt = o_tok.shape[-1]
    full = lambda a: pl.BlockSpec(a.shape, lambda i: (0, 0))
    return pl.pallas_call(
        _out_proj_kernel,
        grid=(1,),
        in_specs=[full(x), full(o_tok), full(o_mem),
                  pl.BlockSpec((None,) + w_out.shape[1:], lambda i: (layer, 0, 0))],
        out_specs=pl.BlockSpec((m, D_MODEL), lambda i: (0, 0)),
        out_shape=jax.ShapeDtypeStruct((m, D_MODEL), F32),
        compiler_params=_cp(("arbitrary",)),
        name="out_proj",
    )(x, o_tok, o_mem, w_out)


def _ffn_kernel(x_ref, g_ref, wg_ref, wu_ref, wd_ref, o_ref, h_scr, acc_scr):
    f = pl.program_id(1)

    @pl.when(f == 0)
    def _():
        x = x_ref[...]
        h_scr[...] = _rms(x, g_ref[...]).astype(h_scr.dtype)
        acc_scr[...] = x

    h = h_scr[...]
    t = _silu(_mm(h, wg_ref[...])) * _mm(h, wu_ref[...])
    acc_scr[...] += _mm(t, wd_ref[...])

    @pl.when(f == pl.num_programs(1) - 1)
    def _():
        o_ref[...] = acc_scr[...]


def ffn(x, g, wg, wu, wd, layer, tf=1408):
    m = x.shape[0]
    tm = _row_tile(m)
    return pl.pallas_call(
        _ffn_kernel,
        grid=(m // tm, D_FF // tf),
        in_specs=[pl.BlockSpec((tm, D_MODEL), lambda i, f: (i, 0)),
                  pl.BlockSpec((1, D_MODEL), lambda i, f: (0, 0)),
                  pl.BlockSpec((None, D_MODEL, tf), lambda i, f: (layer, 0, f)),
                  pl.BlockSpec((None, D_MODEL, tf), lambda i, f: (layer, 0, f)),
                  pl.BlockSpec((None, tf, D_MODEL), lambda i, f: (layer, f, 0))],
        out_specs=pl.BlockSpec((tm, D_MODEL), lambda i, f: (i, 0)),
        out_shape=jax.ShapeDtypeStruct((m, D_MODEL), F32),
        scratch_shapes=[pltpu.VMEM((tm, D_MODEL), wg.dtype), pltpu.VMEM((tm, D_MODEL), F32)],
        compiler_params=_cp(("parallel", "arbitrary")),
        name="ffn",
    )(x, g.reshape(1, D_MODEL), wg, wu, wd)


def _top2_gates(logits):
    lane = lax.broadcasted_iota(jnp.int32, logits.shape, 1)
    probs = _softmax_rows(logits)
    p1 = jnp.max(probs, -1, keepdims=True)
    i1 = jnp.min(jnp.where(probs == p1, lane, LANES), -1, keepdims=True)
    m1 = lane == i1
    rest = jnp.where(m1, -1.0, probs)
    p2 = jnp.max(rest, -1, keepdims=True)
    i2 = jnp.min(jnp.where(rest == p2, lane, LANES), -1, keepdims=True)
    m2 = lane == i2
    tot = p1 + p2
    return jnp.where(m1, p1 / tot, 0.0) + jnp.where(m2, p2 / tot, 0.0)


def _moe_kernel(x_ref, g_ref, wr_ref, wg_ref, wu_ref, wd_ref, gf_ref, o_ref,
                h_scr, acc_scr, gate_scr, *, final_norm):
    e = pl.program_id(1)

    @pl.when(e == 0)
    def _():
        x = x_ref[...]
        h = _rms(x, g_ref[...])
        h_scr[...] = h.astype(BF16)
        acc_scr[...] = x
        logits = jnp.dot(h, wr_ref[...], precision=HI, preferred_element_type=F32)
        lane = lax.broadcasted_iota(jnp.int32, logits.shape, 1)
        gate_scr[...] = _top2_gates(jnp.where(lane < N_EXP, logits, -jnp.inf))

    h = h_scr[...]
    lane = lax.broadcasted_iota(jnp.int32, gate_scr.shape, 1)
    gate = jnp.sum(jnp.where(lane == e, gate_scr[...], 0.0), -1, keepdims=True)
    t = (_silu(_dot(h, wg_ref[...])) * _dot(h, wu_ref[...])).astype(BF16)
    acc_scr[...] += gate * _dot(t, wd_ref[...])

    @pl.when(e == pl.num_programs(1) - 1)
    def _():
        y = acc_scr[...]
        o_ref[...] = _rms(y, gf_ref[...]) if final_norm else y


def moe(x, g, w_router_pad, wg, wu, wd, g_final, final_norm, layer):
    m = x.shape[0]
    tm = _row_tile(m)
    return pl.pallas_call(
        functools.partial(_moe_kernel, final_norm=final_norm),
        grid=(m // tm, N_EXP),
        in_specs=[pl.BlockSpec((tm, D_MODEL), lambda i, e: (i, 0)),
                  pl.BlockSpec((1, D_MODEL), lambda i, e: (0, 0)),
                  pl.BlockSpec((D_MODEL, LANES), lambda i, e: (0, 0)),
                  pl.BlockSpec((None, None, D_MODEL, D_FF_E), lambda i, e: (layer, e, 0, 0)),
                  pl.BlockSpec((None, None, D_MODEL, D_FF_E), lambda i, e: (layer, e, 0, 0)),
                  pl.BlockSpec((None, None, D_FF_E, D_MODEL), lambda i, e: (layer, e, 0, 0)),
                  pl.BlockSpec((1, D_MODEL), lambda i, e: (0, 0))],
        out_specs=pl.BlockSpec((tm, D_MODEL), lambda i, e: (i, 0)),
        out_shape=jax.ShapeDtypeStruct((m, D_MODEL), F32),
        scratch_shapes=[pltpu.VMEM((tm, D_MODEL), BF16), pltpu.VMEM((tm, D_MODEL), F32),
                        pltpu.VMEM((tm, LANES), F32)],
        compiler_params=_cp(("parallel", "arbitrary")),
        name="moe",
    )(x, g.reshape(1, D_MODEL), w_router_pad, wg, wu, wd, g_final.reshape(1, D_MODEL))


def _moe_routed_kernel(x_ref, g_ref, wr_ref, wg_ref, wu_ref, wd_ref, gf_ref, o_ref,
                       h_scr, acc_scr, rank_scr, gate_t_scr, rank_t_scr, cnt_scr,
                       *, final_norm, rb):
    e = pl.program_id(1)
    tm = x_ref.shape[0]
    sub = 256

    @pl.when(e == 0)
    def _():
        x = x_ref[...]
        h = _rms(x, g_ref[...])
        h_scr[...] = h.astype(BF16)
        acc_scr[...] = x
        logits = jnp.dot(h, wr_ref[...], precision=HI, preferred_element_type=F32)
        lane = lax.broadcasted_iota(jnp.int32, logits.shape, 1)
        gates = _top2_gates(jnp.where(lane < N_EXP, logits, -jnp.inf))
        routed = gates > 0.0
        hit = jnp.where(routed, 1.0, 0.0).astype(BF16)
        col = lax.broadcasted_iota(jnp.int32, (sub, tm), 1)
        for r0 in range(0, tm, sub):
            row = r0 + lax.broadcasted_iota(jnp.int32, (sub, tm), 0)
            before = jnp.where(col < row, 1.0, 0.0).astype(BF16)
            rank = _dot(before, hit)
            rank_scr[r0:r0 + sub] = jnp.where(routed[r0:r0 + sub], rank, -1.0)
        gate_t_scr[...] = gates.T
        rank_t_scr[...] = rank_scr[...].T
        cnt_scr[...] = jnp.broadcast_to(jnp.sum(hit.astype(F32), 0, keepdims=True), cnt_scr.shape)

    lane = lax.broadcasted_iota(jnp.int32, (tm, LANES), 1)
    rank_col = jnp.sum(jnp.where(lane == e, rank_scr[...], 0.0), -1, keepdims=True)
    rank_row = rank_t_scr[pl.ds(e, 1), :]
    gate_row = gate_t_scr[pl.ds(e, 1), :]
    lane1 = lax.broadcasted_iota(jnp.int32, (1, LANES), 1)
    count = jnp.sum(jnp.where(lane1 == e, cnt_scr[0:1], 0.0)).astype(jnp.int32)

    def block(first_rank, rows):
        base = first_rank.astype(F32)
        pick = rank_row == base + lax.broadcasted_iota(jnp.int32, (rows, tm), 0).astype(F32)
        hg = _dot(jnp.where(pick, 1.0, 0.0).astype(BF16), h_scr[...]).astype(BF16)
        gate = jnp.sum(jnp.where(pick, gate_row, 0.0), -1, keepdims=True)
        t = (_silu(_dot(hg, wg_ref[...])) * _dot(hg, wu_ref[...])).astype(BF16)
        y = (gate * _dot(t, wd_ref[...])).astype(BF16)
        put = rank_col == base + lax.broadcasted_iota(jnp.int32, (tm, rows), 1).astype(F32)
        acc_scr[...] += _dot(jnp.where(put, 1.0, 0.0).astype(BF16), y)

    def full_block(j, carry):
        block(j * rb, rb)
        return carry

    n_full = count // rb
    tail = count - n_full * rb
    lax.fori_loop(0, n_full, full_block, 0)
    pl.when(tail > rb // 2)(lambda: block(n_full * rb, rb))
    pl.when(jnp.logical_and(tail > 0, tail <= rb // 2))(lambda: block(n_full * rb, rb // 2))

    @pl.when(e == pl.num_programs(1) - 1)
    def _():
        y = acc_scr[...]
        o_ref[...] = _rms(y, gf_ref[...]) if final_norm else y


def moe_routed(x, g, w_router_pad, wg, wu, wd, g_final, final_norm, layer, tm=MOE_TILE,
               rb=MOE_ROWS):
    m = x.shape[0]
    return pl.pallas_call(
        functools.partial(_moe_routed_kernel, final_norm=final_norm, rb=rb),
        grid=(m // tm, N_EXP),
        in_specs=[pl.BlockSpec((tm, D_MODEL), lambda i, e: (i, 0)),
                  pl.BlockSpec((1, D_MODEL), lambda i, e: (0, 0)),
                  pl.BlockSpec((D_MODEL, LANES), lambda i, e: (0, 0)),
                  pl.BlockSpec((None, None, D_MODEL, D_FF_E), lambda i, e: (layer, e, 0, 0)),
                  pl.BlockSpec((None, None, D_MODEL, D_FF_E), lambda i, e: (layer, e, 0, 0)),
                  pl.BlockSpec((None, None, D_FF_E, D_MODEL), lambda i, e: (layer, e, 0, 0)),
                  pl.BlockSpec((1, D_MODEL), lambda i, e: (0, 0))],
        out_specs=pl.BlockSpec((tm, D_MODEL), lambda i, e: (i, 0)),
        out_shape=jax.ShapeDtypeStruct((m, D_MODEL), F32),
        scratch_shapes=[pltpu.VMEM((tm, D_MODEL), BF16), pltpu.VMEM((tm, D_MODEL), F32),
                        pltpu.VMEM((tm, LANES), F32),
                        pltpu.VMEM((LANES, tm), F32), pltpu.VMEM((LANES, tm), F32),
                        pltpu.VMEM((8, LANES), F32)],
        compiler_params=_cp(("parallel", "arbitrary")),
        name="moe_routed",
    )(x, g.reshape(1, D_MODEL), w_router_pad, wg, wu, wd, g_final.reshape(1, D_MODEL))


def _latent_kv_kernel(x_ref, g_ref, w_ref, gc_ref, cos_ref, sin_ref, ckv_ref, kpe_ref, kc_ref,
                      *maybe_kt_ref):
    h = _rms(x_ref[0], g_ref[...]).astype(BF16)
    kv = _dot(h, w_ref[...])
    ckv = _rms(kv[:, :KV_LORA], gc_ref[...])
    pe = kv[:, KV_LORA:KV_LORA + LANES] * cos_ref[...] + kv[:, KV_LORA + LANES:] * sin_ref[...]
    ckv_ref[0] = ckv
    kpe_ref[0] = pe[:, :QK_ROPE]
    kc = jnp.concatenate([ckv, pe], axis=-1)
    kc_ref[0] = kc.astype(BF16)
    for kt_ref in maybe_kt_ref:
        kt_ref[0] = kc.T.astype(BF16)


def latent_kv(x, g_kv, w_kv_bf, g_ckv, cos_t, sin_t, with_transposed):
    b, s, _ = x.shape
    tm = _row_tile(s)
    blk = lambda n: pl.BlockSpec((1, tm, n), lambda bi, i: (bi, i, 0))
    tab = pl.BlockSpec((tm, LANES), lambda bi, i: (i, 0))
    const = lambda a: pl.BlockSpec(a.shape, lambda bi, i: (0, 0))
    g_kv = g_kv.reshape(1, D_MODEL)
    g_ckv = g_ckv.reshape(1, KV_LORA)
    out_specs = [blk(KV_LORA), blk(QK_ROPE), blk(QK_CAT)]
    out_shape = [jax.ShapeDtypeStruct((b, s, KV_LORA), F32),
                 jax.ShapeDtypeStruct((b, s, QK_ROPE), F32),
                 jax.ShapeDtypeStruct((b, s, QK_CAT), BF16)]
    if with_transposed:
        out_specs.append(pl.BlockSpec((1, QK_CAT, tm), lambda bi, i: (bi, 0, i)))
        out_shape.append(jax.ShapeDtypeStruct((b, QK_CAT, s), BF16))
    return pl.pallas_call(
        _latent_kv_kernel,
        grid=(b, s // tm),
        in_specs=[blk(D_MODEL), const(g_kv), const(w_kv_bf), const(g_ckv), tab, tab],
        out_specs=out_specs,
        out_shape=out_shape,
        compiler_params=_cp(("parallel", "parallel")),
        name="latent_kv",
    )(x, g_kv, w_kv_bf, g_ckv, cos_t, sin_t)


def _mla_q_kernel(x_ref, g_ref, win_ref, gq_ref, wqb_ref, wuk_ref, cos_ref, sin_ref,
                  q_ref, mq_ref):
    h = _rms(x_ref[0], g_ref[...]).astype(BF16)
    proj = _dot(h, win_ref[...])
    mq_ref[0] = proj[:, Q_LORA:].astype(mq_ref.dtype)
    ql = _rms(proj[:, :Q_LORA], gq_ref[...]).astype(BF16)
    nh = MLA_H * QK_NOPE
    for hd in range(MLA_H):
        sl = slice(hd * LANES, (hd + 1) * LANES)
        nope = _dot(ql, wqb_ref[:, sl])
        pa = _dot(ql, wqb_ref[:, nh + hd * LANES:nh + (hd + 1) * LANES])
        pb = _dot(ql, wqb_ref[:, 2 * nh + hd * LANES:2 * nh + (hd + 1) * LANES])
        q_lat = _dot(nope.astype(BF16), wuk_ref[hd])
        q_pe = pa * cos_ref[...] + pb * sin_ref[...]
        q_ref[0, hd] = (jnp.concatenate([q_lat, q_pe], axis=-1)
                        * (MLA_SCALE * math.log2(math.e))).astype(q_ref.dtype)


def mla_q(x, g_mix, w_in_bf, g_ql, w_qb_bf, w_ukt_bf, cos_t, sin_t, layer):
    b, s, _ = x.shape
    tm = _row_tile(s)
    const2 = lambda a: pl.BlockSpec(a.shape, lambda bi, i: (0, 0))
    stacked = lambda a: pl.BlockSpec((None,) + a.shape[1:], lambda bi, i: (layer, 0, 0))
    tab = pl.BlockSpec((tm, LANES), lambda bi, i: (i, 0))
    g_mix = g_mix.reshape(1, D_MODEL)
    g_ql = g_ql.reshape(1, Q_LORA)
    return pl.pallas_call(
        _mla_q_kernel,
        grid=(b, s // tm),
        in_specs=[pl.BlockSpec((1, tm, D_MODEL), lambda bi, i: (bi, i, 0)),
                  const2(g_mix), stacked(w_in_bf), const2(g_ql), stacked(w_qb_bf),
                  pl.BlockSpec(w_ukt_bf.shape, lambda bi, i: (0, 0, 0)), tab, tab],
        out_specs=[pl.BlockSpec((1, MLA_H, tm, QK_CAT), lambda bi, i: (bi, 0, i, 0)),
                   pl.BlockSpec((1, tm, MEM_DIM), lambda bi, i: (bi, i, 0))],
        out_shape=[jax.ShapeDtypeStruct((b, MLA_H, s, QK_CAT), BF16),
                   jax.ShapeDtypeStruct((b, s, MEM_DIM), BF16)],
        compiler_params=_cp(("parallel", "parallel")),
        name="mla_q",
    )(x, g_mix, w_in_bf, g_ql, w_qb_bf, w_ukt_bf, cos_t, sin_t)


def _mla_prompt_kernel(qi_ref, ki_ref, q_ref, kt_ref, v_ref, wuv_ref, o_ref, m_scr, l_scr, acc_scr,
                       *, tq, tk):
    t = pl.program_id(1)
    qi = qi_ref[t]
    ki = ki_ref[t]
    reps = tk // LANES

    def wide(a, n):
        return jnp.concatenate([a] * n, axis=-1)

    @pl.when(ki == 0)
    def _():
        m_scr[...] = jnp.full_like(m_scr, -jnp.inf)
        l_scr[...] = jnp.zeros_like(l_scr)
        acc_scr[...] = jnp.zeros_like(acc_scr)

    def step(masked):
        kt = kt_ref[0]
        v = v_ref[0]
        rows = MLA_H * tq
        s = _dot(q_ref[0].reshape(rows, QK_CAT), kt)
        if masked:
            keep = (lax.broadcasted_iota(jnp.int32, (rows, tk), 1)
                    <= (lax.broadcasted_iota(jnp.int32, (rows, tk), 0) & (tq - 1)))
            s = jnp.where(keep, s, -jnp.inf)
        m_prev = m_scr[...]
        m_new = jnp.maximum(m_prev, jnp.max(s, -1, keepdims=True))
        alpha = jnp.exp2(m_prev - m_new)
        p = jnp.exp2(s - wide(m_new, reps))
        l_scr[...] = alpha * l_scr[...] + jnp.sum(p, -1, keepdims=True)
        acc_scr[...] = wide(alpha, KV_LORA // LANES) * acc_scr[...] + _dot(p.astype(BF16), v)
        m_scr[...] = m_new

    pl.when(ki < qi)(lambda: step(False))

    @pl.when(ki == qi)
    def _():
        step(True)
        for hd in range(MLA_H):
            rs = slice(hd * tq, (hd + 1) * tq)
            o_lat = (acc_scr[rs] / wide(l_scr[rs], KV_LORA // LANES)).astype(BF16)
            o_ref[0, :, hd * V_HEAD:(hd + 1) * V_HEAD] = _dot(o_lat, wuv_ref[hd]).astype(o_ref.dtype)


def mla_prompt_attend(q, kc, kc_t, w_uv_bf, tile=256):
    b, _, s, _ = q.shape
    tq = tk = tile
    n = s // tile
    pairs = [(qi, ki) for qi in range(n) for ki in range(qi + 1)]
    qi_tab = jnp.asarray([p[0] for p in pairs], jnp.int32)
    ki_tab = jnp.asarray([p[1] for p in pairs], jnp.int32)
    grid_spec = pltpu.PrefetchScalarGridSpec(
        num_scalar_prefetch=2,
        grid=(b, len(pairs)),
        in_specs=[pl.BlockSpec((1, MLA_H, tq, QK_CAT), lambda bi, t, qt, kt: (bi, 0, qt[t], 0)),
                  pl.BlockSpec((1, QK_CAT, tk), lambda bi, t, qt, kt: (bi, 0, kt[t])),
                  pl.BlockSpec((1, tk, KV_LORA), lambda bi, t, qt, kt: (bi, kt[t], 0)),
                  pl.BlockSpec(w_uv_bf.shape, lambda bi, t, qt, kt: (0, 0, 0))],
        out_specs=pl.BlockSpec((1, tq, MLA_H * V_HEAD), lambda bi, t, qt, kt: (bi, qt[t], 0)),
        scratch_shapes=[pltpu.VMEM((MLA_H * tq, LANES), F32), pltpu.VMEM((MLA_H * tq, LANES), F32),
                        pltpu.VMEM((MLA_H * tq, KV_LORA), F32)],
    )
    return pl.pallas_call(
        functools.partial(_mla_prompt_kernel, tq=tq, tk=tk),
        grid_spec=grid_spec,
        out_shape=jax.ShapeDtypeStruct((b, s, MLA_H * V_HEAD), BF16),
        compiler_params=_cp(("parallel", "arbitrary")),
        name="mla_prompt_attend",
    )(qi_tab, ki_tab, q, kc_t, kc, w_uv_bf)


def _mla_step_kernel(pt_ref, q_ref, knew_ref, wuv_ref, *rest, pps):
    ckv_refs = rest[:pps]
    kpe_refs = rest[pps:2 * pps]
    o_ref, m_scr, l_scr, acc_scr = rest[2 * pps:]
    j = pl.program_id(1)
    q = q_ref[0]
    q_lat = q[:, :KV_LORA]
    q_pe = q[:, KV_LORA:KV_LORA + QK_ROPE]

    @pl.when(j == 0)
    def _():
        s0 = _dot_t(q, knew_ref[0])[:, 0:1]
        m_scr[...] = s0
        l_scr[...] = jnp.ones_like(l_scr)
        acc_scr[...] = jnp.broadcast_to(knew_ref[0][0:1, :KV_LORA].astype(F32), acc_scr.shape)

    ks = [r[0].astype(BF16) for r in ckv_refs]
    ss = [_dot_t(q_lat, kb) + _dot(q_pe, r[0].astype(BF16)) for kb, r in zip(ks, kpe_refs)]
    m_new = m_scr[...]
    for s in ss:
        m_new = jnp.maximum(m_new, jnp.max(s, -1, keepdims=True))
    alpha = jnp.exp2(m_scr[...] - m_new)
    l_new = alpha * l_scr[...]
    acc = alpha * acc_scr[...]
    for s, kb in zip(ss, ks):
        p = jnp.exp2(s - m_new)
        l_new = l_new + jnp.sum(p, -1, keepdims=True)
        acc = acc + _dot(p.astype(BF16), kb)
    m_scr[...] = m_new
    l_scr[...] = l_new
    acc_scr[...] = acc

    @pl.when(j == pl.num_programs(1) - 1)
    def _():
        o_lat = (acc_scr[...] / l_scr[...]).astype(BF16)
        full = _dot(o_lat, wuv_ref[...])
        rowi = lax.broadcasted_iota(jnp.int32, full.shape, 0)
        grp = lax.broadcasted_iota(jnp.int32, full.shape, 1) // V_HEAD
        o_ref[0] = jnp.sum(jnp.where(rowi == grp, full, 0.0), 0, keepdims=True).astype(o_ref.dtype)


def mla_step_attend(q8, k_new, w_uv_flat_bf, cache_ckv, cache_kpe_t, page_table, pps=16):
    nb, npg = page_table.shape
    steps = npg // pps
    page = lambda p: (lambda bi, j, pt: (pt[bi, j * pps + p], 0, 0))
    grid_spec = pltpu.PrefetchScalarGridSpec(
        num_scalar_prefetch=1,
        grid=(nb, steps),
        in_specs=[pl.BlockSpec((1, 8, QK_CAT), lambda bi, j, pt: (bi, 0, 0)),
                  pl.BlockSpec((1, 8, QK_CAT), lambda bi, j, pt: (bi, 0, 0)),
                  pl.BlockSpec(w_uv_flat_bf.shape, lambda bi, j, pt: (0, 0))]
                 + [pl.BlockSpec((1, PAGE, KV_LORA), page(p)) for p in range(pps)]
                 + [pl.BlockSpec((1, QK_ROPE, PAGE), page(p)) for p in range(pps)],
        out_specs=pl.BlockSpec((1, 1, MLA_H * V_HEAD), lambda bi, j, pt: (bi, 0, 0)),
        scratch_shapes=[pltpu.VMEM((8, 1), F32), pltpu.VMEM((8, 1), F32),
                        pltpu.VMEM((8, KV_LORA), F32)],
    )
    return pl.pallas_call(
        functools.partial(_mla_step_kernel, pps=pps),
        grid_spec=grid_spec,
        out_shape=jax.ShapeDtypeStruct((nb, 1, MLA_H * V_HEAD), F32),
        compiler_params=_cp(("parallel", "arbitrary")),
        name="mla_step_attend",
    )(page_table, q8, k_new, w_uv_flat_bf, *([cache_ckv] * pps), *([cache_kpe_t] * pps))


def _swap_halves(w):
    half = w.shape[-1] // 2
    return jnp.concatenate([w[..., half:], w[..., :half]], -1)


def _pad_lanes(w, n=LANES):
    return jnp.pad(w, [(0, 0)] * (w.ndim - 1) + [(0, n - w.shape[-1])])


def _prep_weights(P):
    W = {}
    o1 = CONV_DIM
    o2 = o1 + DN_QK
    o3 = o2 + 2 * DN_H
    wa = P['w_in_a']
    W['in_a_f32'] = jnp.concatenate([wa[..., :o2], _pad_lanes(wa[..., o2:o3]), wa[..., o3:]], -1)
    W['in_a'] = W['in_a_f32'].astype(BF16)
    gate = jnp.zeros((N_A, 2, LANES), F32)
    gate = gate.at[:, 0, DN_H:2 * DN_H].set(P['a_log']).at[:, 1, DN_H:2 * DN_H].set(P['dt_bias'])
    W['gate_rows'] = gate
    W['in_b'] = P['w_in_b'].astype(BF16)
    wqb = P['w_q_b'].reshape(-1, Q_LORA, MLA_H, QK_NOPE + QK_ROPE)
    nope = wqb[..., :QK_NOPE].reshape(-1, Q_LORA, MLA_H * QK_NOPE)
    pe = wqb[..., QK_NOPE:]
    pa = _pad_lanes(pe).reshape(-1, Q_LORA, MLA_H * LANES)
    pb = _pad_lanes(_swap_halves(pe)).reshape(-1, Q_LORA, MLA_H * LANES)
    W['q_b'] = jnp.concatenate([nope, pa, pb], -1).astype(BF16)
    wkv = P['w_kv_a']
    kpe_w = wkv[:, KV_LORA:]
    W['kv_a'] = jnp.concatenate(
        [wkv[:, :KV_LORA], _pad_lanes(kpe_w), _pad_lanes(_swap_halves(kpe_w))], -1).astype(BF16)
    W['uk_t'] = jnp.transpose(P['w_uk'], (1, 2, 0)).astype(BF16)
    W['uv'] = jnp.transpose(P['w_uv'], (1, 0, 2)).astype(BF16)
    W['uv_flat'] = P['w_uv'].reshape(KV_LORA, MLA_H * V_HEAD).astype(BF16)
    W['mem_kv'] = P['w_mem_kv'].astype(BF16)
    W['out'] = P['w_out'].astype(BF16)
    W['gate'] = P['w_gate'].astype(BF16)
    W['up'] = P['w_up'].astype(BF16)
    W['down'] = P['w_down'].astype(BF16)
    W['router'] = _pad_lanes(P['w_router'])
    W['e_gate'] = P['we_gate'].astype(BF16)
    W['e_up'] = P['we_up'].astype(BF16)
    W['e_down'] = P['we_down'].astype(BF16)
    return W


def _rope_tables(pos):
    half = QK_ROPE // 2
    inv = ROPE_THETA ** (-jnp.arange(half, dtype=F32) / half)
    ang = pos.astype(F32)[:, None] * inv[None, :]
    cos = jnp.cos(ang)
    sin = jnp.sin(ang)
    zero = jnp.zeros((pos.shape[0], LANES - QK_ROPE), F32)
    return (jnp.concatenate([cos, cos, zero], -1), jnp.concatenate([-sin, sin, zero], -1))


def _channel_mixer(x2, l, P, W, precise=False):
    i = l // 2
    if l % 2 == 0:
        if precise:
            return ffn(x2, P['g_ffn'][l], P['w_gate'], P['w_up'], P['w_down'], i)
        return ffn(x2, P['g_ffn'][l], W['gate'], W['up'], W['down'], i)
    fn = moe_routed if x2.shape[0] % MOE_TILE == 0 else moe
    return fn(x2, P['g_ffn'][l], W['router'][i], W['e_gate'], W['e_up'], W['e_down'],
              P['g_final'], final_norm=(l == DEPTH - 1), layer=i)


def _prompt_trunk(x, mem_k, mem_v, P, W):
    b, s, _ = x.shape
    m = b * s
    cos_t, sin_t = _rope_tables(jnp.arange(s))
    conv_states, dn_states = [], []
    kc = kc_t = ckv = kpe = None
    for l in range(DEPTH):
        if l < N_A:
            qkv, z, ba, mq = norm_proj(x.reshape(m, D_MODEL), P['g_mix'][l], W['in_a'], l,
                                       (CONV_DIM, DN_QK, LANES, MEM_DIM), (BF16, BF16, F32, BF16),
                                       "in_proj_a")
            qkv = qkv.reshape(b, s, CONV_DIM)
            o_tok, s_new = gdn_prompt(qkv, z.reshape(b, s, DN_QK), ba.reshape(b, s, LANES),
                                      P['conv_w'][l], W['gate_rows'][l], P['g_onorm'][l])
            conv_states.append(qkv[:, s - (CONV_W - 1):, :].astype(F32))
            dn_states.append(s_new)
            mq = mq.reshape(b, s, MEM_DIM)
        else:
            j = l - N_A
            if l == N_A:
                ckv, kpe, kc, kc_t = latent_kv(x, P['g_kv'], W['kv_a'], P['g_ckv'], cos_t, sin_t,
                                               with_transposed=True)
            q, mq = mla_q(x, P['g_mix'][l], W['in_b'], P['g_qlora'][j], W['q_b'], W['uk_t'],
                          cos_t, sin_t, j)
            o_tok = mla_prompt_attend(q, kc, kc_t, W['uv'])
        x = mix_out(x, o_tok, mq, mem_k, mem_v, W['out'], l)
        x = _channel_mixer(x.reshape(m, D_MODEL), l, P, W).reshape(b, s, D_MODEL)
    return x, jnp.stack(conv_states), jnp.stack(dn_states), ckv, kpe


def _sample_trunk(x, mem_k_t, mem_v_t, conv_prev, dn_prev, cache_ckv, cache_kpe_t, page_table, P, W):
    nb = x.shape[0]
    past = page_table.shape[1] * PAGE
    cos_t, sin_t = _rope_tables(jnp.full((nb,), past, jnp.int32))
    x2 = x.reshape(nb, D_MODEL)
    conv_states, dn_states = [], []
    kc = ckv = kpe = None
    for l in range(DEPTH):
        if l < N_A:
            qkv, z, ba, mq = norm_proj(x2, P['g_mix'][l], W['in_a_f32'], l,
                                       (CONV_DIM, DN_QK, LANES, MEM_DIM), (F32, F32, F32, F32),
                                       "in_proj_a_step")
            o_tok, conv_new, s_new = gdn_step(qkv, z, ba, conv_prev, dn_prev, l, P['conv_w'][l],
                                              W['gate_rows'][l], P['g_onorm'][l])
            conv_states.append(conv_new)
            dn_states.append(s_new)
        else:
            j = l - N_A
            x3 = x2.reshape(1, nb, D_MODEL)
            if l == N_A:
                ckv, kpe, kc = latent_kv(x3, P['g_kv'], W['kv_a'], P['g_ckv'], cos_t, sin_t,
                                         with_transposed=False)
                ckv = ckv.reshape(nb, 1, KV_LORA)
                kpe = kpe.reshape(nb, 1, QK_ROPE)
                kc = jnp.broadcast_to(kc.reshape(nb, 1, QK_CAT), (nb, 8, QK_CAT))
            q, mq = mla_q(x3, P['g_mix'][l], W['in_b'], P['g_qlora'][j], W['q_b'], W['uk_t'],
                          cos_t, sin_t, j)
            q8 = jnp.pad(jnp.transpose(q[0], (1, 0, 2)), ((0, 0), (0, 8 - MLA_H), (0, 0)))
            o_tok = mla_step_attend(q8, kc, W['uv_flat'], cache_ckv, cache_kpe_t, page_table)
            mq = mq.reshape(nb, MEM_DIM).astype(F32)
        precise = l < N_A
        o_mem = mem_attend_step(mq, mem_k_t, mem_v_t, l, precise)
        x2 = out_proj(x2, o_tok.reshape(nb, -1), o_mem.reshape(nb, MEM_DIM),
                      P['w_out'] if precise else W['out'], l)
        x2 = _channel_mixer(x2, l, P, W, precise)
    return (x2.reshape(nb, 1, D_MODEL), jnp.stack(conv_states), jnp.stack(dn_states), ckv, kpe)


def kernel(x_prompt, x_sample, cache_mem_k, cache_mem_v, cache_ckv, cache_kpe, state_delta, state_conv,
           page_table, mem_prompt, g_mix, g_ffn, g_final, w_in_a, conv_w, a_log, dt_bias, g_onorm,
           w_in_b, g_qlora, w_q_b, g_kv, w_kv_a, g_ckv, w_uk, w_uv, g_mem, w_mem_kv, w_out,
           w_gate, w_up, w_down, w_router, we_gate, we_up, we_down):
    P = dict(g_mix=g_mix, g_ffn=g_ffn, g_final=g_final, w_in_a=w_in_a, conv_w=conv_w, a_log=a_log,
             dt_bias=dt_bias, g_onorm=g_onorm, w_in_b=w_in_b, g_qlora=g_qlora, w_q_b=w_q_b, g_kv=g_kv,
             w_kv_a=w_kv_a, g_ckv=g_ckv, w_uk=w_uk, w_uv=w_uv, w_mem_kv=w_mem_kv, w_out=w_out,
             w_gate=w_gate, w_up=w_up, w_down=w_down, w_router=w_router, we_gate=we_gate,
             we_up=we_up, we_down=we_down)
    W = _prep_weights(P)
    bp = x_prompt.shape[0]
    nl = g_mem.shape[0]
    mk, mv = mem_kv(mem_prompt.reshape(bp * N_MEM, D_MODEL), g_mem, W['mem_kv'])
    mk = mk.reshape(nl, bp, N_MEM, MEM_DIM)
    mv = mv.reshape(nl, bp, N_MEM, MEM_DIM)
    y_p, p_conv, p_delta, p_ckv, p_kpe = _prompt_trunk(x_prompt, mk, mv, P, W)
    nb = x_sample.shape[0]
    to_feat_tok = lambda c: jnp.transpose(c, (0, 1, 3, 4, 2)).reshape(nl, nb, MEM_DIM, N_MEM)
    y_s, s_conv, s_delta, s_ckv, s_kpe = _sample_trunk(
        x_sample, to_feat_tok(cache_mem_k), to_feat_tok(cache_mem_v), state_conv, state_delta,
        cache_ckv, jnp.swapaxes(cache_kpe, 1, 2), page_table, P, W)
    p_mem_k = mk.reshape(nl, bp, N_MEM, MEM_H, MEM_HD)
    p_mem_v = mv.reshape(nl, bp, N_MEM, MEM_H, MEM_HD)
    return (y_p, y_s, p_delta, p_conv, p_ckv, p_kpe, p_mem_k, p_mem_v, s_delta, s_conv, s_ckv, s_kpe)
```

```python
import functools
import math

import jax
import jax.numpy as jnp
from jax import lax
from jax.experimental import pallas as pl
from jax.experimental.pallas import tpu as pltpu

F32 = jnp.float32
BF16 = jnp.bfloat16

D_MODEL = 1024
DEPTH = 4
N_A = DEPTH // 2
PAGE = 128
DN_H = 6
DN_D = 128
DN_QK = DN_H * DN_D
CONV_W = 4
CONV_DIM = 3 * DN_QK
DN_CHUNK = 256
MLA_H = 6
Q_LORA = 384
KV_LORA = 256
QK_NOPE = 128
QK_ROPE = 64
V_HEAD = 128
ROPE_THETA = 10000.0
MLA_SCALE = (QK_NOPE + QK_ROPE) ** -0.5
QK_CAT = KV_LORA + 128
N_MEM = 256
MEM_H = 4
MEM_HD = 64
MEM_DIM = MEM_H * MEM_HD
D_FF = 2816
N_EXP = 8
D_FF_E = 1408
MOE_TILE = 1024
MOE_ROWS = 128
EPS = 1e-6

LANES = 128
VMEM_LIMIT = 56 * 1024 * 1024
HI = lax.Precision.HIGHEST


def _cp(sem, vmem=VMEM_LIMIT):
    return pltpu.CompilerParams(dimension_semantics=sem, vmem_limit_bytes=vmem)


def _row_tile(m, pref=512):
    return pref if m % pref == 0 else m


def _rms(x, g):
    return x * lax.rsqrt(jnp.mean(x * x, -1, keepdims=True) + EPS) * g


def _silu(x):
    return x * jax.nn.sigmoid(x)


def _softplus(x):
    return jnp.maximum(x, 0.0) + jnp.log1p(jnp.exp(-jnp.abs(x)))


def _dot(a, b):
    return jnp.dot(a, b, preferred_element_type=F32)


def _mm(a, w):
    if w.dtype == F32:
        return jnp.dot(a.astype(F32), w, precision=HI, preferred_element_type=F32)
    return jnp.dot(a.astype(BF16), w, preferred_element_type=F32)


def _dot_t(a, b):
    return lax.dot_general(a, b, (((1,), (1,)), ((), ())), preferred_element_type=F32)


def _softmax_rows(s):
    m = jnp.max(s, -1, keepdims=True)
    e = jnp.exp(s - m)
    return e / jnp.sum(e, -1, keepdims=True)


def _norm_proj_kernel(x_ref, g_ref, w_ref, *out_refs, splits):
    h = _rms(x_ref[...], g_ref[...]).astype(w_ref.dtype)
    off = 0
    for o_ref, n in zip(out_refs, splits):
        o_ref[...] = _mm(h, w_ref[:, off:off + n]).astype(o_ref.dtype)
        off += n


def norm_proj(x, g, w, layer, splits, dtypes, name):
    m, k = x.shape
    tm = _row_tile(m)
    n = w.shape[2]
    return pl.pallas_call(
        functools.partial(_norm_proj_kernel, splits=splits),
        grid=(m // tm,),
        in_specs=[pl.BlockSpec((tm, k), lambda i: (i, 0)),
                  pl.BlockSpec((1, k), lambda i: (0, 0)),
                  pl.BlockSpec((None, k, n), lambda i: (layer, 0, 0))],
        out_specs=[pl.BlockSpec((tm, s), lambda i: (i, 0)) for s in splits],
        out_shape=[jax.ShapeDtypeStruct((m, s), d) for s, d in zip(splits, dtypes)],
        compiler_params=_cp(("parallel",)),
        name=name,
    )(x, g.reshape(1, k), w)


def _mem_kv_kernel(m_ref, g_ref, w_ref, k_ref, v_ref):
    x = m_ref[...]
    mn = x * lax.rsqrt(jnp.mean(x * x, -1, keepdims=True) + EPS)
    kv = _dot((mn * g_ref[0]).astype(BF16), w_ref[0])
    k_ref[0] = kv[:, :MEM_DIM]
    v_ref[0] = kv[:, MEM_DIM:]


def mem_kv(mem, g_mem, w_mem_kv_bf):
    m = mem.shape[0]
    tm = _row_tile(m)
    nl = g_mem.shape[0]
    out = jax.ShapeDtypeStruct((nl, m, MEM_DIM), F32)
    return pl.pallas_call(
        _mem_kv_kernel,
        grid=(m // tm, nl),
        in_specs=[pl.BlockSpec((tm, D_MODEL), lambda i, l: (i, 0)),
                  pl.BlockSpec((1, 1, D_MODEL), lambda i, l: (l, 0, 0)),
                  pl.BlockSpec((1, D_MODEL, 2 * MEM_DIM), lambda i, l: (l, 0, 0))],
        out_specs=[pl.BlockSpec((1, tm, MEM_DIM), lambda i, l: (l, i, 0))] * 2,
        out_shape=[out, out],
        compiler_params=_cp(("parallel", "arbitrary")),
        name="mem_kv",
    )(mem, g_mem.reshape(nl, 1, D_MODEL), w_mem_kv_bf)


def _bmm(a, b):
    return jnp.einsum('hij,hjk->hik', a, b, preferred_element_type=F32)


def _bmm_t(a, b):
    return jnp.einsum('hid,hjd->hij', a, b, preferred_element_type=F32)


def _tri_inverse_minus_eye(lmat, row, col):
    def same_block(bits):
        return jnp.right_shift(row, bits) == jnp.right_shift(col, bits)

    l1 = jnp.where(same_block(4), lmat, 0.0)
    l1b = l1.astype(BF16)
    l2 = _bmm(l1b, l1b)
    l2b = l2.astype(BF16)
    l4 = _bmm(l2b, l2b)
    l4b = l4.astype(BF16)
    l8 = _bmm(l4b, l4b)
    q = -l1
    q = q + l2 + _bmm(q.astype(BF16), l2b)
    q = q + l4 + _bmm(q.astype(BF16), l4b)
    q = q + l8 + _bmm(q.astype(BF16), l8.astype(BF16))
    bits = 4
    while (1 << bits) < DN_CHUNK:
        cross = jnp.logical_and(same_block(bits + 1), jnp.logical_not(same_block(bits)))
        c = jnp.where(cross, lmat, 0.0)
        qb = q.astype(BF16)
        y = c + _bmm(qb, c.astype(BF16))
        q = q - (y + _bmm(y.astype(BF16), qb))
        bits += 1
    return q


def _gdn_prompt_kernel(u_ref, z_ref, ba_ref, w_ref, gate_ref, gon_ref, o_ref, s_out_ref,
                       halo_ref, s_scr):
    c = pl.program_id(1)
    C = DN_CHUNK

    @pl.when(c == 0)
    def _():
        halo_ref[...] = jnp.zeros_like(halo_ref)
        s_scr[...] = jnp.zeros_like(s_scr)

    u = u_ref[0].astype(F32)
    wc = w_ref[...]
    ext = jnp.concatenate([halo_ref[...], u], axis=0)
    y = u * wc[CONV_W - 1:CONV_W]
    for j in range(1, CONV_W):
        y = y + pltpu.roll(ext, j, axis=0)[8:] * wc[CONV_W - 1 - j:CONV_W - j]
    halo_ref[...] = u[C - 8:]
    y = _silu(y)

    ba = ba_ref[0]
    beta_all = jax.nn.sigmoid(ba)
    g_all = -jnp.exp(gate_ref[0:1]) * _softplus(ba + gate_ref[1:2])
    row = lax.broadcasted_iota(jnp.int32, (C, C), 0)
    col = lax.broadcasted_iota(jnp.int32, (C, C), 1)
    tri = row >= col
    gcum_all = jnp.dot(tri.astype(F32), g_all, precision=HI, preferred_element_type=F32)
    gcum_t = gcum_all.T

    heads = range(DN_H)
    per_head = lambda a, off: jnp.stack([a[:, off + h * DN_D:off + (h + 1) * DN_D] for h in heads])
    beta = jnp.stack([beta_all[:, h:h + 1] for h in heads])
    gc = jnp.stack([gcum_all[:, DN_H + h:DN_H + h + 1] for h in heads])
    gr = jnp.stack([gcum_t[DN_H + h:DN_H + h + 1, :] for h in heads])
    gl = gr[:, :, C - 1:C]
    qh = per_head(y, 0)
    kh = per_head(y, DN_QK)
    vh = per_head(y, 2 * DN_QK)
    qh = qh * lax.rsqrt(jnp.sum(qh * qh, -1, keepdims=True) + EPS) * (DN_D ** -0.5)
    kh = kh * lax.rsqrt(jnp.sum(kh * kh, -1, keepdims=True) + EPS)
    decay = jnp.exp(jnp.where(tri, gc - gr, -jnp.inf))
    kb = kh * beta
    vb = vh * beta
    k_bf = kh.astype(BF16)
    lmat = jnp.where(row > col, _bmm_t(kb.astype(BF16), k_bf) * decay, 0.0)
    qinv = _tri_inverse_minus_eye(lmat, row, col)
    egc = jnp.exp(gc)
    rhs = jnp.concatenate([vb, kb * egc], axis=-1)
    sol = rhs + _bmm(qinv.astype(BF16), rhs.astype(BF16))
    un = sol[:, :, :DN_D]
    wn = sol[:, :, DN_D:]
    qk = _bmm_t(qh.astype(BF16), k_bf) * decay
    qg = qh * egc
    kg = (kh * jnp.exp(gl - gc)).astype(BF16)
    st = s_scr[...]
    ws = _bmm(jnp.concatenate([wn, qg], axis=1).astype(BF16), st.astype(BF16))
    v_new = un - ws[:, :C]
    vn_bf = v_new.astype(BF16)
    o = ws[:, C:] + _bmm(qk.astype(BF16), vn_bf)
    st = st * jnp.exp(gl)
    on = _rms(o, gon_ref[...])
    for h in heads:
        s_scr[h] = st[h] + lax.dot_general(
            kg[h], vn_bf[h], (((0,), (0,)), ((), ())), preferred_element_type=F32)
        zh = z_ref[0, :, h * DN_D:(h + 1) * DN_D].astype(F32)
        o_ref[0, :, h * DN_D:(h + 1) * DN_D] = (on[h] * _silu(zh)).astype(o_ref.dtype)

    @pl.when(c == pl.num_programs(1) - 1)
    def _():
        s_out_ref[0] = s_scr[...]


def gdn_prompt(qkv, z, ba, conv_w, gate_rows, g_onorm):
    b, s, _ = qkv.shape
    C = DN_CHUNK
    blk = lambda n: pl.BlockSpec((1, C, n), lambda bi, c: (bi, c, 0))
    const = lambda a: pl.BlockSpec(a.shape, lambda bi, c: (0, 0))
    g_onorm = g_onorm.reshape(1, DN_D)
    return pl.pallas_call(
        _gdn_prompt_kernel,
        grid=(b, s // C),
        in_specs=[blk(CONV_DIM), blk(DN_QK), blk(LANES), const(conv_w), const(gate_rows),
                  const(g_onorm)],
        out_specs=[blk(DN_QK),
                   pl.BlockSpec((1, DN_H, DN_D, DN_D), lambda bi, c: (bi, 0, 0, 0))],
        out_shape=[jax.ShapeDtypeStruct((b, s, DN_QK), BF16),
                   jax.ShapeDtypeStruct((b, DN_H, DN_D, DN_D), F32)],
        scratch_shapes=[pltpu.VMEM((8, CONV_DIM), F32), pltpu.VMEM((DN_H, DN_D, DN_D), F32)],
        compiler_params=_cp(("parallel", "arbitrary")),
        name="gdn_prompt",
    )(qkv, z, ba, conv_w, gate_rows, g_onorm)


def _gdn_step_kernel(u_ref, z_ref, ba_ref, cs_ref, s_ref, w_ref, gate_ref, gon_ref,
                     o_ref, cs_out_ref, s_out_ref):
    u = u_ref[0].astype(F32)
    prev = cs_ref[0]
    w = w_ref[...]
    y = u * w[CONV_W - 1:CONV_W]
    for j in range(CONV_W - 1):
        y = y + prev[j:j + 1] * w[j:j + 1]
    y = _silu(y)
    cs_out_ref[0, 0:CONV_W - 2, :] = cs_ref[0, 1:CONV_W - 1, :]
    cs_out_ref[0, CONV_W - 2:CONV_W - 1, :] = u
    ba = ba_ref[0]
    beta_all = jax.nn.sigmoid(ba)
    g_all = -jnp.exp(gate_ref[0:1]) * _softplus(ba + gate_ref[1:2])
    eye = (lax.broadcasted_iota(jnp.int32, (DN_D, DN_D), 0)
           == lax.broadcasted_iota(jnp.int32, (DN_D, DN_D), 1))

    def to_col(r):
        return jnp.sum(jnp.where(eye, jnp.broadcast_to(r, (DN_D, DN_D)), 0.0), -1, keepdims=True)

    for h in range(DN_H):
        beta = beta_all[:, h:h + 1]
        eg = jnp.exp(g_all[:, DN_H + h:DN_H + h + 1])
        qh = y[:, h * DN_D:(h + 1) * DN_D]
        kh = y[:, DN_QK + h * DN_D:DN_QK + (h + 1) * DN_D]
        vh = y[:, 2 * DN_QK + h * DN_D:2 * DN_QK + (h + 1) * DN_D]
        qh = qh * lax.rsqrt(jnp.sum(qh * qh, -1, keepdims=True) + EPS) * (DN_D ** -0.5)
        kh = kh * lax.rsqrt(jnp.sum(kh * kh, -1, keepdims=True) + EPS)
        st = s_ref[0, h]
        kcol = to_col(kh)
        qcol = to_col(qh)
        ks = jnp.sum(kcol * st, 0, keepdims=True)
        qs = jnp.sum(qcol * st, 0, keepdims=True)
        v_new = beta * vh - (beta * eg) * ks
        o = eg * qs + jnp.sum(qh * kh, -1, keepdims=True) * v_new
        s_out_ref[0, h] = st * eg + kcol * v_new
        zh = z_ref[0, :, h * DN_D:(h + 1) * DN_D].astype(F32)
        o_ref[0, :, h * DN_D:(h + 1) * DN_D] = (_rms(o, gon_ref[...]) * _silu(zh)).astype(o_ref.dtype)


def gdn_step(qkv, z, ba, conv_state, s_state, layer, conv_w, gate_rows, g_onorm):
    nb = qkv.shape[0]
    i3 = lambda bi: (bi, 0, 0)
    return pl.pallas_call(
        _gdn_step_kernel,
        grid=(nb,),
        in_specs=[pl.BlockSpec((1, 1, CONV_DIM), i3), pl.BlockSpec((1, 1, DN_QK), i3),
                  pl.BlockSpec((1, 1, LANES), i3),
                  pl.BlockSpec((None, 1, CONV_W - 1, CONV_DIM), lambda bi: (layer, bi, 0, 0)),
                  pl.BlockSpec((None, 1, DN_H, DN_D, DN_D), lambda bi: (layer, bi, 0, 0, 0)),
                  pl.BlockSpec((CONV_W, CONV_DIM), lambda bi: (0, 0)),
                  pl.BlockSpec((2, LANES), lambda bi: (0, 0)),
                  pl.BlockSpec((1, DN_D), lambda bi: (0, 0))],
        out_specs=[pl.BlockSpec((1, 1, DN_QK), i3), pl.BlockSpec((1, CONV_W - 1, CONV_DIM), i3),
                   pl.BlockSpec((1, DN_H, DN_D, DN_D), lambda bi: (bi, 0, 0, 0))],
        out_shape=[jax.ShapeDtypeStruct((nb, 1, DN_QK), F32),
                   jax.ShapeDtypeStruct((nb, CONV_W - 1, CONV_DIM), F32),
                   jax.ShapeDtypeStruct((nb, DN_H, DN_D, DN_D), F32)],
        compiler_params=_cp(("parallel",)),
        name="gdn_step",
    )(qkv.reshape(nb, 1, CONV_DIM), z.reshape(nb, 1, DN_QK), ba.reshape(nb, 1, LANES),
      conv_state, s_state, conv_w, gate_rows, g_onorm.reshape(1, DN_D))


def _mem_attend_rows(mq, mk, mv):
    lane_head = lax.broadcasted_iota(jnp.int32, (1, MEM_DIM), 1) // MEM_HD
    out = jnp.zeros((mq.shape[0], MEM_DIM), F32)
    for h in range(MEM_H):
        sel = lane_head == h
        s = _dot_t(mq, jnp.where(sel, mk, 0.0).astype(BF16)) * (MEM_HD ** -0.5)
        p = _softmax_rows(s).astype(BF16)
        out = out + _dot(p, jnp.where(sel, mv, 0.0).astype(BF16))
    return out


def _mix_out_kernel(x_ref, ot_ref, mq_ref, mk_ref, mv_ref, w_ref, o_ref):
    om = _mem_attend_rows(mq_ref[0], mk_ref[0], mv_ref[0]).astype(BF16)
    nt = ot_ref.shape[-1]
    o_ref[0] = x_ref[0] + _dot(ot_ref[0], w_ref[:nt]) + _dot(om, w_ref[nt:])


def mix_out(x, o_tok, mq, mk, mv, w_out_bf, layer, tm=512):
    b, s, _ = x.shape
    nt = o_tok.shape[-1]
    blk = lambda n: pl.BlockSpec((1, tm, n), lambda bi, i: (bi, i, 0))
    kv = pl.BlockSpec((None, 1, N_MEM, MEM_DIM), lambda bi, i: (layer, bi, 0, 0))
    return pl.pallas_call(
        _mix_out_kernel,
        grid=(b, s // tm),
        in_specs=[blk(D_MODEL), blk(nt), blk(MEM_DIM), kv, kv,
                  pl.BlockSpec((None, nt + MEM_DIM, D_MODEL), lambda bi, i: (layer, 0, 0))],
        out_specs=blk(D_MODEL),
        out_shape=jax.ShapeDtypeStruct((b, s, D_MODEL), F32),
        compiler_params=_cp(("parallel", "parallel")),
        name="mix_out",
    )(x, o_tok, mq, mk, mv, w_out_bf)


def _mem_attend_step_kernel(mq_ref, mk_ref, mv_ref, o_ref, *, precise):
    lane_head = lax.broadcasted_iota(jnp.int32, (8, MEM_DIM), 1) // MEM_HD
    rowi = lax.broadcasted_iota(jnp.int32, (8, MEM_DIM), 0)
    sel = lane_head == rowi
    q8 = jnp.where(sel, jnp.broadcast_to(mq_ref[0], (8, MEM_DIM)), 0.0)
    kt = mk_ref[0]
    vt = mv_ref[0]
    if precise:
        s = _mm(q8, kt) * (MEM_HD ** -0.5)
        o8 = lax.dot_general(_softmax_rows(s), vt, (((1,), (1,)), ((), ())), precision=HI,
                             preferred_element_type=F32)
    else:
        s = _dot(q8.astype(BF16), kt.astype(BF16)) * (MEM_HD ** -0.5)
        o8 = _dot_t(_softmax_rows(s).astype(BF16), vt.astype(BF16))
    o_ref[0] = jnp.sum(jnp.where(sel, o8, 0.0), 0, keepdims=True).astype(o_ref.dtype)


def mem_attend_step(mq, mk_t, mv_t, layer, precise):
    nb = mq.shape[0]
    i3 = lambda bi: (bi, 0, 0)
    kv = pl.BlockSpec((None, 1, MEM_DIM, N_MEM), lambda bi: (layer, bi, 0, 0))
    return pl.pallas_call(
        functools.partial(_mem_attend_step_kernel, precise=precise),
        grid=(nb,),
        in_specs=[pl.BlockSpec((1, 1, MEM_DIM), i3), kv, kv],
        out_specs=pl.BlockSpec((1, 1, MEM_DIM), i3),
        out_shape=jax.ShapeDtypeStruct((nb, 1, MEM_DIM), F32),
        compiler_params=_cp(("parallel",)),
        name="mem_attend_step",
    )(mq.reshape(nb, 1, MEM_DIM), mk_t, mv_t)


def _out_proj_kernel(x_ref, ot_ref, om_ref, w_ref, o_ref):
    nt = ot_ref.shape[-1]
    o_ref[...] = x_ref[...] + _mm(ot_ref[...], w_ref[:nt]) + _mm(om_ref[...], w_ref[nt:])


def out_proj(x, o_tok, o_mem, w_out, layer):
    m = x.shape[0]
    nt = o_tok.shape[-1]
    full = lambda a: pl.BlockSpec(a.shape, lambda i: (0, 0))
    return pl.pallas_call(
        _out_proj_kernel,
        grid=(1,),
        in_specs=[full(x), full(o_tok), full(o_mem),
                  pl.BlockSpec((None,) + w_out.shape[1:], lambda i: (layer, 0, 0))],
        out_specs=pl.BlockSpec((m, D_MODEL), lambda i: (0, 0)),
        out_shape=jax.ShapeDtypeStruct((m, D_MODEL), F32),
        compiler_params=_cp(("arbitrary",)),
        name="out_proj",
    )(x, o_tok, o_mem, w_out)


def _ffn_kernel(x_ref, g_ref, wg_ref, wu_ref, wd_ref, o_ref, h_scr, acc_scr):
    f = pl.program_id(1)

    @pl.when(f == 0)
    def _():
        x = x_ref[...]
        h_scr[...] = _rms(x, g_ref[...]).astype(h_scr.dtype)
        acc_scr[...] = x

    h = h_scr[...]
    t = _silu(_mm(h, wg_ref[...])) * _mm(h, wu_ref[...])
    acc_scr[...] += _mm(t, wd_ref[...])

    @pl.when(f == pl.num_programs(1) - 1)
    def _():
        o_ref[...] = acc_scr[...]


def ffn(x, g, wg, wu, wd, layer, tf=1408):
    m = x.shape[0]
    tm = _row_tile(m)
    return pl.pallas_call(
        _ffn_kernel,
        grid=(m // tm, D_FF // tf),
        in_specs=[pl.BlockSpec((tm, D_MODEL), lambda i, f: (i, 0)),
                  pl.BlockSpec((1, D_MODEL), lambda i, f: (0, 0)),
                  pl.BlockSpec((None, D_MODEL, tf), lambda i, f: (layer, 0, f)),
                  pl.BlockSpec((None, D_MODEL, tf), lambda i, f: (layer, 0, f)),
                  pl.BlockSpec((None, tf, D_MODEL), lambda i, f: (layer, f, 0))],
        out_specs=pl.BlockSpec((tm, D_MODEL), lambda i, f: (i, 0)),
        out_shape=jax.ShapeDtypeStruct((m, D_MODEL), F32),
        scratch_shapes=[pltpu.VMEM((tm, D_MODEL), wg.dtype), pltpu.VMEM((tm, D_MODEL), F32)],
        compiler_params=_cp(("parallel", "arbitrary")),
        name="ffn",
    )(x, g.reshape(1, D_MODEL), wg, wu, wd)


def _top2_gates(logits):
    lane = lax.broadcasted_iota(jnp.int32, logits.shape, 1)
    probs = _softmax_rows(logits)
    p1 = jnp.max(probs, -1, keepdims=True)
    i1 = jnp.min(jnp.where(probs == p1, lane, LANES), -1, keepdims=True)
    m1 = lane == i1
    rest = jnp.where(m1, -1.0, probs)
    p2 = jnp.max(rest, -1, keepdims=True)
    i2 = jnp.min(jnp.where(rest == p2, lane, LANES), -1, keepdims=True)
    m2 = lane == i2
    tot = p1 + p2
    return jnp.where(m1, p1 / tot, 0.0) + jnp.where(m2, p2 / tot, 0.0)


def _moe_kernel(x_ref, g_ref, wr_ref, wg_ref, wu_ref, wd_ref, gf_ref, o_ref,
                h_scr, acc_scr, gate_scr, *, final_norm):
    e = pl.program_id(1)

    @pl.when(e == 0)
    def _():
        x = x_ref[...]
        h = _rms(x, g_ref[...])
        h_scr[...] = h.astype(BF16)
        acc_scr[...] = x
        logits = jnp.dot(h, wr_ref[...], precision=HI, preferred_element_type=F32)
        lane = lax.broadcasted_iota(jnp.int32, logits.shape, 1)
        gate_scr[...] = _top2_gates(jnp.where(lane < N_EXP, logits, -jnp.inf))

    h = h_scr[...]
    lane = lax.broadcasted_iota(jnp.int32, gate_scr.shape, 1)
    gate = jnp.sum(jnp.where(lane == e, gate_scr[...], 0.0), -1, keepdims=True)
    t = (_silu(_dot(h, wg_ref[...])) * _dot(h, wu_ref[...])).astype(BF16)
    acc_scr[...] += gate * _dot(t, wd_ref[...])

    @pl.when(e == pl.num_programs(1) - 1)
    def _():
        y = acc_scr[...]
        o_ref[...] = _rms(y, gf_ref[...]) if final_norm else y


def moe(x, g, w_router_pad, wg, wu, wd, g_final, final_norm, layer):
    m = x.shape[0]
    tm = _row_tile(m)
    return pl.pallas_call(
        functools.partial(_moe_kernel, final_norm=final_norm),
        grid=(m // tm, N_EXP),
        in_specs=[pl.BlockSpec((tm, D_MODEL), lambda i, e: (i, 0)),
                  pl.BlockSpec((1, D_MODEL), lambda i, e: (0, 0)),
                  pl.BlockSpec((D_MODEL, LANES), lambda i, e: (0, 0)),
                  pl.BlockSpec((None, None, D_MODEL, D_FF_E), lambda i, e: (layer, e, 0, 0)),
                  pl.BlockSpec((None, None, D_MODEL, D_FF_E), lambda i, e: (layer, e, 0, 0)),
                  pl.BlockSpec((None, None, D_FF_E, D_MODEL), lambda i, e: (layer, e, 0, 0)),
                  pl.BlockSpec((1, D_MODEL), lambda i, e: (0, 0))],
        out_specs=pl.BlockSpec((tm, D_MODEL), lambda i, e: (i, 0)),
        out_shape=jax.ShapeDtypeStruct((m, D_MODEL), F32),
        scratch_shapes=[pltpu.VMEM((tm, D_MODEL), BF16), pltpu.VMEM((tm, D_MODEL), F32),
                        pltpu.VMEM((tm, LANES), F32)],
        compiler_params=_cp(("parallel", "arbitrary")),
        name="moe",
    )(x, g.reshape(1, D_MODEL), w_router_pad, wg, wu, wd, g_final.reshape(1, D_MODEL))


def _moe_routed_kernel(x_ref, g_ref, wr_ref, wg_ref, wu_ref, wd_ref, gf_ref, o_ref,
                       h_scr, acc_scr, rank_scr, gate_t_scr, rank_t_scr, cnt_scr,
                       *, final_norm, rb):
    e = pl.program_id(1)
    tm = x_ref.shape[0]
    sub = 256

    @pl.when(e == 0)
    def _():
        x = x_ref[...]
        h = _rms(x, g_ref[...])
        h_scr[...] = h.astype(BF16)
        acc_scr[...] = x
        logits = jnp.dot(h, wr_ref[...], precision=HI, preferred_element_type=F32)
        lane = lax.broadcasted_iota(jnp.int32, logits.shape, 1)
        gates = _top2_gates(jnp.where(lane < N_EXP, logits, -jnp.inf))
        routed = gates > 0.0
        hit = jnp.where(routed, 1.0, 0.0).astype(BF16)
        col = lax.broadcasted_iota(jnp.int32, (sub, tm), 1)
        for r0 in range(0, tm, sub):
            row = r0 + lax.broadcasted_iota(jnp.int32, (sub, tm), 0)
            before = jnp.where(col < row, 1.0, 0.0).astype(BF16)
            rank = _dot(before, hit)
            rank_scr[r0:r0 + sub] = jnp.where(routed[r0:r0 + sub], rank, -1.0)
        gate_t_scr[...] = gates.T
        rank_t_scr[...] = rank_scr[...].T
        cnt_scr[...] = jnp.broadcast_to(jnp.sum(hit.astype(F32), 0, keepdims=True), cnt_scr.shape)

    lane = lax.broadcasted_iota(jnp.int32, (tm, LANES), 1)
    rank_col = jnp.sum(jnp.where(lane == e, rank_scr[...], 0.0), -1, keepdims=True)
    rank_row = rank_t_scr[pl.ds(e, 1), :]
    gate_row = gate_t_scr[pl.ds(e, 1), :]
    lane1 = lax.broadcasted_iota(jnp.int32, (1, LANES), 1)
    count = jnp.sum(jnp.where(lane1 == e, cnt_scr[0:1], 0.0)).astype(jnp.int32)

    def block(first_rank, rows):
        base = first_rank.astype(F32)
        pick = rank_row == base + lax.broadcasted_iota(jnp.int32, (rows, tm), 0).astype(F32)
        hg = _dot(jnp.where(pick, 1.0, 0.0).astype(BF16), h_scr[...]).astype(BF16)
        gate = jnp.sum(jnp.where(pick, gate_row, 0.0), -1, keepdims=True)
        t = (_silu(_dot(hg, wg_ref[...])) * _dot(hg, wu_ref[...])).astype(BF16)
        y = (gate * _dot(t, wd_ref[...])).astype(BF16)
        put = rank_col == base + lax.broadcasted_iota(jnp.int32, (tm, rows), 1).astype(F32)
        acc_scr[...] += _dot(jnp.where(put, 1.0, 0.0).astype(BF16), y)

    def full_block(j, carry):
        block(j * rb, rb)
        return carry

    lax.fori_loop(0, (count + rb - 1) // rb, full_block, 0)

    @pl.when(e == pl.num_programs(1) - 1)
    def _():
        y = acc_scr[...]
        o_ref[...] = _rms(y, gf_ref[...]) if final_norm else y


def moe_routed(x, g, w_router_pad, wg, wu, wd, g_final, final_norm, layer, tm=MOE_TILE,
               rb=MOE_ROWS):
    m = x.shape[0]
    return pl.pallas_call(
        functools.partial(_moe_routed_kernel, final_norm=final_norm, rb=rb),
        grid=(m // tm, N_EXP),
        in_specs=[pl.BlockSpec((tm, D_MODEL), lambda i, e: (i, 0)),
                  pl.BlockSpec((1, D_MODEL), lambda i, e: (0, 0)),
                  pl.BlockSpec((D_MODEL, LANES), lambda i, e: (0, 0)),
                  pl.BlockSpec((None, None, D_MODEL, D_FF_E), lambda i, e: (layer, e, 0, 0)),
                  pl.BlockSpec((None, None, D_MODEL, D_FF_E), lambda i, e: (layer, e, 0, 0)),
                  pl.BlockSpec((None, None, D_FF_E, D_MODEL), lambda i, e: (layer, e, 0, 0)),
                  pl.BlockSpec((1, D_MODEL), lambda i, e: (0, 0))],
        out_specs=pl.BlockSpec((tm, D_MODEL), lambda i, e: (i, 0)),
        out_shape=jax.ShapeDtypeStruct((m, D_MODEL), F32),
        scratch_shapes=[pltpu.VMEM((tm, D_MODEL), BF16), pltpu.VMEM((tm, D_MODEL), F32),
                        pltpu.VMEM((tm, LANES), F32),
                        pltpu.VMEM((LANES, tm), F32), pltpu.VMEM((LANES, tm), F32),
                        pltpu.VMEM((8, LANES), F32)],
        compiler_params=_cp(("parallel", "arbitrary")),
        name="moe_routed",
    )(x, g.reshape(1, D_MODEL), w_router_pad, wg, wu, wd, g_final.reshape(1, D_MODEL))


def _latent_kv_kernel(x_ref, g_ref, w_ref, gc_ref, cos_ref, sin_ref, ckv_ref, kpe_ref, kc_ref,
                      *maybe_kt_ref):
    h = _rms(x_ref[0], g_ref[...]).astype(BF16)
    kv = _dot(h, w_ref[...])
    ckv = _rms(kv[:, :KV_LORA], gc_ref[...])
    pe = kv[:, KV_LORA:KV_LORA + LANES] * cos_ref[...] + kv[:, KV_LORA + LANES:] * sin_ref[...]
    ckv_ref[0] = ckv
    kpe_ref[0] = pe[:, :QK_ROPE]
    kc = jnp.concatenate([ckv, pe], axis=-1)
    kc_ref[0] = kc.astype(BF16)
    for kt_ref in maybe_kt_ref:
        kt_ref[0] = kc.T.astype(BF16)


def latent_kv(x, g_kv, w_kv_bf, g_ckv, cos_t, sin_t, with_transposed):
    b, s, _ = x.shape
    tm = _row_tile(s)
    blk = lambda n: pl.BlockSpec((1, tm, n), lambda bi, i: (bi, i, 0))
    tab = pl.BlockSpec((tm, LANES), lambda bi, i: (i, 0))
    const = lambda a: pl.BlockSpec(a.shape, lambda bi, i: (0, 0))
    g_kv = g_kv.reshape(1, D_MODEL)
    g_ckv = g_ckv.reshape(1, KV_LORA)
    out_specs = [blk(KV_LORA), blk(QK_ROPE), blk(QK_CAT)]
    out_shape = [jax.ShapeDtypeStruct((b, s, KV_LORA), F32),
                 jax.ShapeDtypeStruct((b, s, QK_ROPE), F32),
                 jax.ShapeDtypeStruct((b, s, QK_CAT), BF16)]
    if with_transposed:
        out_specs.append(pl.BlockSpec((1, QK_CAT, tm), lambda bi, i: (bi, 0, i)))
        out_shape.append(jax.ShapeDtypeStruct((b, QK_CAT, s), BF16))
    return pl.pallas_call(
        _latent_kv_kernel,
        grid=(b, s // tm),
        in_specs=[blk(D_MODEL), const(g_kv), const(w_kv_bf), const(g_ckv), tab, tab],
        out_specs=out_specs,
        out_shape=out_shape,
        compiler_params=_cp(("parallel", "parallel")),
        name="latent_kv",
    )(x, g_kv, w_kv_bf, g_ckv, cos_t, sin_t)


def _mla_q_kernel(x_ref, g_ref, win_ref, gq_ref, wqb_ref, wuk_ref, cos_ref, sin_ref,
                  q_ref, mq_ref):
    h = _rms(x_ref[0], g_ref[...]).astype(BF16)
    proj = _dot(h, win_ref[...])
    mq_ref[0] = proj[:, Q_LORA:].astype(mq_ref.dtype)
    ql = _rms(proj[:, :Q_LORA], gq_ref[...]).astype(BF16)
    nh = MLA_H * QK_NOPE
    for hd in range(MLA_H):
        sl = slice(hd * LANES, (hd + 1) * LANES)
        nope = _dot(ql, wqb_ref[:, sl])
        pa = _dot(ql, wqb_ref[:, nh + hd * LANES:nh + (hd + 1) * LANES])
        pb = _dot(ql, wqb_ref[:, 2 * nh + hd * LANES:2 * nh + (hd + 1) * LANES])
        q_lat = _dot(nope.astype(BF16), wuk_ref[hd])
        q_pe = pa * cos_ref[...] + pb * sin_ref[...]
        q_ref[0, hd] = (jnp.concatenate([q_lat, q_pe], axis=-1)
                        * (MLA_SCALE * math.log2(math.e))).astype(q_ref.dtype)


def mla_q(x, g_mix, w_in_bf, g_ql, w_qb_bf, w_ukt_bf, cos_t, sin_t, layer):
    b, s, _ = x.shape
    tm = _row_tile(s)
    const2 = lambda a: pl.BlockSpec(a.shape, lambda bi, i: (0, 0))
    stacked = lambda a: pl.BlockSpec((None,) + a.shape[1:], lambda bi, i: (layer, 0, 0))
    tab = pl.BlockSpec((tm, LANES), lambda bi, i: (i, 0))
    g_mix = g_mix.reshape(1, D_MODEL)
    g_ql = g_ql.reshape(1, Q_LORA)
    return pl.pallas_call(
        _mla_q_kernel,
        grid=(b, s // tm),
        in_specs=[pl.BlockSpec((1, tm, D_MODEL), lambda bi, i: (bi, i, 0)),
                  const2(g_mix), stacked(w_in_bf), const2(g_ql), stacked(w_qb_bf),
                  pl.BlockSpec(w_ukt_bf.shape, lambda bi, i: (0, 0, 0)), tab, tab],
        out_specs=[pl.BlockSpec((1, MLA_H, tm, QK_CAT), lambda bi, i: (bi, 0, i, 0)),
                   pl.BlockSpec((1, tm, MEM_DIM), lambda bi, i: (bi, i, 0))],
        out_shape=[jax.ShapeDtypeStruct((b, MLA_H, s, QK_CAT), BF16),
                   jax.ShapeDtypeStruct((b, s, MEM_DIM), BF16)],
        compiler_params=_cp(("parallel", "parallel")),
        name="mla_q",
    )(x, g_mix, w_in_bf, g_ql, w_qb_bf, w_ukt_bf, cos_t, sin_t)


def _mla_prompt_kernel(qi_ref, ki_ref, q_ref, kt_ref, v_ref, wuv_ref, o_ref, m_scr, l_scr, acc_scr,
                       *, tq, tk):
    t = pl.program_id(1)
    qi = qi_ref[t]
    ki = ki_ref[t]
    reps = tk // LANES

    def wide(a, n):
        return jnp.concatenate([a] * n, axis=-1)

    @pl.when(ki == 0)
    def _():
        m_scr[...] = jnp.full_like(m_scr, -jnp.inf)
        l_scr[...] = jnp.zeros_like(l_scr)
        acc_scr[...] = jnp.zeros_like(acc_scr)

    def step(masked):
        kt = kt_ref[0]
        v = v_ref[0]
        rows = MLA_H * tq
        s = _dot(q_ref[0].reshape(rows, QK_CAT), kt)
        if masked:
            keep = (lax.broadcasted_iota(jnp.int32, (rows, tk), 1)
                    <= (lax.broadcasted_iota(jnp.int32, (rows, tk), 0) & (tq - 1)))
            s = jnp.where(keep, s, -jnp.inf)
        m_prev = m_scr[...]
        m_new = jnp.maximum(m_prev, jnp.max(s, -1, keepdims=True))
        alpha = jnp.exp2(m_prev - m_new)
        p = jnp.exp2(s - wide(m_new, reps))
        l_scr[...] = alpha * l_scr[...] + jnp.sum(p, -1, keepdims=True)
        acc_scr[...] = wide(alpha, KV_LORA // LANES) * acc_scr[...] + _dot(p.astype(BF16), v)
        m_scr[...] = m_new

    pl.when(ki < qi)(lambda: step(False))

    @pl.when(ki == qi)
    def _():
        step(True)
        for hd in range(MLA_H):
            rs = slice(hd * tq, (hd + 1) * tq)
            o_lat = (acc_scr[rs] / wide(l_scr[rs], KV_LORA // LANES)).astype(BF16)
            o_ref[0, :, hd * V_HEAD:(hd + 1) * V_HEAD] = _dot(o_lat, wuv_ref[hd]).astype(o_ref.dtype)


def mla_prompt_attend(q, kc, kc_t, w_uv_bf, tile=256):
    b, _, s, _ = q.shape
    tq = tk = tile
    n = s // tile
    pairs = [(qi, ki) for qi in range(n) for ki in range(qi + 1)]
    qi_tab = jnp.asarray([p[0] for p in pairs], jnp.int32)
    ki_tab = jnp.asarray([p[1] for p in pairs], jnp.int32)
    grid_spec = pltpu.PrefetchScalarGridSpec(
        num_scalar_prefetch=2,
        grid=(b, len(pairs)),
        in_specs=[pl.BlockSpec((1, MLA_H, tq, QK_CAT), lambda bi, t, qt, kt: (bi, 0, qt[t], 0)),
                  pl.BlockSpec((1, QK_CAT, tk), lambda bi, t, qt, kt: (bi, 0, kt[t])),
                  pl.BlockSpec((1, tk, KV_LORA), lambda bi, t, qt, kt: (bi, kt[t], 0)),
                  pl.BlockSpec(w_uv_bf.shape, lambda bi, t, qt, kt: (0, 0, 0))],
        out_specs=pl.BlockSpec((1, tq, MLA_H * V_HEAD), lambda bi, t, qt, kt: (bi, qt[t], 0)),
        scratch_shapes=[pltpu.VMEM((MLA_H * tq, LANES), F32), pltpu.VMEM((MLA_H * tq, LANES), F32),
                        pltpu.VMEM((MLA_H * tq, KV_LORA), F32)],
    )
    return pl.pallas_call(
        functools.partial(_mla_prompt_kernel, tq=tq, tk=tk),
        grid_spec=grid_spec,
        out_shape=jax.ShapeDtypeStruct((b, s, MLA_H * V_HEAD), BF16),
        compiler_params=_cp(("parallel", "arbitrary")),
        name="mla_prompt_attend",
    )(qi_tab, ki_tab, q, kc_t, kc, w_uv_bf)


def _mla_step_kernel(pt_ref, q_ref, knew_ref, wuv_ref, ckv_hbm, kpe_hbm, o_ref,
                     kbuf, pbuf, sem, m_scr, l_scr, acc_scr, *, pps):
    b = pl.program_id(0)
    j = pl.program_id(1)
    nj = pl.num_programs(1)
    t = b * nj + j
    slot = t % 2

    def page_copies(bi, ji, sl):
        cps = []
        for p in range(pps):
            pid = pt_ref[bi, ji * pps + p]
            cps.append(pltpu.make_async_copy(ckv_hbm.at[pid], kbuf.at[sl, p], sem.at[sl, 0]))
            cps.append(pltpu.make_async_copy(kpe_hbm.at[pid], pbuf.at[sl, p], sem.at[sl, 1]))
        return cps

    @pl.when(t == 0)
    def _():
        for cp in page_copies(b, j, slot):
            cp.start()

    @pl.when(t + 1 < pl.num_programs(0) * nj)
    def _():
        last = j == nj - 1
        for cp in page_copies(jnp.where(last, b + 1, b), jnp.where(last, 0, j + 1), 1 - slot):
            cp.start()

    for cp in page_copies(b, j, slot):
        cp.wait()

    q = q_ref[0]
    q_lat = q[:, :KV_LORA]
    q_pe = q[:, KV_LORA:KV_LORA + QK_ROPE]

    @pl.when(j == 0)
    def _():
        s0 = _dot_t(q, knew_ref[0])[:, 0:1]
        m_scr[...] = s0
        l_scr[...] = jnp.ones_like(l_scr)
        acc_scr[...] = jnp.broadcast_to(knew_ref[0][0:1, :KV_LORA].astype(F32), acc_scr.shape)

    ks = [kbuf[slot, p].astype(BF16) for p in range(pps)]
    ss = [_dot_t(q_lat, kb) + _dot(q_pe, pbuf[slot, p].astype(BF16)) for p, kb in enumerate(ks)]
    m_new = m_scr[...]
    for s in ss:
        m_new = jnp.maximum(m_new, jnp.max(s, -1, keepdims=True))
    alpha = jnp.exp2(m_scr[...] - m_new)
    l_new = alpha * l_scr[...]
    acc = alpha * acc_scr[...]
    for s, kb in zip(ss, ks):
        p = jnp.exp2(s - m_new)
        l_new = l_new + jnp.sum(p, -1, keepdims=True)
        acc = acc + _dot(p.astype(BF16), kb)
    m_scr[...] = m_new
    l_scr[...] = l_new
    acc_scr[...] = acc

    @pl.when(j == pl.num_programs(1) - 1)
    def _():
        o_lat = (acc_scr[...] / l_scr[...]).astype(BF16)
        full = _dot(o_lat, wuv_ref[...])
        rowi = lax.broadcasted_iota(jnp.int32, full.shape, 0)
        grp = lax.broadcasted_iota(jnp.int32, full.shape, 1) // V_HEAD
        o_ref[0] = jnp.sum(jnp.where(rowi == grp, full, 0.0), 0, keepdims=True).astype(o_ref.dtype)


def mla_step_attend(q8, k_new, w_uv_flat_bf, cache_ckv, cache_kpe_t, page_table, pps=16):
    nb, npg = page_table.shape
    steps = npg // pps
    grid_spec = pltpu.PrefetchScalarGridSpec(
        num_scalar_prefetch=1,
        grid=(nb, steps),
        in_specs=[pl.BlockSpec((1, 8, QK_CAT), lambda bi, j, pt: (bi, 0, 0)),
                  pl.BlockSpec((1, 8, QK_CAT), lambda bi, j, pt: (bi, 0, 0)),
                  pl.BlockSpec(w_uv_flat_bf.shape, lambda bi, j, pt: (0, 0)),
                  pl.BlockSpec(memory_space=pl.ANY), pl.BlockSpec(memory_space=pl.ANY)],
        out_specs=pl.BlockSpec((1, 1, MLA_H * V_HEAD), lambda bi, j, pt: (bi, 0, 0)),
        scratch_shapes=[pltpu.VMEM((2, pps, PAGE, KV_LORA), F32),
                        pltpu.VMEM((2, pps, QK_ROPE, PAGE), F32),
                        pltpu.SemaphoreType.DMA((2, 2)),
                        pltpu.VMEM((8, 1), F32), pltpu.VMEM((8, 1), F32),
                        pltpu.VMEM((8, KV_LORA), F32)],
    )
    return pl.pallas_call(
        functools.partial(_mla_step_kernel, pps=pps),
        grid_spec=grid_spec,
        out_shape=jax.ShapeDtypeStruct((nb, 1, MLA_H * V_HEAD), F32),
        compiler_params=_cp(("arbitrary", "arbitrary")),
        name="mla_step_attend",
    )(page_table, q8, k_new, w_uv_flat_bf, cache_ckv, cache_kpe_t)


def _swap_halves(w):
    half = w.shape[-1] // 2
    return jnp.concatenate([w[..., half:], w[..., :half]], -1)


def _pad_lanes(w, n=LANES):
    return jnp.pad(w, [(0, 0)] * (w.ndim - 1) + [(0, n - w.shape[-1])])


def _prep_weights(P):
    W = {}
    o1 = CONV_DIM
    o2 = o1 + DN_QK
    o3 = o2 + 2 * DN_H
    wa = P['w_in_a']
    W['in_a_f32'] = jnp.concatenate([wa[..., :o2], _pad_lanes(wa[..., o2:o3]), wa[..., o3:]], -1)
    W['in_a'] = W['in_a_f32'].astype(BF16)
    gate = jnp.zeros((N_A, 2, LANES), F32)
    gate = gate.at[:, 0, DN_H:2 * DN_H].set(P['a_log']).at[:, 1, DN_H:2 * DN_H].set(P['dt_bias'])
    W['gate_rows'] = gate
    W['in_b'] = P['w_in_b'].astype(BF16)
    wqb = P['w_q_b'].reshape(-1, Q_LORA, MLA_H, QK_NOPE + QK_ROPE)
    nope = wqb[..., :QK_NOPE].reshape(-1, Q_LORA, MLA_H * QK_NOPE)
    pe = wqb[..., QK_NOPE:]
    pa = _pad_lanes(pe).reshape(-1, Q_LORA, MLA_H * LANES)
    pb = _pad_lanes(_swap_halves(pe)).reshape(-1, Q_LORA, MLA_H * LANES)
    W['q_b'] = jnp.concatenate([nope, pa, pb], -1).astype(BF16)
    wkv = P['w_kv_a']
    kpe_w = wkv[:, KV_LORA:]
    W['kv_a'] = jnp.concatenate(
        [wkv[:, :KV_LORA], _pad_lanes(kpe_w), _pad_lanes(_swap_halves(kpe_w))], -1).astype(BF16)
    W['uk_t'] = jnp.transpose(P['w_uk'], (1, 2, 0)).astype(BF16)
    W['uv'] = jnp.transpose(P['w_uv'], (1, 0, 2)).astype(BF16)
    W['uv_flat'] = P['w_uv'].reshape(KV_LORA, MLA_H * V_HEAD).astype(BF16)
    W['mem_kv'] = P['w_mem_kv'].astype(BF16)
    W['out'] = P['w_out'].astype(BF16)
    W['gate'] = P['w_gate'].astype(BF16)
    W['up'] = P['w_up'].astype(BF16)
    W['down'] = P['w_down'].astype(BF16)
    W['router'] = _pad_lanes(P['w_router'])
    W['e_gate'] = P['we_gate'].astype(BF16)
    W['e_up'] = P['we_up'].astype(BF16)
    W['e_down'] = P['we_down'].astype(BF16)
    return W


def _rope_tables(pos):
    half = QK_ROPE // 2
    inv = ROPE_THETA ** (-jnp.arange(half, dtype=F32) / half)
    ang = pos.astype(F32)[:, None] * inv[None, :]
    cos = jnp.cos(ang)
    sin = jnp.sin(ang)
    zero = jnp.zeros((pos.shape[0], LANES - QK_ROPE), F32)
    return (jnp.concatenate([cos, cos, zero], -1), jnp.concatenate([-sin, sin, zero], -1))


def _channel_mixer(x2, l, P, W, precise=False):
    i = l // 2
    if l % 2 == 0:
        if precise:
            return ffn(x2, P['g_ffn'][l], P['w_gate'], P['w_up'], P['w_down'], i)
        return ffn(x2, P['g_ffn'][l], W['gate'], W['up'], W['down'], i)
    fn = moe_routed if x2.shape[0] % MOE_TILE == 0 else moe
    return fn(x2, P['g_ffn'][l], W['router'][i], W['e_gate'], W['e_up'], W['e_down'],
              P['g_final'], final_norm=(l == DEPTH - 1), layer=i)


def _prompt_trunk(x, mem_k, mem_v, P, W):
    b, s, _ = x.shape
    m = b * s
    cos_t, sin_t = _rope_tables(jnp.arange(s))
    conv_states, dn_states = [], []
    kc = kc_t = ckv = kpe = None
    for l in range(DEPTH):
        if l < N_A:
            qkv, z, ba, mq = norm_proj(x.reshape(m, D_MODEL), P['g_mix'][l], W['in_a'], l,
                                       (CONV_DIM, DN_QK, LANES, MEM_DIM), (BF16, BF16, F32, BF16),
                                       "in_proj_a")
            qkv = qkv.reshape(b, s, CONV_DIM)
            o_tok, s_new = gdn_prompt(qkv, z.reshape(b, s, DN_QK), ba.reshape(b, s, LANES),
                                      P['conv_w'][l], W['gate_rows'][l], P['g_onorm'][l])
            conv_states.append(qkv[:, s - (CONV_W - 1):, :].astype(F32))
            dn_states.append(s_new)
            mq = mq.reshape(b, s, MEM_DIM)
        else:
            j = l - N_A
            if l == N_A:
                ckv, kpe, kc, kc_t = latent_kv(x, P['g_kv'], W['kv_a'], P['g_ckv'], cos_t, sin_t,
                                               with_transposed=True)
            q, mq = mla_q(x, P['g_mix'][l], W['in_b'], P['g_qlora'][j], W['q_b'], W['uk_t'],
                          cos_t, sin_t, j)
            o_tok = mla_prompt_attend(q, kc, kc_t, W['uv'])
        x = mix_out(x, o_tok, mq, mem_k, mem_v, W['out'], l)
        x = _channel_mixer(x.reshape(m, D_MODEL), l, P, W).reshape(b, s, D_MODEL)
    return x, jnp.stack(conv_states), jnp.stack(dn_states), ckv, kpe


def _sample_trunk(x, mem_k_t, mem_v_t, conv_prev, dn_prev, cache_ckv, cache_kpe_t, page_table, P, W):
    nb = x.shape[0]
    past = page_table.shape[1] * PAGE
    cos_t, sin_t = _rope_tables(jnp.full((nb,), past, jnp.int32))
    x2 = x.reshape(nb, D_MODEL)
    conv_states, dn_states = [], []
    kc = ckv = kpe = None
    for l in range(DEPTH):
        if l < N_A:
            qkv, z, ba, mq = norm_proj(x2, P['g_mix'][l], W['in_a_f32'], l,
                                       (CONV_DIM, DN_QK, LANES, MEM_DIM), (F32, F32, F32, F32),
                                       "in_proj_a_step")
            o_tok, conv_new, s_new = gdn_step(qkv, z, ba, conv_prev, dn_prev, l, P['conv_w'][l],
                                              W['gate_rows'][l], P['g_onorm'][l])
            conv_states.append(conv_new)
            dn_states.append(s_new)
        else:
            j = l - N_A
            x3 = x2.reshape(1, nb, D_MODEL)
            if l == N_A:
                ckv, kpe, kc = latent_kv(x3, P['g_kv'], W['kv_a'], P['g_ckv'], cos_t, sin_t,
                                         with_transposed=False)
                ckv = ckv.reshape(nb, 1, KV_LORA)
                kpe = kpe.reshape(nb, 1, QK_ROPE)
                kc = jnp.broadcast_to(kc.reshape(nb, 1, QK_CAT), (nb, 8, QK_CAT))
            q, mq = mla_q(x3, P['g_mix'][l], W['in_b'], P['g_qlora'][j], W['q_b'], W['uk_t'],
                          cos_t, sin_t, j)
            q8 = jnp.pad(jnp.transpose(q[0], (1, 0, 2)), ((0, 0), (0, 8 - MLA_H), (0, 0)))
            o_tok = mla_step_attend(q8, kc, W['uv_flat'], cache_ckv, cache_kpe_t, page_table)
            mq = mq.reshape(nb, MEM_DIM).astype(F32)
        precise = l < N_A
        o_mem = mem_attend_step(mq, mem_k_t, mem_v_t, l, precise)
        x2 = out_proj(x2, o_tok.reshape(nb, -1), o_mem.reshape(nb, MEM_DIM),
                      P['w_out'] if precise else W['out'], l)
        x2 = _channel_mixer(x2, l, P, W, precise)
    return (x2.reshape(nb, 1, D_MODEL), jnp.stack(conv_states), jnp.stack(dn_states), ckv, kpe)


def kernel(x_prompt, x_sample, cache_mem_k, cache_mem_v, cache_ckv, cache_kpe, state_delta, state_conv,
           page_table, mem_prompt, g_mix, g_ffn, g_final, w_in_a, conv_w, a_log, dt_bias, g_onorm,
           w_in_b, g_qlora, w_q_b, g_kv, w_kv_a, g_ckv, w_uk, w_uv, g_mem, w_mem_kv, w_out,
           w_gate, w_up, w_down, w_router, we_gate, we_up, we_down):
    P = dict(g_mix=g_mix, g_ffn=g_ffn, g_final=g_final, w_in_a=w_in_a, conv_w=conv_w, a_log=a_log,
             dt_bias=dt_bias, g_onorm=g_onorm, w_in_b=w_in_b, g_qlora=g_qlora, w_q_b=w_q_b, g_kv=g_kv,
             w_kv_a=w_kv_a, g_ckv=g_ckv, w_uk=w_uk, w_uv=w_uv, w_mem_kv=w_mem_kv, w_out=w_out,
             w_gate=w_gate, w_up=w_up, w_down=w_down, w_router=w_router, we_gate=we_gate,
             we_up=we_up, we_down=we_down)
    W = _prep_weights(P)
    bp = x_prompt.shape[0]
    nl = g_mem.shape[0]
    mk, mv = mem_kv(mem_prompt.reshape(bp * N_MEM, D_MODEL), g_mem, W['mem_kv'])
    mk = mk.reshape(nl, bp, N_MEM, MEM_DIM)
    mv = mv.reshape(nl, bp, N_MEM, MEM_DIM)
    y_p, p_conv, p_delta, p_ckv, p_kpe = _prompt_trunk(x_prompt, mk, mv, P, W)
    nb = x_sample.shape[0]
    to_feat_tok = lambda c: jnp.transpose(c, (0, 1, 3, 4, 2)).reshape(nl, nb, MEM_DIM, N_MEM)
    y_s, s_conv, s_delta, s_ckv, s_kpe = _sample_trunk(
        x_sample, to_feat_tok(cache_mem_k), to_feat_tok(cache_mem_v), state_conv, state_delta,
        cache_ckv, jnp.swapaxes(cache_kpe, 1, 2), page_table, P, W)
    p_mem_k = mk.reshape(nl, bp, N_MEM, MEM_H, MEM_HD)
    p_mem_v = mv.reshape(nl, bp, N_MEM, MEM_H, MEM_HD)
    return (y_p, y_s, p_delta, p_conv, p_ckv, p_kpe, p_mem_k, p_mem_v, s_delta, s_conv, s_ckv, s_kpe)
```

```python
import functools
import math

import jax
import jax.numpy as jnp
from jax import lax
from jax.experimental import pallas as pl
from jax.experimental.pallas import tpu as pltpu

F32 = jnp.float32
BF16 = jnp.bfloat16

D_MODEL = 1024
DEPTH = 4
N_A = DEPTH // 2
PAGE = 128
DN_H = 6
DN_D = 128
DN_QK = DN_H * DN_D
CONV_W = 4
CONV_DIM = 3 * DN_QK
DN_CHUNK = 256
MLA_H = 6
Q_LORA = 384
KV_LORA = 256
QK_NOPE = 128
QK_ROPE = 64
V_HEAD = 128
ROPE_THETA = 10000.0
MLA_SCALE = (QK_NOPE + QK_ROPE) ** -0.5
QK_CAT = KV_LORA + 128
N_MEM = 256
MEM_H = 4
MEM_HD = 64
MEM_DIM = MEM_H * MEM_HD
D_FF = 2816
N_EXP = 8
D_FF_E = 1408
MOE_TILE = 1024
MOE_ROWS = 128
EPS = 1e-6

LANES = 128
VMEM_LIMIT = 56 * 1024 * 1024
HI = lax.Precision.HIGHEST


def _cp(sem, vmem=VMEM_LIMIT):
    return pltpu.CompilerParams(dimension_semantics=sem, vmem_limit_bytes=vmem)


def _row_tile(m, pref=512):
    return pref if m % pref == 0 else m


def _rms(x, g):
    return x * lax.rsqrt(jnp.mean(x * x, -1, keepdims=True) + EPS) * g


def _silu(x):
    return x * jax.nn.sigmoid(x)


def _softplus(x):
    return jnp.maximum(x, 0.0) + jnp.log1p(jnp.exp(-jnp.abs(x)))


def _dot(a, b):
    return jnp.dot(a, b, preferred_element_type=F32)


def _mm(a, w):
    if w.dtype == F32:
        return jnp.dot(a.astype(F32), w, precision=HI, preferred_element_type=F32)
    return jnp.dot(a.astype(BF16), w, preferred_element_type=F32)


def _dot_t(a, b):
    return lax.dot_general(a, b, (((1,), (1,)), ((), ())), preferred_element_type=F32)


def _softmax_rows(s):
    m = jnp.max(s, -1, keepdims=True)
    e = jnp.exp(s - m)
    return e / jnp.sum(e, -1, keepdims=True)


def _norm_proj_kernel(x_ref, g_ref, w_ref, *out_refs, splits):
    h = _rms(x_ref[...], g_ref[...]).astype(w_ref.dtype)
    off = 0
    for o_ref, n in zip(out_refs, splits):
        o_ref[...] = _mm(h, w_ref[:, off:off + n]).astype(o_ref.dtype)
        off += n


def norm_proj(x, g, w, layer, splits, dtypes, name):
    m, k = x.shape
    tm = _row_tile(m)
    n = w.shape[2]
    return pl.pallas_call(
        functools.partial(_norm_proj_kernel, splits=splits),
        grid=(m // tm,),
        in_specs=[pl.BlockSpec((tm, k), lambda i: (i, 0)),
                  pl.BlockSpec((1, k), lambda i: (0, 0)),
                  pl.BlockSpec((None, k, n), lambda i: (layer, 0, 0))],
        out_specs=[pl.BlockSpec((tm, s), lambda i: (i, 0)) for s in splits],
        out_shape=[jax.ShapeDtypeStruct((m, s), d) for s, d in zip(splits, dtypes)],
        compiler_params=_cp(("parallel",)),
        name=name,
    )(x, g.reshape(1, k), w)


def _mem_kv_kernel(m_ref, g_ref, w_ref, k_ref, v_ref):
    x = m_ref[...]
    mn = x * lax.rsqrt(jnp.mean(x * x, -1, keepdims=True) + EPS)
    kv = _dot((mn * g_ref[0]).astype(BF16), w_ref[0])
    k_ref[0] = kv[:, :MEM_DIM]
    v_ref[0] = kv[:, MEM_DIM:]


def mem_kv(mem, g_mem, w_mem_kv_bf):
    m = mem.shape[0]
    tm = _row_tile(m)
    nl = g_mem.shape[0]
    out = jax.ShapeDtypeStruct((nl, m, MEM_DIM), F32)
    return pl.pallas_call(
        _mem_kv_kernel,
        grid=(m // tm, nl),
        in_specs=[pl.BlockSpec((tm, D_MODEL), lambda i, l: (i, 0)),
                  pl.BlockSpec((1, 1, D_MODEL), lambda i, l: (l, 0, 0)),
                  pl.BlockSpec((1, D_MODEL, 2 * MEM_DIM), lambda i, l: (l, 0, 0))],
        out_specs=[pl.BlockSpec((1, tm, MEM_DIM), lambda i, l: (l, i, 0))] * 2,
        out_shape=[out, out],
        compiler_params=_cp(("parallel", "arbitrary")),
        name="mem_kv",
    )(mem, g_mem.reshape(nl, 1, D_MODEL), w_mem_kv_bf)


def _bmm(a, b):
    return jnp.einsum('hij,hjk->hik', a, b, preferred_element_type=F32)


def _bmm_t(a, b):
    return jnp.einsum('hid,hjd->hij', a, b, preferred_element_type=F32)


def _tri_inverse_minus_eye(lmat, row, col):
    def same_block(bits):
        return jnp.right_shift(row, bits) == jnp.right_shift(col, bits)

    l1 = jnp.where(same_block(4), lmat, 0.0)
    l1b = l1.astype(BF16)
    l2 = _bmm(l1b, l1b)
    l2b = l2.astype(BF16)
    l4 = _bmm(l2b, l2b)
    l4b = l4.astype(BF16)
    l8 = _bmm(l4b, l4b)
    q = -l1
    q = q + l2 + _bmm(q.astype(BF16), l2b)
    q = q + l4 + _bmm(q.astype(BF16), l4b)
    q = q + l8 + _bmm(q.astype(BF16), l8.astype(BF16))
    bits = 4
    while (1 << bits) < DN_CHUNK:
        cross = jnp.logical_and(same_block(bits + 1), jnp.logical_not(same_block(bits)))
        c = jnp.where(cross, lmat, 0.0)
        qb = q.astype(BF16)
        y = c + _bmm(qb, c.astype(BF16))
        q = q - (y + _bmm(y.astype(BF16), qb))
        bits += 1
    return q


def _gdn_prompt_kernel(u_ref, z_ref, ba_ref, w_ref, gate_ref, gon_ref, o_ref, s_out_ref,
                       halo_ref, s_scr):
    c = pl.program_id(1)
    C = DN_CHUNK

    @pl.when(c == 0)
    def _():
        halo_ref[...] = jnp.zeros_like(halo_ref)
        s_scr[...] = jnp.zeros_like(s_scr)

    u = u_ref[0].astype(F32)
    wc = w_ref[...]
    ext = jnp.concatenate([halo_ref[...], u], axis=0)
    y = u * wc[CONV_W - 1:CONV_W]
    for j in range(1, CONV_W):
        y = y + pltpu.roll(ext, j, axis=0)[8:] * wc[CONV_W - 1 - j:CONV_W - j]
    halo_ref[...] = u[C - 8:]
    y = _silu(y)

    ba = ba_ref[0]
    beta_all = jax.nn.sigmoid(ba)
    g_all = -jnp.exp(gate_ref[0:1]) * _softplus(ba + gate_ref[1:2])
    row = lax.broadcasted_iota(jnp.int32, (C, C), 0)
    col = lax.broadcasted_iota(jnp.int32, (C, C), 1)
    tri = row >= col
    gcum_all = jnp.dot(tri.astype(F32), g_all, precision=HI, preferred_element_type=F32)
    gcum_t = gcum_all.T

    heads = range(DN_H)
    per_head = lambda a, off: jnp.stack([a[:, off + h * DN_D:off + (h + 1) * DN_D] for h in heads])
    beta = jnp.stack([beta_all[:, h:h + 1] for h in heads])
    gc = jnp.stack([gcum_all[:, DN_H + h:DN_H + h + 1] for h in heads])
    gr = jnp.stack([gcum_t[DN_H + h:DN_H + h + 1, :] for h in heads])
    gl = gr[:, :, C - 1:C]
    qh = per_head(y, 0)
    kh = per_head(y, DN_QK)
    vh = per_head(y, 2 * DN_QK)
    qh = qh * lax.rsqrt(jnp.sum(qh * qh, -1, keepdims=True) + EPS) * (DN_D ** -0.5)
    kh = kh * lax.rsqrt(jnp.sum(kh * kh, -1, keepdims=True) + EPS)
    decay = jnp.exp(jnp.where(tri, gc - gr, -jnp.inf))
    kb = kh * beta
    vb = vh * beta
    k_bf = kh.astype(BF16)
    lmat = jnp.where(row > col, _bmm_t(kb.astype(BF16), k_bf) * decay, 0.0)
    qinv = _tri_inverse_minus_eye(lmat, row, col)
    egc = jnp.exp(gc)
    rhs = jnp.concatenate([vb, kb * egc], axis=-1)
    sol = rhs + _bmm(qinv.astype(BF16), rhs.astype(BF16))
    un = sol[:, :, :DN_D]
    wn = sol[:, :, DN_D:]
    qk = _bmm_t(qh.astype(BF16), k_bf) * decay
    qg = qh * egc
    kg = (kh * jnp.exp(gl - gc)).astype(BF16)
    st = s_scr[...]
    ws = _bmm(jnp.concatenate([wn, qg], axis=1).astype(BF16), st.astype(BF16))
    v_new = un - ws[:, :C]
    vn_bf = v_new.astype(BF16)
    o = ws[:, C:] + _bmm(qk.astype(BF16), vn_bf)
    st = st * jnp.exp(gl)
    on = _rms(o, gon_ref[...])
    for h in heads:
        s_scr[h] = st[h] + lax.dot_general(
            kg[h], vn_bf[h], (((0,), (0,)), ((), ())), preferred_element_type=F32)
        zh = z_ref[0, :, h * DN_D:(h + 1) * DN_D].astype(F32)
        o_ref[0, :, h * DN_D:(h + 1) * DN_D] = (on[h] * _silu(zh)).astype(o_ref.dtype)

    @pl.when(c == pl.num_programs(1) - 1)
    def _():
        s_out_ref[0] = s_scr[...]


def gdn_prompt(qkv, z, ba, conv_w, gate_rows, g_onorm):
    b, s, _ = qkv.shape
    C = DN_CHUNK
    blk = lambda n: pl.BlockSpec((1, C, n), lambda bi, c: (bi, c, 0))
    const = lambda a: pl.BlockSpec(a.shape, lambda bi, c: (0, 0))
    g_onorm = g_onorm.reshape(1, DN_D)
    return pl.pallas_call(
        _gdn_prompt_kernel,
        grid=(b, s // C),
        in_specs=[blk(CONV_DIM), blk(DN_QK), blk(LANES), const(conv_w), const(gate_rows),
                  const(g_onorm)],
        out_specs=[blk(DN_QK),
                   pl.BlockSpec((1, DN_H, DN_D, DN_D), lambda bi, c: (bi, 0, 0, 0))],
        out_shape=[jax.ShapeDtypeStruct((b, s, DN_QK), BF16),
                   jax.ShapeDtypeStruct((b, DN_H, DN_D, DN_D), F32)],
        scratch_shapes=[pltpu.VMEM((8, CONV_DIM), F32), pltpu.VMEM((DN_H, DN_D, DN_D), F32)],
        compiler_params=_cp(("parallel", "arbitrary")),
        name="gdn_prompt",
    )(qkv, z, ba, conv_w, gate_rows, g_onorm)


def _gdn_step_kernel(u_ref, z_ref, ba_ref, cs_ref, s_ref, w_ref, gate_ref, gon_ref,
                     o_ref, cs_out_ref, s_out_ref):
    u = u_ref[0].astype(F32)
    prev = cs_ref[0]
    w = w_ref[...]
    y = u * w[CONV_W - 1:CONV_W]
    for j in range(CONV_W - 1):
        y = y + prev[j:j + 1] * w[j:j + 1]
    y = _silu(y)
    cs_out_ref[0, 0:CONV_W - 2, :] = cs_ref[0, 1:CONV_W - 1, :]
    cs_out_ref[0, CONV_W - 2:CONV_W - 1, :] = u
    ba = ba_ref[0]
    beta_all = jax.nn.sigmoid(ba)
    g_all = -jnp.exp(gate_ref[0:1]) * _softplus(ba + gate_ref[1:2])
    eye = (lax.broadcasted_iota(jnp.int32, (DN_D, DN_D), 0)
           == lax.broadcasted_iota(jnp.int32, (DN_D, DN_D), 1))

    def to_col(r):
        return jnp.sum(jnp.where(eye, jnp.broadcast_to(r, (DN_D, DN_D)), 0.0), -1, keepdims=True)

    for h in range(DN_H):
        beta = beta_all[:, h:h + 1]
        eg = jnp.exp(g_all[:, DN_H + h:DN_H + h + 1])
        qh = y[:, h * DN_D:(h + 1) * DN_D]
        kh = y[:, DN_QK + h * DN_D:DN_QK + (h + 1) * DN_D]
        vh = y[:, 2 * DN_QK + h * DN_D:2 * DN_QK + (h + 1) * DN_D]
        qh = qh * lax.rsqrt(jnp.sum(qh * qh, -1, keepdims=True) + EPS) * (DN_D ** -0.5)
        kh = kh * lax.rsqrt(jnp.sum(kh * kh, -1, keepdims=True) + EPS)
        st = s_ref[0, h]
        kcol = to_col(kh)
        qcol = to_col(qh)
        ks = jnp.sum(kcol * st, 0, keepdims=True)
        qs = jnp.sum(qcol * st, 0, keepdims=True)
        v_new = beta * vh - (beta * eg) * ks
        o = eg * qs + jnp.sum(qh * kh, -1, keepdims=True) * v_new
        s_out_ref[0, h] = st * eg + kcol * v_new
        zh = z_ref[0, :, h * DN_D:(h + 1) * DN_D].astype(F32)
        o_ref[0, :, h * DN_D:(h + 1) * DN_D] = (_rms(o, gon_ref[...]) * _silu(zh)).astype(o_ref.dtype)


def gdn_step(qkv, z, ba, conv_state, s_state, layer, conv_w, gate_rows, g_onorm):
    nb = qkv.shape[0]
    i3 = lambda bi: (bi, 0, 0)
    return pl.pallas_call(
        _gdn_step_kernel,
        grid=(nb,),
        in_specs=[pl.BlockSpec((1, 1, CONV_DIM), i3), pl.BlockSpec((1, 1, DN_QK), i3),
                  pl.BlockSpec((1, 1, LANES), i3),
                  pl.BlockSpec((None, 1, CONV_W - 1, CONV_DIM), lambda bi: (layer, bi, 0, 0)),
                  pl.BlockSpec((None, 1, DN_H, DN_D, DN_D), lambda bi: (layer, bi, 0, 0, 0)),
                  pl.BlockSpec((CONV_W, CONV_DIM), lambda bi: (0, 0)),
                  pl.BlockSpec((2, LANES), lambda bi: (0, 0)),
                  pl.BlockSpec((1, DN_D), lambda bi: (0, 0))],
        out_specs=[pl.BlockSpec((1, 1, DN_QK), i3), pl.BlockSpec((1, CONV_W - 1, CONV_DIM), i3),
                   pl.BlockSpec((1, DN_H, DN_D, DN_D), lambda bi: (bi, 0, 0, 0))],
        out_shape=[jax.ShapeDtypeStruct((nb, 1, DN_QK), F32),
                   jax.ShapeDtypeStruct((nb, CONV_W - 1, CONV_DIM), F32),
                   jax.ShapeDtypeStruct((nb, DN_H, DN_D, DN_D), F32)],
        compiler_params=_cp(("parallel",)),
        name="gdn_step",
    )(qkv.reshape(nb, 1, CONV_DIM), z.reshape(nb, 1, DN_QK), ba.reshape(nb, 1, LANES),
      conv_state, s_state, conv_w, gate_rows, g_onorm.reshape(1, DN_D))


def _mem_attend_rows(mq, mk, mv):
    lane_head = lax.broadcasted_iota(jnp.int32, (1, MEM_DIM), 1) // MEM_HD
    out = jnp.zeros((mq.shape[0], MEM_DIM), F32)
    for h in range(MEM_H):
        sel = lane_head == h
        s = _dot_t(mq, jnp.where(sel, mk, 0.0).astype(BF16)) * (MEM_HD ** -0.5)
        p = _softmax_rows(s).astype(BF16)
        out = out + _dot(p, jnp.where(sel, mv, 0.0).astype(BF16))
    return out


def _mix_out_kernel(x_ref, ot_ref, mq_ref, mk_ref, mv_ref, w_ref, o_ref):
    om = _mem_attend_rows(mq_ref[0], mk_ref[0], mv_ref[0]).astype(BF16)
    nt = ot_ref.shape[-1]
    o_ref[0] = x_ref[0] + _dot(ot_ref[0], w_ref[:nt]) + _dot(om, w_ref[nt:])


def mix_out(x, o_tok, mq, mk, mv, w_out_bf, layer, tm=512):
    b, s, _ = x.shape
    nt = o_tok.shape[-1]
    blk = lambda n: pl.BlockSpec((1, tm, n), lambda bi, i: (bi, i, 0))
    kv = pl.BlockSpec((None, 1, N_MEM, MEM_DIM), lambda bi, i: (layer, bi, 0, 0))
    return pl.pallas_call(
        _mix_out_kernel,
        grid=(b, s // tm),
        in_specs=[blk(D_MODEL), blk(nt), blk(MEM_DIM), kv, kv,
                  pl.BlockSpec((None, nt + MEM_DIM, D_MODEL), lambda bi, i: (layer, 0, 0))],
        out_specs=blk(D_MODEL),
        out_shape=jax.ShapeDtypeStruct((b, s, D_MODEL), F32),
        compiler_params=_cp(("parallel", "parallel")),
        name="mix_out",
    )(x, o_tok, mq, mk, mv, w_out_bf)


def _mem_attend_step_kernel(mq_ref, mk_ref, mv_ref, o_ref, *, precise):
    lane_head = lax.broadcasted_iota(jnp.int32, (8, MEM_DIM), 1) // MEM_HD
    rowi = lax.broadcasted_iota(jnp.int32, (8, MEM_DIM), 0)
    sel = lane_head == rowi
    q8 = jnp.where(sel, jnp.broadcast_to(mq_ref[0], (8, MEM_DIM)), 0.0)
    kt = mk_ref[0]
    vt = mv_ref[0]
    if precise:
        s = _mm(q8, kt) * (MEM_HD ** -0.5)
        o8 = lax.dot_general(_softmax_rows(s), vt, (((1,), (1,)), ((), ())), precision=HI,
                             preferred_element_type=F32)
    else:
        s = _dot(q8.astype(BF16), kt.astype(BF16)) * (MEM_HD ** -0.5)
        o8 = _dot_t(_softmax_rows(s).astype(BF16), vt.astype(BF16))
    o_ref[0] = jnp.sum(jnp.where(sel, o8, 0.0), 0, keepdims=True).astype(o_ref.dtype)


def mem_attend_step(mq, mk_t, mv_t, layer, precise):
    nb = mq.shape[0]
    i3 = lambda bi: (bi, 0, 0)
    kv = pl.BlockSpec((None, 1, MEM_DIM, N_MEM), lambda bi: (layer, bi, 0, 0))
    return pl.pallas_call(
        functools.partial(_mem_attend_step_kernel, precise=precise),
        grid=(nb,),
        in_specs=[pl.BlockSpec((1, 1, MEM_DIM), i3), kv, kv],
        out_specs=pl.BlockSpec((1, 1, MEM_DIM), i3),
        out_shape=jax.ShapeDtypeStruct((nb, 1, MEM_DIM), F32),
        compiler_params=_cp(("parallel",)),
        name="mem_attend_step",
    )(mq.reshape(nb, 1, MEM_DIM), mk_t, mv_t)


def _out_proj_kernel(x_ref, ot_ref, om_ref, w_ref, o_ref):
    nt = ot_ref.shape[-1]
    o_ref[...] = x_ref[...] + _mm(ot_ref[...], w_ref[:nt]) + _mm(om_ref[...], w_ref[nt:])


def out_proj(x, o_tok, o_mem, w_out, layer):
    m = x.shape[0]
    nt = o_tok.shape[-1]
    full = lambda a: pl.BlockSpec(a.shape, lambda i: (0, 0))
    return pl.pallas_call(
        _out_proj_kernel,
        grid=(1,),
        in_specs=[full(x), full(o_tok), full(o_mem),
                  pl.BlockSpec((None,) + w_out.shape[1:], lambda i: (layer, 0, 0))],
        out_specs=pl.BlockSpec((m, D_MODEL), lambda i: (0, 0)),
        out_shape=jax.ShapeDtypeStruct((m, D_MODEL), F32),
        compiler_params=_cp(("arbitrary",)),
        name="out_proj",
    )(x, o_tok, o_mem, w_out)


def _ffn_kernel(x_ref, g_ref, wg_ref, wu_ref, wd_ref, o_ref, h_scr, acc_scr):
    f = pl.program_id(1)

    @pl.when(f == 0)
    def _():
        x = x_ref[...]
        h_scr[...] = _rms(x, g_ref[...]).astype(h_scr.dtype)
        acc_scr[...] = x

    h = h_scr[...]
    t = _silu(_mm(h, wg_ref[...])) * _mm(h, wu_ref[...])
    acc_scr[...] += _mm(t, wd_ref[...])

    @pl.when(f == pl.num_programs(1) - 1)
    def _():
        o_ref[...] = acc_scr[...]


def ffn(x, g, wg, wu, wd, layer, tf=1408):
    m = x.shape[0]
    tm = _row_tile(m)
    return pl.pallas_call(
        _ffn_kernel,
        grid=(m // tm, D_FF // tf),
        in_specs=[pl.BlockSpec((tm, D_MODEL), lambda i, f: (i, 0)),
                  pl.BlockSpec((1, D_MODEL), lambda i, f: (0, 0)),
                  pl.BlockSpec((None, D_MODEL, tf), lambda i, f: (layer, 0, f)),
                  pl.BlockSpec((None, D_MODEL, tf), lambda i, f: (layer, 0, f)),
                  pl.BlockSpec((None, tf, D_MODEL), lambda i, f: (layer, f, 0))],
        out_specs=pl.BlockSpec((tm, D_MODEL), lambda i, f: (i, 0)),
        out_shape=jax.ShapeDtypeStruct((m, D_MODEL), F32),
        scratch_shapes=[pltpu.VMEM((tm, D_MODEL), wg.dtype), pltpu.VMEM((tm, D_MODEL), F32)],
        compiler_params=_cp(("parallel", "arbitrary")),
        name="ffn",
    )(x, g.reshape(1, D_MODEL), wg, wu, wd)


def _top2_gates(logits):
    lane = lax.broadcasted_iota(jnp.int32, logits.shape, 1)
    probs = _softmax_rows(logits)
    p1 = jnp.max(probs, -1, keepdims=True)
    i1 = jnp.min(jnp.where(probs == p1, lane, LANES), -1, keepdims=True)
    m1 = lane == i1
    rest = jnp.where(m1, -1.0, probs)
    p2 = jnp.max(rest, -1, keepdims=True)
    i2 = jnp.min(jnp.where(rest == p2, lane, LANES), -1, keepdims=True)
    m2 = lane == i2
    tot = p1 + p2
    return jnp.where(m1, p1 / tot, 0.0) + jnp.where(m2, p2 / tot, 0.0)


def _moe_kernel(x_ref, g_ref, wr_ref, wg_ref, wu_ref, wd_ref, gf_ref, o_ref,
                h_scr, acc_scr, gate_scr, *, final_norm):
    e = pl.program_id(1)

    @pl.when(e == 0)
    def _():
        x = x_ref[...]
        h = _rms(x, g_ref[...])
        h_scr[...] = h.astype(BF16)
        acc_scr[...] = x
        logits = jnp.dot(h, wr_ref[...], precision=HI, preferred_element_type=F32)
        lane = lax.broadcasted_iota(jnp.int32, logits.shape, 1)
        gate_scr[...] = _top2_gates(jnp.where(lane < N_EXP, logits, -jnp.inf))

    h = h_scr[...]
    lane = lax.broadcasted_iota(jnp.int32, gate_scr.shape, 1)
    gate = jnp.sum(jnp.where(lane == e, gate_scr[...], 0.0), -1, keepdims=True)
    t = (_silu(_dot(h, wg_ref[...])) * _dot(h, wu_ref[...])).astype(BF16)
    acc_scr[...] += gate * _dot(t, wd_ref[...])

    @pl.when(e == pl.num_programs(1) - 1)
    def _():
        y = acc_scr[...]
        o_ref[...] = _rms(y, gf_ref[...]) if final_norm else y


def moe(x, g, w_router_pad, wg, wu, wd, g_final, final_norm, layer):
    m = x.shape[0]
    tm = _row_tile(m)
    return pl.pallas_call(
        functools.partial(_moe_kernel, final_norm=final_norm),
        grid=(m // tm, N_EXP),
        in_specs=[pl.BlockSpec((tm, D_MODEL), lambda i, e: (i, 0)),
                  pl.BlockSpec((1, D_MODEL), lambda i, e: (0, 0)),
                  pl.BlockSpec((D_MODEL, LANES), lambda i, e: (0, 0)),
                  pl.BlockSpec((None, None, D_MODEL, D_FF_E), lambda i, e: (layer, e, 0, 0)),
                  pl.BlockSpec((None, None, D_MODEL, D_FF_E), lambda i, e: (layer, e, 0, 0)),
                  pl.BlockSpec((None, None, D_FF_E, D_MODEL), lambda i, e: (layer, e, 0, 0)),
                  pl.BlockSpec((1, D_MODEL), lambda i, e: (0, 0))],
        out_specs=pl.BlockSpec((tm, D_MODEL), lambda i, e: (i, 0)),
        out_shape=jax.ShapeDtypeStruct((m, D_MODEL), F32),
        scratch_shapes=[pltpu.VMEM((tm, D_MODEL), BF16), pltpu.VMEM((tm, D_MODEL), F32),
                        pltpu.VMEM((tm, LANES), F32)],
        compiler_params=_cp(("parallel", "arbitrary")),
        name="moe",
    )(x, g.reshape(1, D_MODEL), w_router_pad, wg, wu, wd, g_final.reshape(1, D_MODEL))


def _moe_routed_kernel(x_ref, g_ref, wr_ref, wg_ref, wu_ref, wd_ref, gf_ref, o_ref,
                       h_scr, acc_scr, rank_scr, gate_t_scr, rank_t_scr, cnt_scr,
                       *, final_norm, rb):
    e = pl.program_id(1)
    tm = x_ref.shape[0]
    sub = 256

    @pl.when(e == 0)
    def _():
        x = x_ref[...]
        h = _rms(x, g_ref[...])
        h_scr[...] = h.astype(BF16)
        acc_scr[...] = x
        logits = jnp.dot(h, wr_ref[...], precision=HI, preferred_element_type=F32)
        lane = lax.broadcasted_iota(jnp.int32, logits.shape, 1)
        gates = _top2_gates(jnp.where(lane < N_EXP, logits, -jnp.inf))
        routed = gates > 0.0
        hit = jnp.where(routed, 1.0, 0.0).astype(BF16)
        col = lax.broadcasted_iota(jnp.int32, (sub, tm), 1)
        for r0 in range(0, tm, sub):
            row = r0 + lax.broadcasted_iota(jnp.int32, (sub, tm), 0)
            before = jnp.where(col < row, 1.0, 0.0).astype(BF16)
            rank = _dot(before, hit)
            rank_scr[r0:r0 + sub] = jnp.where(routed[r0:r0 + sub], rank, -1.0)
        gate_t_scr[...] = gates.T
        rank_t_scr[...] = rank_scr[...].T
        cnt_scr[...] = jnp.broadcast_to(jnp.sum(hit.astype(F32), 0, keepdims=True), cnt_scr.shape)

    lane = lax.broadcasted_iota(jnp.int32, (tm, LANES), 1)
    rank_col = jnp.sum(jnp.where(lane == e, rank_scr[...], 0.0), -1, keepdims=True)
    rank_row = rank_t_scr[pl.ds(e, 1), :]
    gate_row = gate_t_scr[pl.ds(e, 1), :]
    lane1 = lax.broadcasted_iota(jnp.int32, (1, LANES), 1)
    count = jnp.sum(jnp.where(lane1 == e, cnt_scr[0:1], 0.0)).astype(jnp.int32)

    def block(first_rank, rows):
        base = first_rank.astype(F32)
        pick = rank_row == base + lax.broadcasted_iota(jnp.int32, (rows, tm), 0).astype(F32)
        hg = _dot(jnp.where(pick, 1.0, 0.0).astype(BF16), h_scr[...]).astype(BF16)
        gate = jnp.sum(jnp.where(pick, gate_row, 0.0), -1, keepdims=True)
        t = (_silu(_dot(hg, wg_ref[...])) * _dot(hg, wu_ref[...])).astype(BF16)
        y = (gate * _dot(t, wd_ref[...])).astype(BF16)
        put = rank_col == base + lax.broadcasted_iota(jnp.int32, (tm, rows), 1).astype(F32)
        acc_scr[...] += _dot(jnp.where(put, 1.0, 0.0).astype(BF16), y)

    def full_block(j, carry):
        block(j * rb, rb)
        return carry

    lax.fori_loop(0, (count + rb - 1) // rb, full_block, 0)

    @pl.when(e == pl.num_programs(1) - 1)
    def _():
        y = acc_scr[...]
        o_ref[...] = _rms(y, gf_ref[...]) if final_norm else y


def moe_routed(x, g, w_router_pad, wg, wu, wd, g_final, final_norm, layer, tm=MOE_TILE,
               rb=MOE_ROWS):
    m = x.shape[0]
    return pl.pallas_call(
        functools.partial(_moe_routed_kernel, final_norm=final_norm, rb=rb),
        grid=(m // tm, N_EXP),
        in_specs=[pl.BlockSpec((tm, D_MODEL), lambda i, e: (i, 0)),
                  pl.BlockSpec((1, D_MODEL), lambda i, e: (0, 0)),
                  pl.BlockSpec((D_MODEL, LANES), lambda i, e: (0, 0)),
                  pl.BlockSpec((None, None, D_MODEL, D_FF_E), lambda i, e: (layer, e, 0, 0)),
                  pl.BlockSpec((None, None, D_MODEL, D_FF_E), lambda i, e: (layer, e, 0, 0)),
                  pl.BlockSpec((None, None, D_FF_E, D_MODEL), lambda i, e: (layer, e, 0, 0)),
                  pl.BlockSpec((1, D_MODEL), lambda i, e: (0, 0))],
        out_specs=pl.BlockSpec((tm, D_MODEL), lambda i, e: (i, 0)),
        out_shape=jax.ShapeDtypeStruct((m, D_MODEL), F32),
        scratch_shapes=[pltpu.VMEM((tm, D_MODEL), BF16), pltpu.VMEM((tm, D_MODEL), F32),
                        pltpu.VMEM((tm, LANES), F32),
                        pltpu.VMEM((LANES, tm), F32), pltpu.VMEM((LANES, tm), F32),
                        pltpu.VMEM((8, LANES), F32)],
        compiler_params=_cp(("parallel", "arbitrary")),
        name="moe_routed",
    )(x, g.reshape(1, D_MODEL), w_router_pad, wg, wu, wd, g_final.reshape(1, D_MODEL))


def _latent_kv_kernel(x_ref, g_ref, w_ref, gc_ref, cos_ref, sin_ref, ckv_ref, kpe_ref, kc_ref,
                      *maybe_kt_ref):
    h = _rms(x_ref[0], g_ref[...]).astype(BF16)
    kv = _dot(h, w_ref[...])
    ckv = _rms(kv[:, :KV_LORA], gc_ref[...])
    pe = kv[:, KV_LORA:KV_LORA + LANES] * cos_ref[...] + kv[:, KV_LORA + LANES:] * sin_ref[...]
    ckv_ref[0] = ckv
    kpe_ref[0] = pe[:, :QK_ROPE]
    kc = jnp.concatenate([ckv, pe], axis=-1)
    kc_ref[0] = kc.astype(BF16)
    for kt_ref in maybe_kt_ref:
        kt_ref[0] = kc.T.astype(BF16)


def latent_kv(x, g_kv, w_kv_bf, g_ckv, cos_t, sin_t, with_transposed):
    b, s, _ = x.shape
    tm = _row_tile(s)
    blk = lambda n: pl.BlockSpec((1, tm, n), lambda bi, i: (bi, i, 0))
    tab = pl.BlockSpec((tm, LANES), lambda bi, i: (i, 0))
    const = lambda a: pl.BlockSpec(a.shape, lambda bi, i: (0, 0))
    g_kv = g_kv.reshape(1, D_MODEL)
    g_ckv = g_ckv.reshape(1, KV_LORA)
    out_specs = [blk(KV_LORA), blk(QK_ROPE), blk(QK_CAT)]
    out_shape = [jax.ShapeDtypeStruct((b, s, KV_LORA), F32),
                 jax.ShapeDtypeStruct((b, s, QK_ROPE), F32),
                 jax.ShapeDtypeStruct((b, s, QK_CAT), BF16)]
    if with_transposed:
        out_specs.append(pl.BlockSpec((1, QK_CAT, tm), lambda bi, i: (bi, 0, i)))
        out_shape.append(jax.ShapeDtypeStruct((b, QK_CAT, s), BF16))
    return pl.pallas_call(
        _latent_kv_kernel,
        grid=(b, s // tm),
        in_specs=[blk(D_MODEL), const(g_kv), const(w_kv_bf), const(g_ckv), tab, tab],
        out_specs=out_specs,
        out_shape=out_shape,
        compiler_params=_cp(("parallel", "parallel")),
        name="latent_kv",
    )(x, g_kv, w_kv_bf, g_ckv, cos_t, sin_t)


def _mla_q_kernel(x_ref, g_ref, win_ref, gq_ref, wqb_ref, wuk_ref, cos_ref, sin_ref,
                  q_ref, mq_ref):
    h = _rms(x_ref[0], g_ref[...]).astype(BF16)
    proj = _dot(h, win_ref[...])
    mq_ref[0] = proj[:, Q_LORA:].astype(mq_ref.dtype)
    ql = _rms(proj[:, :Q_LORA], gq_ref[...]).astype(BF16)
    nh = MLA_H * QK_NOPE
    for hd in range(MLA_H):
        sl = slice(hd * LANES, (hd + 1) * LANES)
        nope = _dot(ql, wqb_ref[:, sl])
        pa = _dot(ql, wqb_ref[:, nh + hd * LANES:nh + (hd + 1) * LANES])
        pb = _dot(ql, wqb_ref[:, 2 * nh + hd * LANES:2 * nh + (hd + 1) * LANES])
        q_lat = _dot(nope.astype(BF16), wuk_ref[hd])
        q_pe = pa * cos_ref[...] + pb * sin_ref[...]
        q_ref[0, hd] = (jnp.concatenate([q_lat, q_pe], axis=-1)
                        * (MLA_SCALE * math.log2(math.e))).astype(q_ref.dtype)


def mla_q(x, g_mix, w_in_bf, g_ql, w_qb_bf, w_ukt_bf, cos_t, sin_t, layer):
    b, s, _ = x.shape
    tm = _row_tile(s)
    const2 = lambda a: pl.BlockSpec(a.shape, lambda bi, i: (0, 0))
    stacked = lambda a: pl.BlockSpec((None,) + a.shape[1:], lambda bi, i: (layer, 0, 0))
    tab = pl.BlockSpec((tm, LANES), lambda bi, i: (i, 0))
    g_mix = g_mix.reshape(1, D_MODEL)
    g_ql = g_ql.reshape(1, Q_LORA)
    return pl.pallas_call(
        _mla_q_kernel,
        grid=(b, s // tm),
        in_specs=[pl.BlockSpec((1, tm, D_MODEL), lambda bi, i: (bi, i, 0)),
                  const2(g_mix), stacked(w_in_bf), const2(g_ql), stacked(w_qb_bf),
                  pl.BlockSpec(w_ukt_bf.shape, lambda bi, i: (0, 0, 0)), tab, tab],
        out_specs=[pl.BlockSpec((1, MLA_H, tm, QK_CAT), lambda bi, i: (bi, 0, i, 0)),
                   pl.BlockSpec((1, tm, MEM_DIM), lambda bi, i: (bi, i, 0))],
        out_shape=[jax.ShapeDtypeStruct((b, MLA_H, s, QK_CAT), BF16),
                   jax.ShapeDtypeStruct((b, s, MEM_DIM), BF16)],
        compiler_params=_cp(("parallel", "parallel")),
        name="mla_q",
    )(x, g_mix, w_in_bf, g_ql, w_qb_bf, w_ukt_bf, cos_t, sin_t)


def _mla_prompt_kernel(qi_ref, ki_ref, q_ref, kt_ref, v_ref, wuv_ref, o_ref, m_scr, l_scr, acc_scr,
                       *, tq, tk):
    t = pl.program_id(1)
    qi = qi_ref[t]
    ki = ki_ref[t]
    reps = tk // LANES

    def wide(a, n):
        return jnp.concatenate([a] * n, axis=-1)

    @pl.when(ki == 0)
    def _():
        m_scr[...] = jnp.full_like(m_scr, -jnp.inf)
        l_scr[...] = jnp.zeros_like(l_scr)
        acc_scr[...] = jnp.zeros_like(acc_scr)

    def step(masked):
        kt = kt_ref[0]
        v = v_ref[0]
        rows = MLA_H * tq
        s = _dot(q_ref[0].reshape(rows, QK_CAT), kt)
        if masked:
            keep = (lax.broadcasted_iota(jnp.int32, (rows, tk), 1)
                    <= (lax.broadcasted_iota(jnp.int32, (rows, tk), 0) & (tq - 1)))
            s = jnp.where(keep, s, -jnp.inf)
        m_prev = m_scr[...]
        m_new = jnp.maximum(m_prev, jnp.max(s, -1, keepdims=True))
        alpha = jnp.exp2(m_prev - m_new)
        p = jnp.exp2(s - wide(m_new, reps))
        l_scr[...] = alpha * l_scr[...] + jnp.sum(p, -1, keepdims=True)
        acc_scr[...] = wide(alpha, KV_LORA // LANES) * acc_scr[...] + _dot(p.astype(BF16), v)
        m_scr[...] = m_new

    pl.when(ki < qi)(lambda: step(False))

    @pl.when(ki == qi)
    def _():
        step(True)
        for hd in range(MLA_H):
            rs = slice(hd * tq, (hd + 1) * tq)
            o_lat = (acc_scr[rs] / wide(l_scr[rs], KV_LORA // LANES)).astype(BF16)
            o_ref[0, :, hd * V_HEAD:(hd + 1) * V_HEAD] = _dot(o_lat, wuv_ref[hd]).astype(o_ref.dtype)


def mla_prompt_attend(q, kc, kc_t, w_uv_bf, tile=512):
    b, _, s, _ = q.shape
    tq = tk = tile
    n = s // tile
    pairs = [(qi, ki) for qi in range(n) for ki in range(qi + 1)]
    qi_tab = jnp.asarray([p[0] for p in pairs], jnp.int32)
    ki_tab = jnp.asarray([p[1] for p in pairs], jnp.int32)
    grid_spec = pltpu.PrefetchScalarGridSpec(
        num_scalar_prefetch=2,
        grid=(b, len(pairs)),
        in_specs=[pl.BlockSpec((1, MLA_H, tq, QK_CAT), lambda bi, t, qt, kt: (bi, 0, qt[t], 0)),
                  pl.BlockSpec((1, QK_CAT, tk), lambda bi, t, qt, kt: (bi, 0, kt[t])),
                  pl.BlockSpec((1, tk, KV_LORA), lambda bi, t, qt, kt: (bi, kt[t], 0)),
                  pl.BlockSpec(w_uv_bf.shape, lambda bi, t, qt, kt: (0, 0, 0))],
        out_specs=pl.BlockSpec((1, tq, MLA_H * V_HEAD), lambda bi, t, qt, kt: (bi, qt[t], 0)),
        scratch_shapes=[pltpu.VMEM((MLA_H * tq, LANES), F32), pltpu.VMEM((MLA_H * tq, LANES), F32),
                        pltpu.VMEM((MLA_H * tq, KV_LORA), F32)],
    )
    return pl.pallas_call(
        functools.partial(_mla_prompt_kernel, tq=tq, tk=tk),
        grid_spec=grid_spec,
        out_shape=jax.ShapeDtypeStruct((b, s, MLA_H * V_HEAD), BF16),
        compiler_params=_cp(("parallel", "arbitrary")),
        name="mla_prompt_attend",
    )(qi_tab, ki_tab, q, kc_t, kc, w_uv_bf)


def _mla_step_kernel(pt_ref, q_ref, knew_ref, wuv_ref, ckv_hbm, kpe_hbm, o_ref,
                     kbuf, pbuf, sem, m_scr, l_scr, acc_scr, *, pps):
    b = pl.program_id(0)
    j = pl.program_id(1)
    nj = pl.num_programs(1)
    t = b * nj + j
    slot = t % 2

    def page_copies(bi, ji, sl):
        cps = []
        for p in range(pps):
            pid = pt_ref[bi, ji * pps + p]
            cps.append(pltpu.make_async_copy(ckv_hbm.at[pid], kbuf.at[sl, p], sem.at[sl, 0]))
            cps.append(pltpu.make_async_copy(kpe_hbm.at[pid], pbuf.at[sl, p], sem.at[sl, 1]))
        return cps

    @pl.when(t == 0)
    def _():
        for cp in page_copies(b, j, slot):
            cp.start()

    @pl.when(t + 1 < pl.num_programs(0) * nj)
    def _():
        last = j == nj - 1
        for cp in page_copies(jnp.where(last, b + 1, b), jnp.where(last, 0, j + 1), 1 - slot):
            cp.start()

    for cp in page_copies(b, j, slot):
        cp.wait()

    q = q_ref[0]
    q_lat = q[:, :KV_LORA]
    q_pe = q[:, KV_LORA:KV_LORA + QK_ROPE]

    @pl.when(j == 0)
    def _():
        s0 = _dot_t(q, knew_ref[0])[:, 0:1]
        m_scr[...] = s0
        l_scr[...] = jnp.ones_like(l_scr)
        acc_scr[...] = jnp.broadcast_to(knew_ref[0][0:1, :KV_LORA].astype(F32), acc_scr.shape)

    ks = [kbuf[slot, p].astype(BF16) for p in range(pps)]
    ss = [_dot_t(q_lat, kb) + _dot(q_pe, pbuf[slot, p].astype(BF16)) for p, kb in enumerate(ks)]
    m_new = m_scr[...]
    for s in ss:
        m_new = jnp.maximum(m_new, jnp.max(s, -1, keepdims=True))
    alpha = jnp.exp2(m_scr[...] - m_new)
    l_new = alpha * l_scr[...]
    acc = alpha * acc_scr[...]
    for s, kb in zip(ss, ks):
        p = jnp.exp2(s - m_new)
        l_new = l_new + jnp.sum(p, -1, keepdims=True)
        acc = acc + _dot(p.astype(BF16), kb)
    m_scr[...] = m_new
    l_scr[...] = l_new
    acc_scr[...] = acc

    @pl.when(j == pl.num_programs(1) - 1)
    def _():
        o_lat = (acc_scr[...] / l_scr[...]).astype(BF16)
        full = _dot(o_lat, wuv_ref[...])
        rowi = lax.broadcasted_iota(jnp.int32, full.shape, 0)
        grp = lax.broadcasted_iota(jnp.int32, full.shape, 1) // V_HEAD
        o_ref[0] = jnp.sum(jnp.where(rowi == grp, full, 0.0), 0, keepdims=True).astype(o_ref.dtype)


def mla_step_attend(q8, k_new, w_uv_flat_bf, cache_ckv, cache_kpe_t, page_table, pps=64):
    nb, npg = page_table.shape
    steps = npg // pps
    grid_spec = pltpu.PrefetchScalarGridSpec(
        num_scalar_prefetch=1,
        grid=(nb, steps),
        in_specs=[pl.BlockSpec((1, 8, QK_CAT), lambda bi, j, pt: (bi, 0, 0)),
                  pl.BlockSpec((1, 8, QK_CAT), lambda bi, j, pt: (bi, 0, 0)),
                  pl.BlockSpec(w_uv_flat_bf.shape, lambda bi, j, pt: (0, 0)),
                  pl.BlockSpec(memory_space=pl.ANY), pl.BlockSpec(memory_space=pl.ANY)],
        out_specs=pl.BlockSpec((1, 1, MLA_H * V_HEAD), lambda bi, j, pt: (bi, 0, 0)),
        scratch_shapes=[pltpu.VMEM((2, pps, PAGE, KV_LORA), F32),
                        pltpu.VMEM((2, pps, QK_ROPE, PAGE), F32),
                        pltpu.SemaphoreType.DMA((2, 2)),
                        pltpu.VMEM((8, 1), F32), pltpu.VMEM((8, 1), F32),
                        pltpu.VMEM((8, KV_LORA), F32)],
    )
    return pl.pallas_call(
        functools.partial(_mla_step_kernel, pps=pps),
        grid_spec=grid_spec,
        out_shape=jax.ShapeDtypeStruct((nb, 1, MLA_H * V_HEAD), F32),
        compiler_params=_cp(("arbitrary", "arbitrary")),
        name="mla_step_attend",
    )(page_table, q8, k_new, w_uv_flat_bf, cache_ckv, cache_kpe_t)


def _swap_halves(w):
    half = w.shape[-1] // 2
    return jnp.concatenate([w[..., half:], w[..., :half]], -1)


def _pad_lanes(w, n=LANES):
    return jnp.pad(w, [(0, 0)] * (w.ndim - 1) + [(0, n - w.shape[-1])])


def _prep_weights(P):
    W = {}
    o1 = CONV_DIM
    o2 = o1 + DN_QK
    o3 = o2 + 2 * DN_H
    wa = P['w_in_a']
    W['in_a_f32'] = jnp.concatenate([wa[..., :o2], _pad_lanes(wa[..., o2:o3]), wa[..., o3:]], -1)
    W['in_a'] = W['in_a_f32'].astype(BF16)
    gate = jnp.zeros((N_A, 2, LANES), F32)
    gate = gate.at[:, 0, DN_H:2 * DN_H].set(P['a_log']).at[:, 1, DN_H:2 * DN_H].set(P['dt_bias'])
    W['gate_rows'] = gate
    W['in_b'] = P['w_in_b'].astype(BF16)
    wqb = P['w_q_b'].reshape(-1, Q_LORA, MLA_H, QK_NOPE + QK_ROPE)
    nope = wqb[..., :QK_NOPE].reshape(-1, Q_LORA, MLA_H * QK_NOPE)
    pe = wqb[..., QK_NOPE:]
    pa = _pad_lanes(pe).reshape(-1, Q_LORA, MLA_H * LANES)
    pb = _pad_lanes(_swap_halves(pe)).reshape(-1, Q_LORA, MLA_H * LANES)
    W['q_b'] = jnp.concatenate([nope, pa, pb], -1).astype(BF16)
    wkv = P['w_kv_a']
    kpe_w = wkv[:, KV_LORA:]
    W['kv_a'] = jnp.concatenate(
        [wkv[:, :KV_LORA], _pad_lanes(kpe_w), _pad_lanes(_swap_halves(kpe_w))], -1).astype(BF16)
    W['uk_t'] = jnp.transpose(P['w_uk'], (1, 2, 0)).astype(BF16)
    W['uv'] = jnp.transpose(P['w_uv'], (1, 0, 2)).astype(BF16)
    W['uv_flat'] = P['w_uv'].reshape(KV_LORA, MLA_H * V_HEAD).astype(BF16)
    W['mem_kv'] = P['w_mem_kv'].astype(BF16)
    W['out'] = P['w_out'].astype(BF16)
    W['gate'] = P['w_gate'].astype(BF16)
    W['up'] = P['w_up'].astype(BF16)
    W['down'] = P['w_down'].astype(BF16)
    W['router'] = _pad_lanes(P['w_router'])
    W['e_gate'] = P['we_gate'].astype(BF16)
    W['e_up'] = P['we_up'].astype(BF16)
    W['e_down'] = P['we_down'].astype(BF16)
    return W


def _rope_tables(pos):
    half = QK_ROPE // 2
    inv = ROPE_THETA ** (-jnp.arange(half, dtype=F32) / half)
    ang = pos.astype(F32)[:, None] * inv[None, :]
    cos = jnp.cos(ang)
    sin = jnp.sin(ang)
    zero = jnp.zeros((pos.shape[0], LANES - QK_ROPE), F32)
    return (jnp.concatenate([cos, cos, zero], -1), jnp.concatenate([-sin, sin, zero], -1))


def _channel_mixer(x2, l, P, W, precise=False):
    i = l // 2
    if l % 2 == 0:
        if precise:
            return ffn(x2, P['g_ffn'][l], P['w_gate'], P['w_up'], P['w_down'], i)
        return ffn(x2, P['g_ffn'][l], W['gate'], W['up'], W['down'], i)
    fn = moe_routed if x2.shape[0] % MOE_TILE == 0 else moe
    return fn(x2, P['g_ffn'][l], W['router'][i], W['e_gate'], W['e_up'], W['e_down'],
              P['g_final'], final_norm=(l == DEPTH - 1), layer=i)


def _prompt_trunk(x, mem_k, mem_v, P, W):
    b, s, _ = x.shape
    m = b * s
    cos_t, sin_t = _rope_tables(jnp.arange(s))
    conv_states, dn_states = [], []
    kc = kc_t = ckv = kpe = None
    for l in range(DEPTH):
        if l < N_A:
            qkv, z, ba, mq = norm_proj(x.reshape(m, D_MODEL), P['g_mix'][l], W['in_a'], l,
                                       (CONV_DIM, DN_QK, LANES, MEM_DIM), (BF16, BF16, F32, BF16),
                                       "in_proj_a")
            qkv = qkv.reshape(b, s, CONV_DIM)
            o_tok, s_new = gdn_prompt(qkv, z.reshape(b, s, DN_QK), ba.reshape(b, s, LANES),
                                      P['conv_w'][l], W['gate_rows'][l], P['g_onorm'][l])
            conv_states.append(qkv[:, s - (CONV_W - 1):, :].astype(F32))
            dn_states.append(s_new)
            mq = mq.reshape(b, s, MEM_DIM)
        else:
            j = l - N_A
            if l == N_A:
                ckv, kpe, kc, kc_t = latent_kv(x, P['g_kv'], W['kv_a'], P['g_ckv'], cos_t, sin_t,
                                               with_transposed=True)
            q, mq = mla_q(x, P['g_mix'][l], W['in_b'], P['g_qlora'][j], W['q_b'], W['uk_t'],
                          cos_t, sin_t, j)
            o_tok = mla_prompt_attend(q, kc, kc_t, W['uv'])
        x = mix_out(x, o_tok, mq, mem_k, mem_v, W['out'], l)
        x = _channel_mixer(x.reshape(m, D_MODEL), l, P, W).reshape(b, s, D_MODEL)
    return x, jnp.stack(conv_states), jnp.stack(dn_states), ckv, kpe


def _sample_trunk(x, mem_k_t, mem_v_t, conv_prev, dn_prev, cache_ckv, cache_kpe_t, page_table, P, W):
    nb = x.shape[0]
    past = page_table.shape[1] * PAGE
    cos_t, sin_t = _rope_tables(jnp.full((nb,), past, jnp.int32))
    x2 = x.reshape(nb, D_MODEL)
    conv_states, dn_states = [], []
    kc = ckv = kpe = None
    for l in range(DEPTH):
        if l < N_A:
            qkv, z, ba, mq = norm_proj(x2, P['g_mix'][l], W['in_a_f32'], l,
                                       (CONV_DIM, DN_QK, LANES, MEM_DIM), (F32, F32, F32, F32),
                                       "in_proj_a_step")
            o_tok, conv_new, s_new = gdn_step(qkv, z, ba, conv_prev, dn_prev, l, P['conv_w'][l],
                                              W['gate_rows'][l], P['g_onorm'][l])
            conv_states.append(conv_new)
            dn_states.append(s_new)
        else:
            j = l - N_A
            x3 = x2.reshape(1, nb, D_MODEL)
            if l == N_A:
                ckv, kpe, kc = latent_kv(x3, P['g_kv'], W['kv_a'], P['g_ckv'], cos_t, sin_t,
                                         with_transposed=False)
                ckv = ckv.reshape(nb, 1, KV_LORA)
                kpe = kpe.reshape(nb, 1, QK_ROPE)
                kc = jnp.broadcast_to(kc.reshape(nb, 1, QK_CAT), (nb, 8, QK_CAT))
            q, mq = mla_q(x3, P['g_mix'][l], W['in_b'], P['g_qlora'][j], W['q_b'], W['uk_t'],
                          cos_t, sin_t, j)
            q8 = jnp.pad(jnp.transpose(q[0], (1, 0, 2)), ((0, 0), (0, 8 - MLA_H), (0, 0)))
            o_tok = mla_step_attend(q8, kc, W['uv_flat'], cache_ckv, cache_kpe_t, page_table)
            mq = mq.reshape(nb, MEM_DIM).astype(F32)
        precise = l < N_A
        o_mem = mem_attend_step(mq, mem_k_t, mem_v_t, l, precise)
        x2 = out_proj(x2, o_tok.reshape(nb, -1), o_mem.reshape(nb, MEM_DIM),
                      P['w_out'] if precise else W['out'], l)
        x2 = _channel_mixer(x2, l, P, W, precise)
    return (x2.reshape(nb, 1, D_MODEL), jnp.stack(conv_states), jnp.stack(dn_states), ckv, kpe)


def kernel(x_prompt, x_sample, cache_mem_k, cache_mem_v, cache_ckv, cache_kpe, state_delta, state_conv,
           page_table, mem_prompt, g_mix, g_ffn, g_final, w_in_a, conv_w, a_log, dt_bias, g_onorm,
           w_in_b, g_qlora, w_q_b, g_kv, w_kv_a, g_ckv, w_uk, w_uv, g_mem, w_mem_kv, w_out,
           w_gate, w_up, w_down, w_router, we_gate, we_up, we_down):
    P = dict(g_mix=g_mix, g_ffn=g_ffn, g_final=g_final, w_in_a=w_in_a, conv_w=conv_w, a_log=a_log,
             dt_bias=dt_bias, g_onorm=g_onorm, w_in_b=w_in_b, g_qlora=g_qlora, w_q_b=w_q_b, g_kv=g_kv,
             w_kv_a=w_kv_a, g_ckv=g_ckv, w_uk=w_uk, w_uv=w_uv, w_mem_kv=w_mem_kv, w_out=w_out,
             w_gate=w_gate, w_up=w_up, w_down=w_down, w_router=w_router, we_gate=we_gate,
             we_up=we_up, we_down=we_down)
    W = _prep_weights(P)
    bp = x_prompt.shape[0]
    nl = g_mem.shape[0]
    mk, mv = mem_kv(mem_prompt.reshape(bp * N_MEM, D_MODEL), g_mem, W['mem_kv'])
    mk = mk.reshape(nl, bp, N_MEM, MEM_DIM)
    mv = mv.reshape(nl, bp, N_MEM, MEM_DIM)
    y_p, p_conv, p_delta, p_ckv, p_kpe = _prompt_trunk(x_prompt, mk, mv, P, W)
    nb = x_sample.shape[0]
    to_feat_tok = lambda c: jnp.transpose(c, (0, 1, 3, 4, 2)).reshape(nl, nb, MEM_DIM, N_MEM)
    y_s, s_conv, s_delta, s_ckv, s_kpe = _sample_trunk(
        x_sample, to_feat_tok(cache_mem_k), to_feat_tok(cache_mem_v), state_conv, state_delta,
        cache_ckv, jnp.swapaxes(cache_kpe, 1, 2), page_table, P, W)
    p_mem_k = mk.reshape(nl, bp, N_MEM, MEM_H, MEM_HD)
    p_mem_v = mv.reshape(nl, bp, N_MEM, MEM_H, MEM_HD)
    return (y_p, y_s, p_delta, p_conv, p_ckv, p_kpe, p_mem_k, p_mem_v, s_delta, s_conv, s_ckv, s_kpe)
```

```python
import functools
import math

import jax
import jax.numpy as jnp
from jax import lax
from jax.experimental import pallas as pl
from jax.experimental.pallas import tpu as pltpu

F32 = jnp.float32
BF16 = jnp.bfloat16

D_MODEL = 1024
DEPTH = 4
N_A = DEPTH // 2
PAGE = 128
DN_H = 6
DN_D = 128
DN_QK = DN_H * DN_D
CONV_W = 4
CONV_DIM = 3 * DN_QK
DN_CHUNK = 128
MLA_H = 6
Q_LORA = 384
KV_LORA = 256
QK_NOPE = 128
QK_ROPE = 64
V_HEAD = 128
ROPE_THETA = 10000.0
MLA_SCALE = (QK_NOPE + QK_ROPE) ** -0.5
QK_CAT = KV_LORA + 128
N_MEM = 256
MEM_H = 4
MEM_HD = 64
MEM_DIM = MEM_H * MEM_HD
D_FF = 2816
N_EXP = 8
D_FF_E = 1408
MOE_TILE = 1024
MOE_ROWS = 128
EPS = 1e-6

LANES = 128
VMEM_LIMIT = 56 * 1024 * 1024
HI = lax.Precision.HIGHEST


def _cp(sem, vmem=VMEM_LIMIT):
    return pltpu.CompilerParams(dimension_semantics=sem, vmem_limit_bytes=vmem)


def _row_tile(m, pref=512):
    return pref if m % pref == 0 else m


def _rms(x, g):
    return x * lax.rsqrt(jnp.mean(x * x, -1, keepdims=True) + EPS) * g


def _silu(x):
    return x * jax.nn.sigmoid(x)


def _softplus(x):
    return jnp.maximum(x, 0.0) + jnp.log1p(jnp.exp(-jnp.abs(x)))


def _dot(a, b):
    return jnp.dot(a, b, preferred_element_type=F32)


def _mm(a, w):
    if w.dtype == F32:
        return jnp.dot(a.astype(F32), w, precision=HI, preferred_element_type=F32)
    return jnp.dot(a.astype(BF16), w, preferred_element_type=F32)


def _dot_t(a, b):
    return lax.dot_general(a, b, (((1,), (1,)), ((), ())), preferred_element_type=F32)


def _softmax_rows(s):
    m = jnp.max(s, -1, keepdims=True)
    e = jnp.exp(s - m)
    return e / jnp.sum(e, -1, keepdims=True)


def _norm_proj_kernel(x_ref, g_ref, w_ref, *out_refs, splits):
    h = _rms(x_ref[...], g_ref[...]).astype(w_ref.dtype)
    off = 0
    for o_ref, n in zip(out_refs, splits):
        o_ref[...] = _mm(h, w_ref[:, off:off + n]).astype(o_ref.dtype)
        off += n


def norm_proj(x, g, w, layer, splits, dtypes, name):
    m, k = x.shape
    tm = _row_tile(m)
    n = w.shape[2]
    return pl.pallas_call(
        functools.partial(_norm_proj_kernel, splits=splits),
        grid=(m // tm,),
        in_specs=[pl.BlockSpec((tm, k), lambda i: (i, 0)),
                  pl.BlockSpec((1, k), lambda i: (0, 0)),
                  pl.BlockSpec((None, k, n), lambda i: (layer, 0, 0))],
        out_specs=[pl.BlockSpec((tm, s), lambda i: (i, 0)) for s in splits],
        out_shape=[jax.ShapeDtypeStruct((m, s), d) for s, d in zip(splits, dtypes)],
        compiler_params=_cp(("parallel",)),
        name=name,
    )(x, g.reshape(1, k), w)


def _mem_kv_kernel(m_ref, g_ref, w_ref, k_ref, v_ref):
    x = m_ref[...]
    mn = x * lax.rsqrt(jnp.mean(x * x, -1, keepdims=True) + EPS)
    kv = _dot((mn * g_ref[0]).astype(BF16), w_ref[0])
    k_ref[0] = kv[:, :MEM_DIM]
    v_ref[0] = kv[:, MEM_DIM:]


def mem_kv(mem, g_mem, w_mem_kv_bf):
    m = mem.shape[0]
    tm = _row_tile(m)
    nl = g_mem.shape[0]
    out = jax.ShapeDtypeStruct((nl, m, MEM_DIM), F32)
    return pl.pallas_call(
        _mem_kv_kernel,
        grid=(m // tm, nl),
        in_specs=[pl.BlockSpec((tm, D_MODEL), lambda i, l: (i, 0)),
                  pl.BlockSpec((1, 1, D_MODEL), lambda i, l: (l, 0, 0)),
                  pl.BlockSpec((1, D_MODEL, 2 * MEM_DIM), lambda i, l: (l, 0, 0))],
        out_specs=[pl.BlockSpec((1, tm, MEM_DIM), lambda i, l: (l, i, 0))] * 2,
        out_shape=[out, out],
        compiler_params=_cp(("parallel", "arbitrary")),
        name="mem_kv",
    )(mem, g_mem.reshape(nl, 1, D_MODEL), w_mem_kv_bf)


def _bmm(a, b):
    return jnp.einsum('hij,hjk->hik', a, b, preferred_element_type=F32)


def _bmm_t(a, b):
    return jnp.einsum('hid,hjd->hij', a, b, preferred_element_type=F32)


def _tri_inverse_minus_eye(lmat, row, col):
    def same_block(bits):
        return jnp.right_shift(row, bits) == jnp.right_shift(col, bits)

    l1 = jnp.where(same_block(4), lmat, 0.0)
    l1b = l1.astype(BF16)
    l2 = _bmm(l1b, l1b)
    l2b = l2.astype(BF16)
    l4 = _bmm(l2b, l2b)
    l4b = l4.astype(BF16)
    l8 = _bmm(l4b, l4b)
    q = -l1
    q = q + l2 + _bmm(q.astype(BF16), l2b)
    q = q + l4 + _bmm(q.astype(BF16), l4b)
    q = q + l8 + _bmm(q.astype(BF16), l8.astype(BF16))
    bits = 4
    while (1 << bits) < DN_CHUNK:
        cross = jnp.logical_and(same_block(bits + 1), jnp.logical_not(same_block(bits)))
        c = jnp.where(cross, lmat, 0.0)
        qb = q.astype(BF16)
        y = c + _bmm(qb, c.astype(BF16))
        q = q - (y + _bmm(y.astype(BF16), qb))
        bits += 1
    return q


def _gdn_prompt_kernel(u_ref, z_ref, ba_ref, w_ref, gate_ref, gon_ref, o_ref, s_out_ref,
                       halo_ref, s_scr):
    c = pl.program_id(1)
    C = DN_CHUNK

    @pl.when(c == 0)
    def _():
        halo_ref[...] = jnp.zeros_like(halo_ref)
        s_scr[...] = jnp.zeros_like(s_scr)

    u = u_ref[0].astype(F32)
    wc = w_ref[...]
    ext = jnp.concatenate([halo_ref[...], u], axis=0)
    y = u * wc[CONV_W - 1:CONV_W]
    for j in range(1, CONV_W):
        y = y + pltpu.roll(ext, j, axis=0)[8:] * wc[CONV_W - 1 - j:CONV_W - j]
    halo_ref[...] = u[C - 8:]
    y = _silu(y)

    ba = ba_ref[0]
    beta_all = jax.nn.sigmoid(ba)
    g_all = -jnp.exp(gate_ref[0:1]) * _softplus(ba + gate_ref[1:2])
    row = lax.broadcasted_iota(jnp.int32, (C, C), 0)
    col = lax.broadcasted_iota(jnp.int32, (C, C), 1)
    tri = row >= col
    gcum_all = jnp.dot(tri.astype(F32), g_all, precision=HI, preferred_element_type=F32)
    gcum_t = gcum_all.T

    heads = range(DN_H)
    per_head = lambda a, off: jnp.stack([a[:, off + h * DN_D:off + (h + 1) * DN_D] for h in heads])
    beta = jnp.stack([beta_all[:, h:h + 1] for h in heads])
    gc = jnp.stack([gcum_all[:, DN_H + h:DN_H + h + 1] for h in heads])
    gr = jnp.stack([gcum_t[DN_H + h:DN_H + h + 1, :] for h in heads])
    gl = gr[:, :, C - 1:C]
    qh = per_head(y, 0)
    kh = per_head(y, DN_QK)
    vh = per_head(y, 2 * DN_QK)
    qh = qh * lax.rsqrt(jnp.sum(qh * qh, -1, keepdims=True) + EPS) * (DN_D ** -0.5)
    kh = kh * lax.rsqrt(jnp.sum(kh * kh, -1, keepdims=True) + EPS)
    decay = jnp.exp(jnp.where(tri, gc - gr, -jnp.inf))
    kb = kh * beta
    vb = vh * beta
    k_bf = kh.astype(BF16)
    lmat = jnp.where(row > col, _bmm_t(kb.astype(BF16), k_bf) * decay, 0.0)
    qinv = _tri_inverse_minus_eye(lmat, row, col)
    egc = jnp.exp(gc)
    rhs = jnp.concatenate([vb, kb * egc], axis=-1)
    sol = rhs + _bmm(qinv.astype(BF16), rhs.astype(BF16))
    un = sol[:, :, :DN_D]
    wn = sol[:, :, DN_D:]
    qk = _bmm_t(qh.astype(BF16), k_bf) * decay
    qg = qh * egc
    kg = (kh * jnp.exp(gl - gc)).astype(BF16)
    st = s_scr[...]
    ws = _bmm(jnp.concatenate([wn, qg], axis=1).astype(BF16), st.astype(BF16))
    v_new = un - ws[:, :C]
    vn_bf = v_new.astype(BF16)
    o = ws[:, C:] + _bmm(qk.astype(BF16), vn_bf)
    st = st * jnp.exp(gl)
    on = _rms(o, gon_ref[...])
    for h in heads:
        s_scr[h] = st[h] + lax.dot_general(
            kg[h], vn_bf[h], (((0,), (0,)), ((), ())), preferred_element_type=F32)
        zh = z_ref[0, :, h * DN_D:(h + 1) * DN_D].astype(F32)
        o_ref[0, :, h * DN_D:(h + 1) * DN_D] = (on[h] * _silu(zh)).astype(o_ref.dtype)

    @pl.when(c == pl.num_programs(1) - 1)
    def _():
        s_out_ref[0] = s_scr[...]


def gdn_prompt(qkv, z, ba, conv_w, gate_rows, g_onorm):
    b, s, _ = qkv.shape
    C = DN_CHUNK
    blk = lambda n: pl.BlockSpec((1, C, n), lambda bi, c: (bi, c, 0))
    const = lambda a: pl.BlockSpec(a.shape, lambda bi, c: (0, 0))
    g_onorm = g_onorm.reshape(1, DN_D)
    return pl.pallas_call(
        _gdn_prompt_kernel,
        grid=(b, s // C),
        in_specs=[blk(CONV_DIM), blk(DN_QK), blk(LANES), const(conv_w), const(gate_rows),
                  const(g_onorm)],
        out_specs=[blk(DN_QK),
                   pl.BlockSpec((1, DN_H, DN_D, DN_D), lambda bi, c: (bi, 0, 0, 0))],
        out_shape=[jax.ShapeDtypeStruct((b, s, DN_QK), BF16),
                   jax.ShapeDtypeStruct((b, DN_H, DN_D, DN_D), F32)],
        scratch_shapes=[pltpu.VMEM((8, CONV_DIM), F32), pltpu.VMEM((DN_H, DN_D, DN_D), F32)],
        compiler_params=_cp(("parallel", "arbitrary")),
        name="gdn_prompt",
    )(qkv, z, ba, conv_w, gate_rows, g_onorm)


def _gdn_step_kernel(u_ref, z_ref, ba_ref, cs_ref, s_ref, w_ref, gate_ref, gon_ref,
                     o_ref, cs_out_ref, s_out_ref):
    u = u_ref[0].astype(F32)
    prev = cs_ref[0]
    w = w_ref[...]
    y = u * w[CONV_W - 1:CONV_W]
    for j in range(CONV_W - 1):
        y = y + prev[j:j + 1] * w[j:j + 1]
    y = _silu(y)
    cs_out_ref[0, 0:CONV_W - 2, :] = cs_ref[0, 1:CONV_W - 1, :]
    cs_out_ref[0, CONV_W - 2:CONV_W - 1, :] = u
    ba = ba_ref[0]
    beta_all = jax.nn.sigmoid(ba)
    g_all = -jnp.exp(gate_ref[0:1]) * _softplus(ba + gate_ref[1:2])
    eye = (lax.broadcasted_iota(jnp.int32, (DN_D, DN_D), 0)
           == lax.broadcasted_iota(jnp.int32, (DN_D, DN_D), 1))

    def to_col(r):
        return jnp.sum(jnp.where(eye, jnp.broadcast_to(r, (DN_D, DN_D)), 0.0), -1, keepdims=True)

    for h in range(DN_H):
        beta = beta_all[:, h:h + 1]
        eg = jnp.exp(g_all[:, DN_H + h:DN_H + h + 1])
        qh = y[:, h * DN_D:(h + 1) * DN_D]
        kh = y[:, DN_QK + h * DN_D:DN_QK + (h + 1) * DN_D]
        vh = y[:, 2 * DN_QK + h * DN_D:2 * DN_QK + (h + 1) * DN_D]
        qh = qh * lax.rsqrt(jnp.sum(qh * qh, -1, keepdims=True) + EPS) * (DN_D ** -0.5)
        kh = kh * lax.rsqrt(jnp.sum(kh * kh, -1, keepdims=True) + EPS)
        st = s_ref[0, h]
        kcol = to_col(kh)
        qcol = to_col(qh)
        ks = jnp.sum(kcol * st, 0, keepdims=True)
        qs = jnp.sum(qcol * st, 0, keepdims=True)
        v_new = beta * vh - (beta * eg) * ks
        o = eg * qs + jnp.sum(qh * kh, -1, keepdims=True) * v_new
        s_out_ref[0, h] = st * eg + kcol * v_new
        zh = z_ref[0, :, h * DN_D:(h + 1) * DN_D].astype(F32)
        o_ref[0, :, h * DN_D:(h + 1) * DN_D] = (_rms(o, gon_ref[...]) * _silu(zh)).astype(o_ref.dtype)


def gdn_step(qkv, z, ba, conv_state, s_state, layer, conv_w, gate_rows, g_onorm):
    nb = qkv.shape[0]
    i3 = lambda bi: (bi, 0, 0)
    return pl.pallas_call(
        _gdn_step_kernel,
        grid=(nb,),
        in_specs=[pl.BlockSpec((1, 1, CONV_DIM), i3), pl.BlockSpec((1, 1, DN_QK), i3),
                  pl.BlockSpec((1, 1, LANES), i3),
                  pl.BlockSpec((None, 1, CONV_W - 1, CONV_DIM), lambda bi: (layer, bi, 0, 0)),
                  pl.BlockSpec((None, 1, DN_H, DN_D, DN_D), lambda bi: (layer, bi, 0, 0, 0)),
                  pl.BlockSpec((CONV_W, CONV_DIM), lambda bi: (0, 0)),
                  pl.BlockSpec((2, LANES), lambda bi: (0, 0)),
                  pl.BlockSpec((1, DN_D), lambda bi: (0, 0))],
        out_specs=[pl.BlockSpec((1, 1, DN_QK), i3), pl.BlockSpec((1, CONV_W - 1, CONV_DIM), i3),
                   pl.BlockSpec((1, DN_H, DN_D, DN_D), lambda bi: (bi, 0, 0, 0))],
        out_shape=[jax.ShapeDtypeStruct((nb, 1, DN_QK), F32),
                   jax.ShapeDtypeStruct((nb, CONV_W - 1, CONV_DIM), F32),
                   jax.ShapeDtypeStruct((nb, DN_H, DN_D, DN_D), F32)],
        compiler_params=_cp(("parallel",)),
        name="gdn_step",
    )(qkv.reshape(nb, 1, CONV_DIM), z.reshape(nb, 1, DN_QK), ba.reshape(nb, 1, LANES),
      conv_state, s_state, conv_w, gate_rows, g_onorm.reshape(1, DN_D))


def _mem_attend_rows(mq, mk, mv):
    lane_head = lax.broadcasted_iota(jnp.int32, (1, MEM_DIM), 1) // MEM_HD
    out = jnp.zeros((mq.shape[0], MEM_DIM), F32)
    for h in range(MEM_H):
        sel = lane_head == h
        s = _dot_t(mq, jnp.where(sel, mk, 0.0).astype(BF16)) * (MEM_HD ** -0.5)
        p = _softmax_rows(s).astype(BF16)
        out = out + _dot(p, jnp.where(sel, mv, 0.0).astype(BF16))
    return out


def _mix_out_kernel(x_ref, ot_ref, mq_ref, mk_ref, mv_ref, w_ref, o_ref):
    om = _mem_attend_rows(mq_ref[0], mk_ref[0], mv_ref[0]).astype(BF16)
    nt = ot_ref.shape[-1]
    o_ref[0] = x_ref[0] + _dot(ot_ref[0], w_ref[:nt]) + _dot(om, w_ref[nt:])


def mix_out(x, o_tok, mq, mk, mv, w_out_bf, layer, tm=512):
    b, s, _ = x.shape
    nt = o_tok.shape[-1]
    blk = lambda n: pl.BlockSpec((1, tm, n), lambda bi, i: (bi, i, 0))
    kv = pl.BlockSpec((None, 1, N_MEM, MEM_DIM), lambda bi, i: (layer, bi, 0, 0))
    return pl.pallas_call(
        _mix_out_kernel,
        grid=(b, s // tm),
        in_specs=[blk(D_MODEL), blk(nt), blk(MEM_DIM), kv, kv,
                  pl.BlockSpec((None, nt + MEM_DIM, D_MODEL), lambda bi, i: (layer, 0, 0))],
        out_specs=blk(D_MODEL),
        out_shape=jax.ShapeDtypeStruct((b, s, D_MODEL), F32),
        compiler_params=_cp(("parallel", "parallel")),
        name="mix_out",
    )(x, o_tok, mq, mk, mv, w_out_bf)


def _mem_attend_step_kernel(mq_ref, mk_ref, mv_ref, o_ref, *, precise):
    lane_head = lax.broadcasted_iota(jnp.int32, (8, MEM_DIM), 1) // MEM_HD
    rowi = lax.broadcasted_iota(jnp.int32, (8, MEM_DIM), 0)
    sel = lane_head == rowi
    q8 = jnp.where(sel, jnp.broadcast_to(mq_ref[0], (8, MEM_DIM)), 0.0)
    kt = mk_ref[0]
    vt = mv_ref[0]
    if precise:
        s = _mm(q8, kt) * (MEM_HD ** -0.5)
        o8 = lax.dot_general(_softmax_rows(s), vt, (((1,), (1,)), ((), ())), precision=HI,
                             preferred_element_type=F32)
    else:
        s = _dot(q8.astype(BF16), kt.astype(BF16)) * (MEM_HD ** -0.5)
        o8 = _dot_t(_softmax_rows(s).astype(BF16), vt.astype(BF16))
    o_ref[0] = jnp.sum(jnp.where(sel, o8, 0.0), 0, keepdims=True).astype(o_ref.dtype)


def mem_attend_step(mq, mk_t, mv_t, layer, precise):
    nb = mq.shape[0]
    i3 = lambda bi: (bi, 0, 0)
    kv = pl.BlockSpec((None, 1, MEM_DIM, N_MEM), lambda bi: (layer, bi, 0, 0))
    return pl.pallas_call(
        functools.partial(_mem_attend_step_kernel, precise=precise),
        grid=(nb,),
        in_specs=[pl.BlockSpec((1, 1, MEM_DIM), i3), kv, kv],
        out_specs=pl.BlockSpec((1, 1, MEM_DIM), i3),
        out_shape=jax.ShapeDtypeStruct((nb, 1, MEM_DIM), F32),
        compiler_params=_cp(("parallel",)),
        name="mem_attend_step",
    )(mq.reshape(nb, 1, MEM_DIM), mk_t, mv_t)


def _out_proj_kernel(x_ref, ot_ref, om_ref, w_ref, o_ref):
    nt = ot_ref.shape[-1]
    o_ref[...] = x_ref[...] + _mm(ot_ref[...], w_ref[:nt]) + _mm(om_ref[...], w_ref[nt:])


def out_proj(x, o_tok, o_mem, w_out, layer):
    m = x.shape[0]
    nt = o_tok.shape[-1]
    full = lambda a: pl.BlockSpec(a.shape, lambda i: (0, 0))
    return pl.pallas_call(
        _out_proj_kernel,
        grid=(1,),
        in_specs=[full(x), full(o_tok), full(o_mem),
                  pl.BlockSpec((None,) + w_out.shape[1:], lambda i: (layer, 0, 0))],
        out_specs=pl.BlockSpec((m, D_MODEL), lambda i: (0, 0)),
        out_shape=jax.ShapeDtypeStruct((m, D_MODEL), F32),
        compiler_params=_cp(("arbitrary",)),
        name="out_proj",
    )(x, o_tok, o_mem, w_out)


def _ffn_kernel(x_ref, g_ref, wg_ref, wu_ref, wd_ref, o_ref, h_scr, acc_scr):
    f = pl.program_id(1)

    @pl.when(f == 0)
    def _():
        x = x_ref[...]
        h_scr[...] = _rms(x, g_ref[...]).astype(h_scr.dtype)
        acc_scr[...] = x

    h = h_scr[...]
    t = _silu(_mm(h, wg_ref[...])) * _mm(h, wu_ref[...])
    acc_scr[...] += _mm(t, wd_ref[...])

    @pl.when(f == pl.num_programs(1) - 1)
    def _():
        o_ref[...] = acc_scr[...]


def ffn(x, g, wg, wu, wd, layer, tf=1408):
    m = x.shape[0]
    tm = _row_tile(m)
    return pl.pallas_call(
        _ffn_kernel,
        grid=(m // tm, D_FF // tf),
        in_specs=[pl.BlockSpec((tm, D_MODEL), lambda i, f: (i, 0)),
                  pl.BlockSpec((1, D_MODEL), lambda i, f: (0, 0)),
                  pl.BlockSpec((None, D_MODEL, tf), lambda i, f: (layer, 0, f)),
                  pl.BlockSpec((None, D_MODEL, tf), lambda i, f: (layer, 0, f)),
                  pl.BlockSpec((None, tf, D_MODEL), lambda i, f: (layer, f, 0))],
        out_specs=pl.BlockSpec((tm, D_MODEL), lambda i, f: (i, 0)),
        out_shape=jax.ShapeDtypeStruct((m, D_MODEL), F32),
        scratch_shapes=[pltpu.VMEM((tm, D_MODEL), wg.dtype), pltpu.VMEM((tm, D_MODEL), F32)],
        compiler_params=_cp(("parallel", "arbitrary")),
        name="ffn",
    )(x, g.reshape(1, D_MODEL), wg, wu, wd)


def _top2_gates(logits):
    lane = lax.broadcasted_iota(jnp.int32, logits.shape, 1)
    probs = _softmax_rows(logits)
    p1 = jnp.max(probs, -1, keepdims=True)
    i1 = jnp.min(jnp.where(probs == p1, lane, LANES), -1, keepdims=True)
    m1 = lane == i1
    rest = jnp.where(m1, -1.0, probs)
    p2 = jnp.max(rest, -1, keepdims=True)
    i2 = jnp.min(jnp.where(rest == p2, lane, LANES), -1, keepdims=True)
    m2 = lane == i2
    tot = p1 + p2
    return jnp.where(m1, p1 / tot, 0.0) + jnp.where(m2, p2 / tot, 0.0)


def _moe_kernel(x_ref, g_ref, wr_ref, wg_ref, wu_ref, wd_ref, gf_ref, o_ref,
                h_scr, acc_scr, gate_scr, *, final_norm):
    e = pl.program_id(1)

    @pl.when(e == 0)
    def _():
        x = x_ref[...]
        h = _rms(x, g_ref[...])
        h_scr[...] = h.astype(BF16)
        acc_scr[...] = x
        logits = jnp.dot(h, wr_ref[...], precision=HI, preferred_element_type=F32)
        lane = lax.broadcasted_iota(jnp.int32, logits.shape, 1)
        gate_scr[...] = _top2_gates(jnp.where(lane < N_EXP, logits, -jnp.inf))

    h = h_scr[...]
    lane = lax.broadcasted_iota(jnp.int32, gate_scr.shape, 1)
    gate = jnp.sum(jnp.where(lane == e, gate_scr[...], 0.0), -1, keepdims=True)
    t = (_silu(_dot(h, wg_ref[...])) * _dot(h, wu_ref[...])).astype(BF16)
    acc_scr[...] += gate * _dot(t, wd_ref[...])

    @pl.when(e == pl.num_programs(1) - 1)
    def _():
        y = acc_scr[...]
        o_ref[...] = _rms(y, gf_ref[...]) if final_norm else y


def moe(x, g, w_router_pad, wg, wu, wd, g_final, final_norm, layer):
    m = x.shape[0]
    tm = _row_tile(m)
    return pl.pallas_call(
        functools.partial(_moe_kernel, final_norm=final_norm),
        grid=(m // tm, N_EXP),
        in_specs=[pl.BlockSpec((tm, D_MODEL), lambda i, e: (i, 0)),
                  pl.BlockSpec((1, D_MODEL), lambda i, e: (0, 0)),
                  pl.BlockSpec((D_MODEL, LANES), lambda i, e: (0, 0)),
                  pl.BlockSpec((None, None, D_MODEL, D_FF_E), lambda i, e: (layer, e, 0, 0)),
                  pl.BlockSpec((None, None, D_MODEL, D_FF_E), lambda i, e: (layer, e, 0, 0)),
                  pl.BlockSpec((None, None, D_FF_E, D_MODEL), lambda i, e: (layer, e, 0, 0)),
                  pl.BlockSpec((1, D_MODEL), lambda i, e: (0, 0))],
        out_specs=pl.BlockSpec((tm, D_MODEL), lambda i, e: (i, 0)),
        out_shape=jax.ShapeDtypeStruct((m, D_MODEL), F32),
        scratch_shapes=[pltpu.VMEM((tm, D_MODEL), BF16), pltpu.VMEM((tm, D_MODEL), F32),
                        pltpu.VMEM((tm, LANES), F32)],
        compiler_params=_cp(("parallel", "arbitrary")),
        name="moe",
    )(x, g.reshape(1, D_MODEL), w_router_pad, wg, wu, wd, g_final.reshape(1, D_MODEL))


def _moe_routed_kernel(x_ref, g_ref, wr_ref, wg_ref, wu_ref, wd_ref, gf_ref, o_ref,
                       h_scr, acc_scr, rank_scr, gate_t_scr, rank_t_scr, cnt_scr,
                       *, final_norm, rb):
    e = pl.program_id(1)
    tm = x_ref.shape[0]
    sub = 256

    @pl.when(e == 0)
    def _():
        x = x_ref[...]
        h = _rms(x, g_ref[...])
        h_scr[...] = h.astype(BF16)
        acc_scr[...] = x
        logits = jnp.dot(h, wr_ref[...], precision=HI, preferred_element_type=F32)
        lane = lax.broadcasted_iota(jnp.int32, logits.shape, 1)
        gates = _top2_gates(jnp.where(lane < N_EXP, logits, -jnp.inf))
        routed = gates > 0.0
        hit = jnp.where(routed, 1.0, 0.0).astype(BF16)
        col = lax.broadcasted_iota(jnp.int32, (sub, tm), 1)
        for r0 in range(0, tm, sub):
            row = r0 + lax.broadcasted_iota(jnp.int32, (sub, tm), 0)
            before = jnp.where(col < row, 1.0, 0.0).astype(BF16)
            rank = _dot(before, hit)
            rank_scr[r0:r0 + sub] = jnp.where(routed[r0:r0 + sub], rank, -1.0)
        gate_t_scr[...] = gates.T
        rank_t_scr[...] = rank_scr[...].T
        cnt_scr[...] = jnp.broadcast_to(jnp.sum(hit.astype(F32), 0, keepdims=True), cnt_scr.shape)

    lane = lax.broadcasted_iota(jnp.int32, (tm, LANES), 1)
    rank_col = jnp.sum(jnp.where(lane == e, rank_scr[...], 0.0), -1, keepdims=True)
    rank_row = rank_t_scr[pl.ds(e, 1), :]
    gate_row = gate_t_scr[pl.ds(e, 1), :]
    lane1 = lax.broadcasted_iota(jnp.int32, (1, LANES), 1)
    count = jnp.sum(jnp.where(lane1 == e, cnt_scr[0:1], 0.0)).astype(jnp.int32)

    def block(first_rank, rows):
        base = first_rank.astype(F32)
        pick = rank_row == base + lax.broadcasted_iota(jnp.int32, (rows, tm), 0).astype(F32)
        hg = _dot(jnp.where(pick, 1.0, 0.0).astype(BF16), h_scr[...]).astype(BF16)
        gate = jnp.sum(jnp.where(pick, gate_row, 0.0), -1, keepdims=True)
        t = (_silu(_dot(hg, wg_ref[...])) * _dot(hg, wu_ref[...])).astype(BF16)
        y = (gate * _dot(t, wd_ref[...])).astype(BF16)
        put = rank_col == base + lax.broadcasted_iota(jnp.int32, (tm, rows), 1).astype(F32)
        acc_scr[...] += _dot(jnp.where(put, 1.0, 0.0).astype(BF16), y)

    def full_block(j, carry):
        block(j * rb, rb)
        return carry

    lax.fori_loop(0, (count + rb - 1) // rb, full_block, 0)

    @pl.when(e == pl.num_programs(1) - 1)
    def _():
        y = acc_scr[...]
        o_ref[...] = _rms(y, gf_ref[...]) if final_norm else y


def moe_routed(x, g, w_router_pad, wg, wu, wd, g_final, final_norm, layer, tm=MOE_TILE,
               rb=MOE_ROWS):
    m = x.shape[0]
    return pl.pallas_call(
        functools.partial(_moe_routed_kernel, final_norm=final_norm, rb=rb),
        grid=(m // tm, N_EXP),
        in_specs=[pl.BlockSpec((tm, D_MODEL), lambda i, e: (i, 0)),
                  pl.BlockSpec((1, D_MODEL), lambda i, e: (0, 0)),
                  pl.BlockSpec((D_MODEL, LANES), lambda i, e: (0, 0)),
                  pl.BlockSpec((None, None, D_MODEL, D_FF_E), lambda i, e: (layer, e, 0, 0)),
                  pl.BlockSpec((None, None, D_MODEL, D_FF_E), lambda i, e: (layer, e, 0, 0)),
                  pl.BlockSpec((None, None, D_FF_E, D_MODEL), lambda i, e: (layer, e, 0, 0)),
                  pl.BlockSpec((1, D_MODEL), lambda i, e: (0, 0))],
        out_specs=pl.BlockSpec((tm, D_MODEL), lambda i, e: (i, 0)),
        out_shape=jax.ShapeDtypeStruct((m, D_MODEL), F32),
        scratch_shapes=[pltpu.VMEM((tm, D_MODEL), BF16), pltpu.VMEM((tm, D_MODEL), F32),
                        pltpu.VMEM((tm, LANES), F32),
                        pltpu.VMEM((LANES, tm), F32), pltpu.VMEM((LANES, tm), F32),
                        pltpu.VMEM((8, LANES), F32)],
        compiler_params=_cp(("parallel", "arbitrary")),
        name="moe_routed",
    )(x, g.reshape(1, D_MODEL), w_router_pad, wg, wu, wd, g_final.reshape(1, D_MODEL))


def _latent_kv_kernel(x_ref, g_ref, w_ref, gc_ref, cos_ref, sin_ref, ckv_ref, kpe_ref, kc_ref,
                      *maybe_kt_ref):
    h = _rms(x_ref[0], g_ref[...]).astype(BF16)
    kv = _dot(h, w_ref[...])
    ckv = _rms(kv[:, :KV_LORA], gc_ref[...])
    pe = kv[:, KV_LORA:KV_LORA + LANES] * cos_ref[...] + kv[:, KV_LORA + LANES:] * sin_ref[...]
    ckv_ref[0] = ckv
    kpe_ref[0] = pe[:, :QK_ROPE]
    kc = jnp.concatenate([ckv, pe], axis=-1)
    kc_ref[0] = kc.astype(BF16)
    for kt_ref in maybe_kt_ref:
        kt_ref[0] = kc.T.astype(BF16)


def latent_kv(x, g_kv, w_kv_bf, g_ckv, cos_t, sin_t, with_transposed):
    b, s, _ = x.shape
    tm = _row_tile(s)
    blk = lambda n: pl.BlockSpec((1, tm, n), lambda bi, i: (bi, i, 0))
    tab = pl.BlockSpec((tm, LANES), lambda bi, i: (i, 0))
    const = lambda a: pl.BlockSpec(a.shape, lambda bi, i: (0, 0))
    g_kv = g_kv.reshape(1, D_MODEL)
    g_ckv = g_ckv.reshape(1, KV_LORA)
    out_specs = [blk(KV_LORA), blk(QK_ROPE), blk(QK_CAT)]
    out_shape = [jax.ShapeDtypeStruct((b, s, KV_LORA), F32),
                 jax.ShapeDtypeStruct((b, s, QK_ROPE), F32),
                 jax.ShapeDtypeStruct((b, s, QK_CAT), BF16)]
    if with_transposed:
        out_specs.append(pl.BlockSpec((1, QK_CAT, tm), lambda bi, i: (bi, 0, i)))
        out_shape.append(jax.ShapeDtypeStruct((b, QK_CAT, s), BF16))
    return pl.pallas_call(
        _latent_kv_kernel,
        grid=(b, s // tm),
        in_specs=[blk(D_MODEL), const(g_kv), const(w_kv_bf), const(g_ckv), tab, tab],
        out_specs=out_specs,
        out_shape=out_shape,
        compiler_params=_cp(("parallel", "parallel")),
        name="latent_kv",
    )(x, g_kv, w_kv_bf, g_ckv, cos_t, sin_t)


def _mla_q_kernel(x_ref, g_ref, win_ref, gq_ref, wqb_ref, wuk_ref, cos_ref, sin_ref,
                  q_ref, mq_ref):
    h = _rms(x_ref[0], g_ref[...]).astype(BF16)
    proj = _dot(h, win_ref[...])
    mq_ref[0] = proj[:, Q_LORA:].astype(mq_ref.dtype)
    ql = _rms(proj[:, :Q_LORA], gq_ref[...]).astype(BF16)
    nh = MLA_H * QK_NOPE
    for hd in range(MLA_H):
        sl = slice(hd * LANES, (hd + 1) * LANES)
        nope = _dot(ql, wqb_ref[:, sl])
        pa = _dot(ql, wqb_ref[:, nh + hd * LANES:nh + (hd + 1) * LANES])
        pb = _dot(ql, wqb_ref[:, 2 * nh + hd * LANES:2 * nh + (hd + 1) * LANES])
        q_lat = _dot(nope.astype(BF16), wuk_ref[hd])
        q_pe = pa * cos_ref[...] + pb * sin_ref[...]
        q_ref[0, hd] = (jnp.concatenate([q_lat, q_pe], axis=-1)
                        * (MLA_SCALE * math.log2(math.e))).astype(q_ref.dtype)


def mla_q(x, g_mix, w_in_bf, g_ql, w_qb_bf, w_ukt_bf, cos_t, sin_t, layer):
    b, s, _ = x.shape
    tm = _row_tile(s)
    const2 = lambda a: pl.BlockSpec(a.shape, lambda bi, i: (0, 0))
    stacked = lambda a: pl.BlockSpec((None,) + a.shape[1:], lambda bi, i: (layer, 0, 0))
    tab = pl.BlockSpec((tm, LANES), lambda bi, i: (i, 0))
    g_mix = g_mix.reshape(1, D_MODEL)
    g_ql = g_ql.reshape(1, Q_LORA)
    return pl.pallas_call(
        _mla_q_kernel,
        grid=(b, s // tm),
        in_specs=[pl.BlockSpec((1, tm, D_MODEL), lambda bi, i: (bi, i, 0)),
                  const2(g_mix), stacked(w_in_bf), const2(g_ql), stacked(w_qb_bf),
                  pl.BlockSpec(w_ukt_bf.shape, lambda bi, i: (0, 0, 0)), tab, tab],
        out_specs=[pl.BlockSpec((1, MLA_H, tm, QK_CAT), lambda bi, i: (bi, 0, i, 0)),
                   pl.BlockSpec((1, tm, MEM_DIM), lambda bi, i: (bi, i, 0))],
        out_shape=[jax.ShapeDtypeStruct((b, MLA_H, s, QK_CAT), BF16),
                   jax.ShapeDtypeStruct((b, s, MEM_DIM), BF16)],
        compiler_params=_cp(("parallel", "parallel")),
        name="mla_q",
    )(x, g_mix, w_in_bf, g_ql, w_qb_bf, w_ukt_bf, cos_t, sin_t)


def _mla_prompt_kernel(qi_ref, ki_ref, q_ref, kt_ref, v_ref, wuv_ref, o_ref, m_scr, l_scr, acc_scr,
                       *, tq, tk):
    t = pl.program_id(1)
    qi = qi_ref[t]
    ki = ki_ref[t]
    reps = tk // LANES

    def wide(a, n):
        return jnp.concatenate([a] * n, axis=-1)

    @pl.when(ki == 0)
    def _():
        m_scr[...] = jnp.full_like(m_scr, -jnp.inf)
        l_scr[...] = jnp.zeros_like(l_scr)
        acc_scr[...] = jnp.zeros_like(acc_scr)

    def step(masked):
        kt = kt_ref[0]
        v = v_ref[0]
        rows = MLA_H * tq
        s = _dot(q_ref[0].reshape(rows, QK_CAT), kt)
        if masked:
            keep = (lax.broadcasted_iota(jnp.int32, (rows, tk), 1)
                    <= (lax.broadcasted_iota(jnp.int32, (rows, tk), 0) & (tq - 1)))
            s = jnp.where(keep, s, -jnp.inf)
        m_prev = m_scr[...]
        m_new = jnp.maximum(m_prev, jnp.max(s, -1, keepdims=True))
        alpha = jnp.exp2(m_prev - m_new)
        p = jnp.exp2(s - wide(m_new, reps))
        l_scr[...] = alpha * l_scr[...] + jnp.sum(p, -1, keepdims=True)
        acc_scr[...] = wide(alpha, KV_LORA // LANES) * acc_scr[...] + _dot(p.astype(BF16), v)
        m_scr[...] = m_new

    pl.when(ki < qi)(lambda: step(False))

    @pl.when(ki == qi)
    def _():
        step(True)
        for hd in range(MLA_H):
            rs = slice(hd * tq, (hd + 1) * tq)
            o_lat = (acc_scr[rs] / wide(l_scr[rs], KV_LORA // LANES)).astype(BF16)
            o_ref[0, :, hd * V_HEAD:(hd + 1) * V_HEAD] = _dot(o_lat, wuv_ref[hd]).astype(o_ref.dtype)


def mla_prompt_attend(q, kc, kc_t, w_uv_bf, tile=512):
    b, _, s, _ = q.shape
    tq = tk = tile
    n = s // tile
    pairs = [(qi, ki) for qi in range(n) for ki in range(qi + 1)]
    qi_tab = jnp.asarray([p[0] for p in pairs], jnp.int32)
    ki_tab = jnp.asarray([p[1] for p in pairs], jnp.int32)
    grid_spec = pltpu.PrefetchScalarGridSpec(
        num_scalar_prefetch=2,
        grid=(b, len(pairs)),
        in_specs=[pl.BlockSpec((1, MLA_H, tq, QK_CAT), lambda bi, t, qt, kt: (bi, 0, qt[t], 0)),
                  pl.BlockSpec((1, QK_CAT, tk), lambda bi, t, qt, kt: (bi, 0, kt[t])),
                  pl.BlockSpec((1, tk, KV_LORA), lambda bi, t, qt, kt: (bi, kt[t], 0)),
                  pl.BlockSpec(w_uv_bf.shape, lambda bi, t, qt, kt: (0, 0, 0))],
        out_specs=pl.BlockSpec((1, tq, MLA_H * V_HEAD), lambda bi, t, qt, kt: (bi, qt[t], 0)),
        scratch_shapes=[pltpu.VMEM((MLA_H * tq, LANES), F32), pltpu.VMEM((MLA_H * tq, LANES), F32),
                        pltpu.VMEM((MLA_H * tq, KV_LORA), F32)],
    )
    return pl.pallas_call(
        functools.partial(_mla_prompt_kernel, tq=tq, tk=tk),
        grid_spec=grid_spec,
        out_shape=jax.ShapeDtypeStruct((b, s, MLA_H * V_HEAD), BF16),
        compiler_params=_cp(("parallel", "arbitrary")),
        name="mla_prompt_attend",
    )(qi_tab, ki_tab, q, kc_t, kc, w_uv_bf)


def _mla_step_kernel(pt_ref, q_ref, knew_ref, wuv_ref, ckv_hbm, kpe_hbm, o_ref,
                     kbuf, pbuf, sem, m_scr, l_scr, acc_scr, *, pps):
    b = pl.program_id(0)
    j = pl.program_id(1)
    nj = pl.num_programs(1)
    t = b * nj + j
    slot = t % 2

    def page_copies(bi, ji, sl):
        cps = []
        for p in range(pps):
            pid = pt_ref[bi, ji * pps + p]
            cps.append(pltpu.make_async_copy(ckv_hbm.at[pid], kbuf.at[sl, p], sem.at[sl, 0]))
            cps.append(pltpu.make_async_copy(kpe_hbm.at[pid], pbuf.at[sl, p], sem.at[sl, 1]))
        return cps

    @pl.when(t == 0)
    def _():
        for cp in page_copies(b, j, slot):
            cp.start()

    @pl.when(t + 1 < pl.num_programs(0) * nj)
    def _():
        last = j == nj - 1
        for cp in page_copies(jnp.where(last, b + 1, b), jnp.where(last, 0, j + 1), 1 - slot):
            cp.start()

    for cp in page_copies(b, j, slot):
        cp.wait()

    q = q_ref[0]
    q_lat = q[:, :KV_LORA]
    q_pe = q[:, KV_LORA:KV_LORA + QK_ROPE]

    @pl.when(j == 0)
    def _():
        s0 = _dot_t(q, knew_ref[0])[:, 0:1]
        m_scr[...] = s0
        l_scr[...] = jnp.ones_like(l_scr)
        acc_scr[...] = jnp.broadcast_to(knew_ref[0][0:1, :KV_LORA].astype(F32), acc_scr.shape)

    ks = [kbuf[slot, p].astype(BF16) for p in range(pps)]
    ss = [_dot_t(q_lat, kb) + _dot(q_pe, pbuf[slot, p].astype(BF16)) for p, kb in enumerate(ks)]
    m_new = m_scr[...]
    for s in ss:
        m_new = jnp.maximum(m_new, jnp.max(s, -1, keepdims=True))
    alpha = jnp.exp2(m_scr[...] - m_new)
    l_new = alpha * l_scr[...]
    acc = alpha * acc_scr[...]
    for s, kb in zip(ss, ks):
        p = jnp.exp2(s - m_new)
        l_new = l_new + jnp.sum(p, -1, keepdims=True)
        acc = acc + _dot(p.astype(BF16), kb)
    m_scr[...] = m_new
    l_scr[...] = l_new
    acc_scr[...] = acc

    @pl.when(j == pl.num_programs(1) - 1)
    def _():
        o_lat = (acc_scr[...] / l_scr[...]).astype(BF16)
        full = _dot(o_lat, wuv_ref[...])
        rowi = lax.broadcasted_iota(jnp.int32, full.shape, 0)
        grp = lax.broadcasted_iota(jnp.int32, full.shape, 1) // V_HEAD
        o_ref[0] = jnp.sum(jnp.where(rowi == grp, full, 0.0), 0, keepdims=True).astype(o_ref.dtype)


def mla_step_attend(q8, k_new, w_uv_flat_bf, cache_ckv, cache_kpe_t, page_table, pps=64):
    nb, npg = page_table.shape
    steps = npg // pps
    grid_spec = pltpu.PrefetchScalarGridSpec(
        num_scalar_prefetch=1,
        grid=(nb, steps),
        in_specs=[pl.BlockSpec((1, 8, QK_CAT), lambda bi, j, pt: (bi, 0, 0)),
                  pl.BlockSpec((1, 8, QK_CAT), lambda bi, j, pt: (bi, 0, 0)),
                  pl.BlockSpec(w_uv_flat_bf.shape, lambda bi, j, pt: (0, 0)),
                  pl.BlockSpec(memory_space=pl.ANY), pl.BlockSpec(memory_space=pl.ANY)],
        out_specs=pl.BlockSpec((1, 1, MLA_H * V_HEAD), lambda bi, j, pt: (bi, 0, 0)),
        scratch_shapes=[pltpu.VMEM((2, pps, PAGE, KV_LORA), F32),
                        pltpu.VMEM((2, pps, QK_ROPE, PAGE), F32),
                        pltpu.SemaphoreType.DMA((2, 2)),
                        pltpu.VMEM((8, 1), F32), pltpu.VMEM((8, 1), F32),
                        pltpu.VMEM((8, KV_LORA), F32)],
    )
    return pl.pallas_call(
        functools.partial(_mla_step_kernel, pps=pps),
        grid_spec=grid_spec,
        out_shape=jax.ShapeDtypeStruct((nb, 1, MLA_H * V_HEAD), F32),
        compiler_params=_cp(("arbitrary", "arbitrary")),
        name="mla_step_attend",
    )(page_table, q8, k_new, w_uv_flat_bf, cache_ckv, cache_kpe_t)


def _swap_halves(w):
    half = w.shape[-1] // 2
    return jnp.concatenate([w[..., half:], w[..., :half]], -1)


def _pad_lanes(w, n=LANES):
    return jnp.pad(w, [(0, 0)] * (w.ndim - 1) + [(0, n - w.shape[-1])])


def _prep_weights(P):
    W = {}
    o1 = CONV_DIM
    o2 = o1 + DN_QK
    o3 = o2 + 2 * DN_H
    wa = P['w_in_a']
    W['in_a_f32'] = jnp.concatenate([wa[..., :o2], _pad_lanes(wa[..., o2:o3]), wa[..., o3:]], -1)
    W['in_a'] = W['in_a_f32'].astype(BF16)
    gate = jnp.zeros((N_A, 2, LANES), F32)
    gate = gate.at[:, 0, DN_H:2 * DN_H].set(P['a_log']).at[:, 1, DN_H:2 * DN_H].set(P['dt_bias'])
    W['gate_rows'] = gate
    W['in_b'] = P['w_in_b'].astype(BF16)
    wqb = P['w_q_b'].reshape(-1, Q_LORA, MLA_H, QK_NOPE + QK_ROPE)
    nope = wqb[..., :QK_NOPE].reshape(-1, Q_LORA, MLA_H * QK_NOPE)
    pe = wqb[..., QK_NOPE:]
    pa = _pad_lanes(pe).reshape(-1, Q_LORA, MLA_H * LANES)
    pb = _pad_lanes(_swap_halves(pe)).reshape(-1, Q_LORA, MLA_H * LANES)
    W['q_b'] = jnp.concatenate([nope, pa, pb], -1).astype(BF16)
    wkv = P['w_kv_a']
    kpe_w = wkv[:, KV_LORA:]
    W['kv_a'] = jnp.concatenate(
        [wkv[:, :KV_LORA], _pad_lanes(kpe_w), _pad_lanes(_swap_halves(kpe_w))], -1).astype(BF16)
    W['uk_t'] = jnp.transpose(P['w_uk'], (1, 2, 0)).astype(BF16)
    W['uv'] = jnp.transpose(P['w_uv'], (1, 0, 2)).astype(BF16)
    W['uv_flat'] = P['w_uv'].reshape(KV_LORA, MLA_H * V_HEAD).astype(BF16)
    W['mem_kv'] = P['w_mem_kv'].astype(BF16)
    W['out'] = P['w_out'].astype(BF16)
    W['gate'] = P['w_gate'].astype(BF16)
    W['up'] = P['w_up'].astype(BF16)
    W['down'] = P['w_down'].astype(BF16)
    W['router'] = _pad_lanes(P['w_router'])
    W['e_gate'] = P['we_gate'].astype(BF16)
    W['e_up'] = P['we_up'].astype(BF16)
    W['e_down'] = P['we_down'].astype(BF16)
    return W


def _rope_tables(pos):
    half = QK_ROPE // 2
    inv = ROPE_THETA ** (-jnp.arange(half, dtype=F32) / half)
    ang = pos.astype(F32)[:, None] * inv[None, :]
    cos = jnp.cos(ang)
    sin = jnp.sin(ang)
    zero = jnp.zeros((pos.shape[0], LANES - QK_ROPE), F32)
    return (jnp.concatenate([cos, cos, zero], -1), jnp.concatenate([-sin, sin, zero], -1))


def _channel_mixer(x2, l, P, W, precise=False):
    i = l // 2
    if l % 2 == 0:
        if precise:
            return ffn(x2, P['g_ffn'][l], P['w_gate'], P['w_up'], P['w_down'], i)
        return ffn(x2, P['g_ffn'][l], W['gate'], W['up'], W['down'], i)
    fn = moe_routed if x2.shape[0] % MOE_TILE == 0 else moe
    return fn(x2, P['g_ffn'][l], W['router'][i], W['e_gate'], W['e_up'], W['e_down'],
              P['g_final'], final_norm=(l == DEPTH - 1), layer=i)


def _prompt_trunk(x, mem_k, mem_v, P, W):
    b, s, _ = x.shape
    m = b * s
    cos_t, sin_t = _rope_tables(jnp.arange(s))
    conv_states, dn_states = [], []
    kc = kc_t = ckv = kpe = None
    for l in range(DEPTH):
        if l < N_A:
            qkv, z, ba, mq = norm_proj(x.reshape(m, D_MODEL), P['g_mix'][l], W['in_a'], l,
                                       (CONV_DIM, DN_QK, LANES, MEM_DIM), (BF16, BF16, F32, BF16),
                                       "in_proj_a")
            qkv = qkv.reshape(b, s, CONV_DIM)
            o_tok, s_new = gdn_prompt(qkv, z.reshape(b, s, DN_QK), ba.reshape(b, s, LANES),
                                      P['conv_w'][l], W['gate_rows'][l], P['g_onorm'][l])
            conv_states.append(qkv[:, s - (CONV_W - 1):, :].astype(F32))
            dn_states.append(s_new)
            mq = mq.reshape(b, s, MEM_DIM)
        else:
            j = l - N_A
            if l == N_A:
                ckv, kpe, kc, kc_t = latent_kv(x, P['g_kv'], W['kv_a'], P['g_ckv'], cos_t, sin_t,
                                               with_transposed=True)
            q, mq = mla_q(x, P['g_mix'][l], W['in_b'], P['g_qlora'][j], W['q_b'], W['uk_t'],
                          cos_t, sin_t, j)
            o_tok = mla_prompt_attend(q, kc, kc_t, W['uv'])
        x = mix_out(x, o_tok, mq, mem_k, mem_v, W['out'], l)
        x = _channel_mixer(x.reshape(m, D_MODEL), l, P, W).reshape(b, s, D_MODEL)
    return x, jnp.stack(conv_states), jnp.stack(dn_states), ckv, kpe


def _sample_trunk(x, mem_k_t, mem_v_t, conv_prev, dn_prev, cache_ckv, cache_kpe_t, page_table, P, W):
    nb = x.shape[0]
    past = page_table.shape[1] * PAGE
    cos_t, sin_t = _rope_tables(jnp.full((nb,), past, jnp.int32))
    x2 = x.reshape(nb, D_MODEL)
    conv_states, dn_states = [], []
    kc = ckv = kpe = None
    for l in range(DEPTH):
        if l < N_A:
            qkv, z, ba, mq = norm_proj(x2, P['g_mix'][l], W['in_a_f32'], l,
                                       (CONV_DIM, DN_QK, LANES, MEM_DIM), (F32, F32, F32, F32),
                                       "in_proj_a_step")
            o_tok, conv_new, s_new = gdn_step(qkv, z, ba, conv_prev, dn_prev, l, P['conv_w'][l],
                                              W['gate_rows'][l], P['g_onorm'][l])
            conv_states.append(conv_new)
            dn_states.append(s_new)
        else:
            j = l - N_A
            x3 = x2.reshape(1, nb, D_MODEL)
            if l == N_A:
                ckv, kpe, kc = latent_kv(x3, P['g_kv'], W['kv_a'], P['g_ckv'], cos_t, sin_t,
                                         with_transposed=False)
                ckv = ckv.reshape(nb, 1, KV_LORA)
                kpe = kpe.reshape(nb, 1, QK_ROPE)
                kc = jnp.broadcast_to(kc.reshape(nb, 1, QK_CAT), (nb, 8, QK_CAT))
            q, mq = mla_q(x3, P['g_mix'][l], W['in_b'], P['g_qlora'][j], W['q_b'], W['uk_t'],
                          cos_t, sin_t, j)
            q8 = jnp.pad(jnp.transpose(q[0], (1, 0, 2)), ((0, 0), (0, 8 - MLA_H), (0, 0)))
            o_tok = mla_step_attend(q8, kc, W['uv_flat'], cache_ckv, cache_kpe_t, page_table)
            mq = mq.reshape(nb, MEM_DIM).astype(F32)
        precise = l < N_A
        o_mem = mem_attend_step(mq, mem_k_t, mem_v_t, l, precise)
        x2 = out_proj(x2, o_tok.reshape(nb, -1), o_mem.reshape(nb, MEM_DIM),
                      P['w_out'] if precise else W['out'], l)
        x2 = _channel_mixer(x2, l, P, W, precise)
    return (x2.reshape(nb, 1, D_MODEL), jnp.stack(conv_states), jnp.stack(dn_states), ckv, kpe)


def kernel(x_prompt, x_sample, cache_mem_k, cache_mem_v, cache_ckv, cache_kpe, state_delta, state_conv,
           page_table, mem_prompt, g_mix, g_ffn, g_final, w_in_a, conv_w, a_log, dt_bias, g_onorm,
           w_in_b, g_qlora, w_q_b, g_kv, w_kv_a, g_ckv, w_uk, w_uv, g_mem, w_mem_kv, w_out,
           w_gate, w_up, w_down, w_router, we_gate, we_up, we_down):
    P = dict(g_mix=g_mix, g_ffn=g_ffn, g_final=g_final, w_in_a=w_in_a, conv_w=conv_w, a_log=a_log,
             dt_bias=dt_bias, g_onorm=g_onorm, w_in_b=w_in_b, g_qlora=g_qlora, w_q_b=w_q_b, g_kv=g_kv,
             w_kv_a=w_kv_a, g_ckv=g_ckv, w_uk=w_uk, w_uv=w_uv, w_mem_kv=w_mem_kv, w_out=w_out,
             w_gate=w_gate, w_up=w_up, w_down=w_down, w_router=w_router, we_gate=we_gate,
             we_up=we_up, we_down=we_down)
    W = _prep_weights(P)
    bp = x_prompt.shape[0]
    nl = g_mem.shape[0]
    mk, mv = mem_kv(mem_prompt.reshape(bp * N_MEM, D_MODEL), g_mem, W['mem_kv'])
    mk = mk.reshape(nl, bp, N_MEM, MEM_DIM)
    mv = mv.reshape(nl, bp, N_MEM, MEM_DIM)
    y_p, p_conv, p_delta, p_ckv, p_kpe = _prompt_trunk(x_prompt, mk, mv, P, W)
    nb = x_sample.shape[0]
    to_feat_tok = lambda c: jnp.transpose(c, (0, 1, 3, 4, 2)).reshape(nl, nb, MEM_DIM, N_MEM)
    y_s, s_conv, s_delta, s_ckv, s_kpe = _sample_trunk(
        x_sample, to_feat_tok(cache_mem_k), to_feat_tok(cache_mem_v), state_conv, state_delta,
        cache_ckv, jnp.swapaxes(cache_kpe, 1, 2), page_table, P, W)
    p_mem_k = mk.reshape(nl, bp, N_MEM, MEM_H, MEM_HD)
    p_mem_v = mv.reshape(nl, bp, N_MEM, MEM_H, MEM_HD)
    return (y_p, y_s, p_delta, p_conv, p_ckv, p_kpe, p_mem_k, p_mem_v, s_delta, s_conv, s_ckv, s_kpe)
```
